```python
import jax
import jax.numpy as jnp
from jax import lax
import numpy as np

D_MODEL = 2048
BATCH = 2
SEQ = 8192
DEPTH = 2

HEAD_DIM = 128
ROPE_THETA = 10000.0
NORM_EPS = 1e-6
Q_BLOCK = 128
GATHER_Q_BLOCK = 64

MLA_HEADS = 8
MLA_Q_RANK = 512
MLA_KV_RANK = 256
MLA_NOPE = 128
MLA_ROPE = 64
MLA_V = 128

DSA_HEADS = 8
IDX_HEADS = 16
IDX_DIM = 64
DSA_TOPK_MAX = 256

MOBA_HEADS = 8
MOBA_BLOCK = 256
MOBA_TOPK = 3
MOBA_Q_BLOCK = 32

DIL_HEADS = 8
DIL_PATTERNS = ((128, 1), (512, 4), (2048, 16))

D_FF = 4 * D_MODEL

EVEN_SPLITS = (MLA_Q_RANK, MLA_KV_RANK, MLA_ROPE, DSA_HEADS * HEAD_DIM, DSA_HEADS * HEAD_DIM, DSA_HEADS * HEAD_DIM, IDX_HEADS * IDX_DIM, IDX_DIM, IDX_HEADS)
EVEN_IN = sum(EVEN_SPLITS)
EVEN_OUT = MLA_HEADS * MLA_V + DSA_HEADS * HEAD_DIM
ODD_SPLITS = (MOBA_HEADS * HEAD_DIM,) * 3 + (DIL_HEADS * HEAD_DIM,) * 3
ODD_IN = sum(ODD_SPLITS)
ODD_OUT = (MOBA_HEADS + DIL_HEADS) * HEAD_DIM
N_EVEN = (DEPTH + 1) // 2
N_ODD = DEPTH // 2

kernel_name = 'hybrid_mla_dsa_moba_dilated'


def rmsnorm(x, g):
    xf = x.astype(jnp.float32)
    y = xf * lax.rsqrt(jnp.mean(xf * xf, axis=-1, keepdims=True) + NORM_EPS)
    return (y * g.astype(jnp.float32)).astype(x.dtype)


def rope_tables(seq, dim):
    inv = ROPE_THETA ** (-jnp.arange(0, dim, 2, dtype=jnp.float32) / dim)
    ang = jnp.arange(seq, dtype=jnp.float32)[:, None] * inv[None, :]
    return (jnp.cos(ang), jnp.sin(ang))


def apply_rope(x, rope):
    cos, sin = rope
    c = cos[None, :, None, :].astype(x.dtype)
    s = sin[None, :, None, :].astype(x.dtype)
    x1, x2 = jnp.split(x, 2, axis=-1)
    return jnp.concatenate([x1 * c - x2 * s, x1 * s + x2 * c], axis=-1)


def masked_softmax(scores, mask):
    return jax.nn.softmax(jnp.where(mask, scores.astype(jnp.float32), -jnp.inf), axis=-1)


def split_cols(x, sizes):
    offs = [int(o) for o in np.cumsum(sizes)[:-1]]
    return jnp.split(x, offs, axis=-1)


def sweep_query_blocks(fn, seq, block):
    n = seq // block
    out = jnp.moveaxis(lax.map(fn, jnp.arange(n)), 0, 1)
    return out.reshape((out.shape[0], n * block) + out.shape[3:])


def mla_attention(c_q, c_kv, k_pe, g_q, g_kv, w_uq, w_ukv, rope64):
    B, S, _ = c_q.shape
    q = (rmsnorm(c_q, g_q) @ w_uq).reshape(B, S, MLA_HEADS, MLA_NOPE + MLA_ROPE)
    kv = (rmsnorm(c_kv, g_kv) @ w_ukv).reshape(B, S, MLA_HEADS, MLA_NOPE + MLA_V)
    q_nope, q_pe = q[..., :MLA_NOPE], apply_rope(q[..., MLA_NOPE:], rope64)
    k_nope, v = kv[..., :MLA_NOPE], kv[..., MLA_NOPE:]
    k_pe = apply_rope(k_pe[:, :, None, :], rope64)[:, :, 0]
    scale = (MLA_NOPE + MLA_ROPE) ** -0.5
    pos = jnp.arange(S)

    def block(i):
        qs = i * Q_BLOCK
        qn = lax.dynamic_slice_in_dim(q_nope, qs, Q_BLOCK, axis=1)
        qp = lax.dynamic_slice_in_dim(q_pe, qs, Q_BLOCK, axis=1)
        s = jnp.einsum('bqhd,bkhd->bhqk', qn, k_nope) + jnp.einsum('bqhr,bkr->bhqk', qp, k_pe)
        qpos = qs + jnp.arange(Q_BLOCK)
        p = masked_softmax(s * scale, pos[None, :] <= qpos[:, None])
        return jnp.einsum('bhqk,bkhd->bqhd', p.astype(v.dtype), v)

    return sweep_query_blocks(block, S, Q_BLOCK)


def dsa_attention(q, k, v, q_idx, k_idx, w_idx, rope128, rope64):
    B, S = q.shape[:2]
    topk = min(DSA_TOPK_MAX, S // 4)
    q = apply_rope(q, rope128)
    k = apply_rope(k, rope128)
    q_idx = apply_rope(q_idx, rope64)
    k_idx = apply_rope(k_idx[:, :, None, :], rope64)[:, :, 0]
    w = w_idx.astype(jnp.float32) * IDX_HEADS ** -0.5
    pos = jnp.arange(S)
    gather = jax.vmap(lambda kb, ib: kb[ib])

    def block(i):
        qs = i * GATHER_Q_BLOCK
        qpos = qs + jnp.arange(GATHER_Q_BLOCK)
        qi = lax.dynamic_slice_in_dim(q_idx, qs, GATHER_Q_BLOCK, axis=1)
        wi = lax.dynamic_slice_in_dim(w, qs, GATHER_Q_BLOCK, axis=1)
        dots = jnp.einsum('bqhd,bsd->bqhs', qi, k_idx).astype(jnp.float32) * IDX_DIM ** -0.5
        score = jnp.einsum('bqh,bqhs->bqs', wi, jax.nn.relu(dots))
        causal = pos[None, :] <= qpos[:, None]
        score = jnp.where(causal[None], score, -jnp.inf)
        _, sel = lax.top_k(score, topk)
        valid = sel <= qpos[None, :, None]
        k_sel = gather(k, sel)
        v_sel = gather(v, sel)
        qb = lax.dynamic_slice_in_dim(q, qs, GATHER_Q_BLOCK, axis=1)
        s = jnp.einsum('bqhd,bqkhd->bhqk', qb, k_sel) * HEAD_DIM ** -0.5
        p = masked_softmax(s, valid[:, None])
        return jnp.einsum('bhqk,bqkhd->bqhd', p.astype(v.dtype), v_sel)

    return sweep_query_blocks(block, S, GATHER_Q_BLOCK)


def moba_attention(q, k, v, rope128):
    B, S, H, Dh = q.shape
    q = apply_rope(q, rope128)
    k = apply_rope(k, rope128)
    n_blk = -(-S // MOBA_BLOCK)
    s_pad = n_blk * MOBA_BLOCK
    pad = ((0, 0), (0, s_pad - S), (0, 0), (0, 0))
    q = jnp.transpose(jnp.pad(q, pad), (0, 2, 1, 3))
    k_blk = jnp.transpose(jnp.pad(k, pad), (0, 2, 1, 3)).reshape(B, H, n_blk, MOBA_BLOCK, Dh)
    v_blk = jnp.transpose(jnp.pad(v, pad), (0, 2, 1, 3)).reshape(B, H, n_blk, MOBA_BLOCK, Dh)
    k_mean = jnp.mean(k_blk, axis=3)
    n_sel = min(MOBA_TOPK, n_blk - 1)
    scale = Dh ** -0.5
    gather = jax.vmap(jax.vmap(lambda kb, ib: kb[ib]))

    def block(i):
        qs = i * MOBA_Q_BLOCK
        qpos = qs + jnp.arange(MOBA_Q_BLOCK)
        own = qs // MOBA_BLOCK
        qb = lax.dynamic_slice_in_dim(q, qs, MOBA_Q_BLOCK, axis=2)
        k_own = lax.dynamic_index_in_dim(k_blk, own, axis=2, keepdims=False)
        v_own = lax.dynamic_index_in_dim(v_blk, own, axis=2, keepdims=False)
        s_own = jnp.einsum('bhqd,bhkd->bhqk', qb, k_own) * scale
        own_mask = (own * MOBA_BLOCK + jnp.arange(MOBA_BLOCK))[None, :] <= qpos[:, None]
        own_mask = jnp.broadcast_to(own_mask, s_own.shape)
        if n_sel == 0:
            p = masked_softmax(s_own, own_mask)
            return jnp.einsum('bhqk,bhkd->bqhd', p.astype(v.dtype), v_own)
        gate = jnp.einsum('bhqd,bhnd->bhqn', qb, k_mean).astype(jnp.float32)
        gate = jnp.where(jnp.arange(n_blk) < own, gate, -jnp.inf)
        _, sel = lax.top_k(gate, n_sel)
        valid = jnp.repeat(sel < own, MOBA_BLOCK, axis=-1)
        k_sel = gather(k_blk, sel).reshape(B, H, MOBA_Q_BLOCK, n_sel * MOBA_BLOCK, Dh)
        v_sel = gather(v_blk, sel).reshape(B, H, MOBA_Q_BLOCK, n_sel * MOBA_BLOCK, Dh)
        s_sel = jnp.einsum('bhqd,bhqkd->bhqk', qb, k_sel) * scale
        p = masked_softmax(jnp.concatenate([s_sel, s_own], -1), jnp.concatenate([valid, own_mask], -1))
        p = p.astype(v.dtype)
        n_s = n_sel * MOBA_BLOCK
        return (jnp.einsum('bhqk,bhqkd->bqhd', p[..., :n_s], v_sel)
                + jnp.einsum('bhqk,bhkd->bqhd', p[..., n_s:], v_own))

    return sweep_query_blocks(block, s_pad, MOBA_Q_BLOCK)[:, :S]


def dilated_attention(q, k, v, rope128):
    B, S, H, Dh = q.shape
    q = apply_rope(q, rope128)
    k = apply_rope(k, rope128)
    scale = Dh ** -0.5

    def block(i):
        qs = i * GATHER_Q_BLOCK
        qpos = qs + jnp.arange(GATHER_Q_BLOCK)
        qb = lax.dynamic_slice_in_dim(q, qs, GATHER_Q_BLOCK, axis=1)
        outs, lses = [], []
        for window, dil in DIL_PATTERNS:
            idx = qpos[:, None] - dil * jnp.arange(window // dil + 1)[None, :]
            valid = idx >= 0
            idx = jnp.maximum(idx, 0)
            k_sel = k[:, idx]
            v_sel = v[:, idx]
            s = jnp.einsum('bqhd,bqnhd->bhqn', qb, k_sel).astype(jnp.float32) * scale
            s = jnp.where(valid[None, None], s, -jnp.inf)
            lse = jax.nn.logsumexp(s, axis=-1)
            p = jnp.exp(s - lse[..., None]).astype(v.dtype)
            outs.append(jnp.einsum('bhqn,bqnhd->bqhd', p, v_sel))
            lses.append(lse)
        wts = jax.nn.softmax(jnp.stack(lses, 0), axis=0)
        out = jnp.einsum('gbhq,gbqhd->bqhd', wts, jnp.stack(outs, 0).astype(jnp.float32))
        return out.astype(q.dtype)

    return sweep_query_blocks(block, S, GATHER_Q_BLOCK)


def even_mixer(h, w_in, g_q, g_kv, w_uq, w_ukv, w_out, rope128, rope64):
    B, S, _ = h.shape
    c_q, c_kv, k_pe, q, k, v, q_idx, k_idx, w_idx = split_cols(h @ w_in, EVEN_SPLITS)
    a = mla_attention(c_q, c_kv, k_pe, g_q, g_kv, w_uq, w_ukv, rope64)
    shp = (B, S, DSA_HEADS, HEAD_DIM)
    b = dsa_attention(q.reshape(shp), k.reshape(shp), v.reshape(shp),
                      q_idx.reshape(B, S, IDX_HEADS, IDX_DIM), k_idx, w_idx, rope128, rope64)
    o = jnp.concatenate([a.reshape(B, S, -1), b.reshape(B, S, -1)], axis=-1)
    return o @ w_out


def odd_mixer(h, w_in, w_out, rope128):
    B, S, _ = h.shape
    qc, kc, vc, qd, kd, vd = split_cols(h @ w_in, ODD_SPLITS)
    sc = (B, S, MOBA_HEADS, HEAD_DIM)
    sd = (B, S, DIL_HEADS, HEAD_DIM)
    c = moba_attention(qc.reshape(sc), kc.reshape(sc), vc.reshape(sc), rope128)
    d = dilated_attention(qd.reshape(sd), kd.reshape(sd), vd.reshape(sd), rope128)
    o = jnp.concatenate([c.reshape(B, S, -1), d.reshape(B, S, -1)], axis=-1)
    return o @ w_out


def squared_relu_mlp(h, w1, w2):
    return jnp.square(jax.nn.relu(h @ w1)) @ w2


def setup_inputs(seed: int = 0) -> dict:
    key = jax.random.key(seed)
    ks = jax.random.split(key, 14)

    def nrm(k, shape, fan_in):
        return jax.random.normal(k, shape, jnp.float32) * fan_in ** -0.5

    def gain(k, shape):
        return 1.0 + 0.02 * jax.random.normal(k, shape, jnp.float32)

    return {
        'x': jax.random.normal(ks[0], (BATCH, SEQ, D_MODEL), jnp.float32),
        'ln_mix': gain(ks[1], (DEPTH, D_MODEL)),
        'ln_mlp': gain(ks[2], (DEPTH, D_MODEL)),
        'ln_final': gain(ks[3], (D_MODEL,)),
        'e_w_in': nrm(ks[4], (N_EVEN, D_MODEL, EVEN_IN), D_MODEL),
        'e_g_q': gain(ks[5], (N_EVEN, MLA_Q_RANK)),
        'e_g_kv': gain(ks[6], (N_EVEN, MLA_KV_RANK)),
        'e_w_uq': nrm(ks[7], (N_EVEN, MLA_Q_RANK, MLA_HEADS * (MLA_NOPE + MLA_ROPE)), MLA_Q_RANK),
        'e_w_ukv': nrm(ks[8], (N_EVEN, MLA_KV_RANK, MLA_HEADS * (MLA_NOPE + MLA_V)), MLA_KV_RANK),
        'e_w_out': nrm(ks[9], (N_EVEN, EVEN_OUT, D_MODEL), EVEN_OUT),
        'o_w_in': nrm(ks[10], (N_ODD, D_MODEL, ODD_IN), D_MODEL),
        'o_w_out': nrm(ks[11], (N_ODD, ODD_OUT, D_MODEL), ODD_OUT),
        'mlp_w1': nrm(ks[12], (DEPTH, D_MODEL, D_FF), D_MODEL),
        'mlp_w2': nrm(ks[13], (DEPTH, D_FF, D_MODEL), D_FF),
    }


def reference(x, ln_mix, ln_mlp, ln_final, e_w_in, e_g_q, e_g_kv, e_w_uq, e_w_ukv, e_w_out,
              o_w_in, o_w_out, mlp_w1, mlp_w2):
    S = x.shape[1]
    rope128 = rope_tables(S, HEAD_DIM)
    rope64 = rope_tables(S, MLA_ROPE)
    for layer in range(DEPTH):
        j = layer // 2
        h = rmsnorm(x, ln_mix[layer])
        if layer % 2 == 0:
            x = x + even_mixer(h, e_w_in[j], e_g_q[j], e_g_kv[j], e_w_uq[j], e_w_ukv[j], e_w_out[j], rope128, rope64)
        else:
            x = x + odd_mixer(h, o_w_in[j], o_w_out[j], rope128)
        h = rmsnorm(x, ln_mlp[layer])
        x = x + squared_relu_mlp(h, mlp_w1[layer], mlp_w2[layer])
    return rmsnorm(x, ln_final)
```

```python
import functools

import numpy as np
import jax
import jax.numpy as jnp
from jax import lax
from jax.experimental import pallas as pl
from jax.experimental.pallas import tpu as pltpu

HEAD_DIM = 128
ROPE_THETA = 10000.0
NORM_EPS = 1e-6
MLA_HEADS, MLA_Q_RANK, MLA_KV_RANK, MLA_NOPE, MLA_ROPE, MLA_V = 8, 512, 256, 128, 64, 128
DSA_HEADS, IDX_HEADS, IDX_DIM, DSA_TOPK_MAX = 8, 16, 64, 256
MOBA_HEADS, MOBA_BLOCK, MOBA_TOPK = 8, 256, 3
DIL_HEADS = 8
DIL_PATTERNS = ((128, 1), (512, 4), (2048, 16))

LANE = 128
VMEM_LIMIT_BYTES = 56 * 2**20

F32 = jnp.float32
_CD = jnp.bfloat16
_NEG = -1e30
_INT_MIN = -(2**31)
_INT_MAX = 2**31 - 1
_KEY_NEG_INF = int(np.int32(np.uint32(0xFF800000) ^ np.uint32(0x7FFFFFFF)))


def _params(*sem):
    return pltpu.CompilerParams(dimension_semantics=sem, vmem_limit_bytes=VMEM_LIMIT_BYTES)


def _dot_nt(a, b):
    return lax.dot_general(a, b, (((1,), (1,)), ((), ())), preferred_element_type=F32)


def _rmsnorm_body(x_ref, g_ref, o_ref):
    x = x_ref[...].astype(F32)
    y = x * lax.rsqrt(jnp.mean(x * x, axis=-1, keepdims=True) + NORM_EPS)
    o_ref[...] = (y * g_ref[...]).astype(o_ref.dtype)


def _rmsnorm(x, g, out_dtype, tm=512):
    m, d = x.shape
    return pl.pallas_call(
        _rmsnorm_body,
        grid=(m // tm,),
        in_specs=[pl.BlockSpec((tm, d), lambda i: (i, 0)), pl.BlockSpec((1, d), lambda i: (0, 0))],
        out_specs=pl.BlockSpec((tm, d), lambda i: (i, 0)),
        out_shape=jax.ShapeDtypeStruct((m, d), out_dtype),
        compiler_params=_params("parallel"),
        name="rmsnorm",
    )(x, g.reshape(1, d).astype(F32))


def _rope_slab(y, cos, sin):
    return y * cos + pltpu.roll(y, LANE // 2, 1) * sin


def _matmul_body(*refs, n_extra, epilogue, out_dtype):
    a_ref, w_ref = refs[0], refs[1]
    extra = refs[2:2 + n_extra]
    o_ref, acc_ref = refs[2 + n_extra], refs[3 + n_extra]
    k = pl.program_id(2)

    @pl.when(k == 0)
    def _init():
        acc_ref[...] = jnp.zeros_like(acc_ref)

    acc_ref[...] += jnp.dot(a_ref[...], w_ref[...], preferred_element_type=F32)

    @pl.when(k == pl.num_programs(2) - 1)
    def _finish():
        o_ref[...] = epilogue(acc_ref[...], *extra).astype(out_dtype)


def _matmul(a, w, *, tm, tn, tk, out_dtype, epilogue, extra=(), extra_specs=(), name):
    m, kd = a.shape
    n = w.shape[1]
    assert m % tm == 0 and n % tn == 0 and kd % tk == 0, (a.shape, w.shape, tm, tn, tk)
    body = functools.partial(_matmul_body, n_extra=len(extra), epilogue=epilogue, out_dtype=out_dtype)
    return pl.pallas_call(
        body,
        grid=(m // tm, n // tn, kd // tk),
        in_specs=[pl.BlockSpec((tm, tk), lambda i, j, k: (i, k)),
                  pl.BlockSpec((tk, tn), lambda i, j, k: (k, j))] + list(extra_specs),
        out_specs=pl.BlockSpec((tm, tn), lambda i, j, k: (i, j)),
        out_shape=jax.ShapeDtypeStruct((m, n), out_dtype),
        scratch_shapes=[pltpu.VMEM((tm, tn), F32)],
        compiler_params=_params("parallel", "parallel", "arbitrary"),
        name=name,
    )(a, w, *extra)


def _ep_scale(scale):
    def ep(y):
        return y if scale == 1.0 else y * scale
    return ep


def _ep_relu2(y):
    return jnp.square(jnp.maximum(y, 0.0))


def _ep_residual(y, r_ref):
    return y + r_ref[...]


def _ep_rmsnorm(y, g_ref):
    return y * lax.rsqrt(jnp.mean(y * y, axis=-1, keepdims=True) + NORM_EPS) * g_ref[...]


def _ep_rope(pattern, plain_scale):
    def ep(y, cos_ref, sin_ref):
        cos, sin = cos_ref[0], sin_ref[0]
        out = []
        for c, rot in enumerate(pattern):
            slab = y[:, c * LANE:(c + 1) * LANE]
            out.append(_rope_slab(slab, cos, sin) if rot else slab * plain_scale)
        return jnp.concatenate(out, axis=1)
    return ep


def _rope_matmul(a, w, tabs, tab_of_tile, *, seq, tm, tn, pattern, plain_scale=1.0, name):
    nblk = seq // tm
    spec = pl.BlockSpec((1, tm, LANE), lambda i, j, k: (tab_of_tile(j), i % nblk, 0))
    return _matmul(a, w, tm=tm, tn=tn, tk=a.shape[1], out_dtype=_CD,
                   epilogue=_ep_rope(pattern, plain_scale), extra=tabs, extra_specs=(spec, spec), name=name)


def _attn_init(m_scr, l_scr, acc_scr):
    m_scr[...] = jnp.full(m_scr.shape, _NEG, F32)
    l_scr[...] = jnp.zeros(l_scr.shape, F32)
    acc_scr[...] = jnp.zeros(acc_scr.shape, F32)


def _attn_step(s, v, m_scr, l_scr, acc_scr):
    m_prev = m_scr[...]
    m_new = jnp.maximum(m_prev, jnp.max(s, axis=1, keepdims=True))
    alpha = jnp.exp(m_prev - m_new)
    p = jnp.exp(s - m_new)
    l_scr[...] = alpha * l_scr[...] + jnp.sum(p, axis=1, keepdims=True)
    acc_scr[...] = alpha * acc_scr[...] + jnp.dot(p.astype(v.dtype), v, preferred_element_type=F32)
    m_scr[...] = m_new


def _attn_finish(o_ref, l_scr, acc_scr):
    o_ref[0] = (acc_scr[...] / l_scr[...]).astype(o_ref.dtype)


def _attn_scratch(blk):
    return [pltpu.VMEM((blk, 1), F32), pltpu.VMEM((blk, 1), F32), pltpu.VMEM((blk, HEAD_DIM), F32)]


def _rows(k_ref, j, blk):
    return k_ref[0, pl.ds(pl.multiple_of(j * blk, blk), blk), :]


def _mla_body(q_ref, kn_ref, kp_ref, v_ref, o_ref, m_scr, l_scr, acc_scr, *, blk):
    i = pl.program_id(2)
    q = q_ref[0]
    _attn_init(m_scr, l_scr, acc_scr)

    def scores(j):
        k = jnp.concatenate([_rows(kn_ref, j, blk), _rows(kp_ref, j, blk)], axis=1)
        return _dot_nt(q, k)

    def past(j, carry):
        _attn_step(scores(j), _rows(v_ref, j, blk), m_scr, l_scr, acc_scr)
        return carry

    lax.fori_loop(0, i, past, 0)
    s = scores(i)
    r = lax.broadcasted_iota(jnp.int32, s.shape, 0)
    c = lax.broadcasted_iota(jnp.int32, s.shape, 1)
    _attn_step(jnp.where(c <= r, s, _NEG), _rows(v_ref, i, blk), m_scr, l_scr, acc_scr)
    _attn_finish(o_ref, l_scr, acc_scr)


def _mla_attention(q, kv, r64, *, blk):
    b, s, _ = q.shape
    h = MLA_HEADS
    return pl.pallas_call(
        functools.partial(_mla_body, blk=blk),
        grid=(b, h, s // blk),
        in_specs=[pl.BlockSpec((1, blk, 2 * LANE), lambda b_, h_, i: (b_, i, h_)),
                  pl.BlockSpec((1, s, LANE), lambda b_, h_, i: (b_, 0, h_)),
                  pl.BlockSpec((1, s, LANE), lambda b_, h_, i: (b_, 0, _R64_KPE)),
                  pl.BlockSpec((1, s, LANE), lambda b_, h_, i: (b_, 0, h + h_))],
        out_specs=pl.BlockSpec((1, blk, LANE), lambda b_, h_, i: (b_, i, h_)),
        out_shape=jax.ShapeDtypeStruct((b, s, h * MLA_V), _CD),
        scratch_shapes=_attn_scratch(blk),
        compiler_params=_params("parallel", "parallel", "arbitrary"),
        name="mla_attention",
    )(q, kv, r64, kv)


_SEARCH_ROWS = 128


def _dsa_body(qi_ref, ka_ref, kb_ref, w_ref, q_ref, k_ref, v_ref, o_ref,
              sel_scr, wb_scr, jc_scr, m_scr, l_scr, acc_scr, *, blk, topk, seq):
    i = pl.program_id(1)
    h = pl.program_id(2)
    nslab = blk // LANE
    rg = _SEARCH_ROWS

    @pl.when(h == 0)
    def _select():
        wv = w_ref[0]
        for hh in range(IDX_HEADS):
            wb_scr[hh] = jnp.broadcast_to(wv[:, hh:hh + 1], (blk, LANE))
        row = lax.broadcasted_iota(jnp.int32, (blk, LANE), 0)
        lane = lax.broadcasted_iota(jnp.int32, (blk, LANE), 1)

        def score_chunk(j, carry):
            ka, kb = _rows(ka_ref, j, blk), _rows(kb_ref, j, blk)
            acc = [jnp.zeros((blk, LANE), F32) for _ in range(nslab)]
            for p in range(IDX_HEADS // 2):
                qp = qi_ref[0, :, p * LANE:(p + 1) * LANE]
                da, db = _dot_nt(qp, ka), _dot_nt(qp, kb)
                wa, wb = wb_scr[2 * p], wb_scr[2 * p + 1]
                for c in range(nslab):
                    sl = slice(c * LANE, (c + 1) * LANE)
                    acc[c] = acc[c] + wa * jnp.maximum(da[:, sl], 0.0) + wb * jnp.maximum(db[:, sl], 0.0)
            for c in range(nslab):
                bits = lax.bitcast_convert_type(acc[c], jnp.int32)
                key = bits ^ ((bits >> 31) & _INT_MAX)
                causal = (j * blk + c * LANE + lane) <= (i * blk + row)
                sel_scr[j, :, c * LANE:(c + 1) * LANE] = jnp.where(causal, key, _KEY_NEG_INF)
            return carry

        lax.fori_loop(0, i + 1, score_chunk, 0)

        lane_g = lax.broadcasted_iota(jnp.int32, (rg, LANE), 1)
        row_g = lax.broadcasted_iota(jnp.int32, (rg, LANE), 0)
        for g in range(blk // rg):
            rs = slice(g * rg, (g + 1) * rg)
            kk = jnp.minimum(topk, i * blk + g * rg + row_g + 1).astype(F32)

            def count(pred):
                def body(c, cnt):
                    for u in range(nslab):
                        keys = sel_scr[c, rs, u * LANE:(u + 1) * LANE]
                        cnt = cnt + jnp.where(pred(keys, c * blk + u * LANE + lane_g), 1.0, 0.0)
                    return cnt
                cnt = lax.fori_loop(0, i + 1, body, jnp.zeros((rg, LANE), F32))
                return jnp.broadcast_to(jnp.sum(cnt, axis=1, keepdims=True), (rg, LANE))

            def bit_step(b, t):
                cand = t + (jnp.int32(1) << (31 - b))
                return jnp.where(count(lambda keys, col: keys >= cand) >= kk, cand, t)

            t = lax.fori_loop(0, 32, bit_step, jnp.full((rg, LANE), _INT_MIN, jnp.int32))
            n_ge = count(lambda keys, col: keys >= t)
            jc_scr[rs, :] = jnp.full((rg, LANE), seq, jnp.int32)

            @pl.when(jnp.max(n_ge - kk) > 0.5)
            def _ties():
                need = kk - count(lambda keys, col: keys > t)

                def col_step(b, jc):
                    cand = jc + (jnp.int32(1) << ((seq - 1).bit_length() - 1 - b))
                    n_lt = count(lambda keys, col: (keys == t) & (col < cand))
                    return jnp.where(n_lt < need, cand, jc)

                jc = lax.fori_loop(0, (seq - 1).bit_length(), col_step, jnp.zeros((rg, LANE), jnp.int32))
                jc_scr[rs, :] = jc

            jc = jc_scr[rs, :]

            def to_bias(c, carry):
                for u in range(nslab):
                    keys = sel_scr[c, rs, u * LANE:(u + 1) * LANE]
                    col = c * blk + u * LANE + lane_g
                    chosen = (keys > t) | ((keys == t) & (col <= jc))
                    bias = jnp.where(chosen, 0.0, _NEG).astype(F32)
                    sel_scr[c, rs, u * LANE:(u + 1) * LANE] = lax.bitcast_convert_type(bias, jnp.int32)
                return carry

            lax.fori_loop(0, i + 1, to_bias, 0)

    q = q_ref[0]
    _attn_init(m_scr, l_scr, acc_scr)

    def att(j, carry):
        s = _dot_nt(q, _rows(k_ref, j, blk)) + lax.bitcast_convert_type(sel_scr[j], F32)
        _attn_step(s, _rows(v_ref, j, blk), m_scr, l_scr, acc_scr)
        return carry

    lax.fori_loop(0, i + 1, att, 0)
    _attn_finish(o_ref, l_scr, acc_scr)


def _dsa_attention(r64, w_idx, qk, v, *, blk):
    b, s, _ = v.shape
    h = DSA_HEADS
    topk = min(DSA_TOPK_MAX, s // 4)
    return pl.pallas_call(
        functools.partial(_dsa_body, blk=blk, topk=topk, seq=s),
        grid=(b, s // blk, h),
        in_specs=[pl.BlockSpec((1, blk, 8 * LANE), lambda b_, i, h_: (b_, i, 0)),
                  pl.BlockSpec((1, s, LANE), lambda b_, i, h_: (b_, 0, _R64_KIA)),
                  pl.BlockSpec((1, s, LANE), lambda b_, i, h_: (b_, 0, _R64_KIB)),
                  pl.BlockSpec((1, blk, LANE), lambda b_, i, h_: (b_, i, 0)),
                  pl.BlockSpec((1, blk, LANE), lambda b_, i, h_: (b_, i, h_)),
                  pl.BlockSpec((1, s, LANE), lambda b_, i, h_: (b_, 0, h + h_)),
                  pl.BlockSpec((1, s, LANE), lambda b_, i, h_: (b_, 0, h_))],
        out_specs=pl.BlockSpec((1, blk, LANE), lambda b_, i, h_: (b_, i, h_)),
        out_shape=jax.ShapeDtypeStruct((b, s, h * HEAD_DIM), _CD),
        scratch_shapes=[pltpu.VMEM((s // blk, blk, blk), jnp.int32),
                        pltpu.VMEM((IDX_HEADS, blk, LANE), F32),
                        pltpu.VMEM((blk, LANE), jnp.int32)] + _attn_scratch(blk),
        compiler_params=_params("parallel", "arbitrary", "arbitrary"),
        name="dsa_attention",
    )(r64, r64, r64, w_idx, qk, qk, v)


def _moba_body(q_ref, k_ref, v_ref, o_ref, kmean_scr, m_scr, l_scr, acc_scr, *, blk, seq):
    i = pl.program_id(2)
    nper = blk // MOBA_BLOCK
    shift = MOBA_BLOCK.bit_length() - 1

    @pl.when(i == 0)
    def _block_means():
        r = lax.broadcasted_iota(jnp.int32, (LANE, seq), 0)
        c = lax.broadcasted_iota(jnp.int32, (LANE, seq), 1)
        avg = jnp.where((c >> shift) == r, 1.0 / MOBA_BLOCK, 0.0).astype(k_ref.dtype)
        kmean_scr[...] = jnp.dot(avg, k_ref[0], preferred_element_type=F32).astype(kmean_scr.dtype)

    q = q_ref[0]
    lane = lax.broadcasted_iota(jnp.int32, (blk, LANE), 1)
    row = lax.broadcasted_iota(jnp.int32, (blk, LANE), 0)
    own = (i * blk + row) >> shift
    gate = jnp.where(lane < own, _dot_nt(q, kmean_scr[...]), _NEG)
    lane_f = lane.astype(F32)
    sel = jnp.zeros((blk, LANE), F32)
    for _ in range(MOBA_TOPK):
        best = jnp.max(gate, axis=1, keepdims=True)
        hit = (gate == best) & (best > 0.5 * _NEG)
        first = jnp.min(jnp.where(hit, lane_f, float(LANE)), axis=1, keepdims=True)
        pick = lane_f == first
        sel = jnp.where(pick, 1.0, sel)
        gate = jnp.where(pick, _NEG, gate)

    _attn_init(m_scr, l_scr, acc_scr)
    cblk = lax.broadcasted_iota(jnp.int32, (blk, blk), 1) >> shift

    def chosen(j):
        out = None
        for u in reversed(range(nper)):
            pu = jnp.sum(jnp.where(lane == j * nper + u, sel, 0.0), axis=1, keepdims=True)
            pu = jnp.broadcast_to(pu, (blk, blk))
            out = pu if out is None else jnp.where(cblk == u, pu, out)
        return out > 0.5

    def past(j, carry):
        s = jnp.where(chosen(j), _dot_nt(q, _rows(k_ref, j, blk)), _NEG)
        _attn_step(s, _rows(v_ref, j, blk), m_scr, l_scr, acc_scr)
        return carry

    lax.fori_loop(0, i, past, 0)
    r = lax.broadcasted_iota(jnp.int32, (blk, blk), 0)
    c = lax.broadcasted_iota(jnp.int32, (blk, blk), 1)
    rblk = r >> shift
    allowed = ((cblk == rblk) & (c <= r)) | ((cblk < rblk) & chosen(i))
    s = jnp.where(allowed, _dot_nt(q, _rows(k_ref, i, blk)), _NEG)
    _attn_step(s, _rows(v_ref, i, blk), m_scr, l_scr, acc_scr)
    _attn_finish(o_ref, l_scr, acc_scr)


def _moba_attention(qk, v, *, blk, q_off, k_off, v_off):
    b, s, _ = qk.shape
    h = MOBA_HEADS
    assert s % MOBA_BLOCK == 0 and s // MOBA_BLOCK <= LANE and blk % MOBA_BLOCK == 0
    return pl.pallas_call(
        functools.partial(_moba_body, blk=blk, seq=s),
        grid=(b, h, s // blk),
        in_specs=[pl.BlockSpec((1, blk, LANE), lambda b_, h_, i: (b_, i, q_off + h_)),
                  pl.BlockSpec((1, s, LANE), lambda b_, h_, i: (b_, 0, k_off + h_)),
                  pl.BlockSpec((1, s, LANE), lambda b_, h_, i: (b_, 0, v_off + h_))],
        out_specs=pl.BlockSpec((1, blk, LANE), lambda b_, h_, i: (b_, i, h_)),
        out_shape=jax.ShapeDtypeStruct((b, s, h * HEAD_DIM), _CD),
        scratch_shapes=[pltpu.VMEM((LANE, HEAD_DIM), _CD)] + _attn_scratch(blk),
        compiler_params=_params("parallel", "parallel", "arbitrary"),
        name="moba_attention",
    )(qk, qk, v)


def _dilated_body(q_ref, k_ref, v_ref, bias_ref, o_ref, m_scr, l_scr, acc_scr, *, blk, nrel):
    i = pl.program_id(2)
    q = q_ref[0]
    _attn_init(m_scr, l_scr, acc_scr)

    def step(j, carry):
        s = _dot_nt(q, _rows(k_ref, j, blk)) + bias_ref[i - j]
        _attn_step(s, _rows(v_ref, j, blk), m_scr, l_scr, acc_scr)
        return carry

    lax.fori_loop(jnp.maximum(i - (nrel - 1), 0), i + 1, step, 0)
    _attn_finish(o_ref, l_scr, acc_scr)


def _dilated_bias(blk):
    reach = max(w for w, _ in DIL_PATTERNS)
    nrel = -(-reach // blk) + 1
    rel = jnp.arange(nrel, dtype=jnp.int32)[:, None, None]
    r = jnp.arange(blk, dtype=jnp.int32)[None, :, None]
    c = jnp.arange(blk, dtype=jnp.int32)[None, None, :]
    d = rel * blk + r - c
    mult = jnp.zeros(d.shape, F32)
    for window, dil in DIL_PATTERNS:
        mult = mult + ((d >= 0) & (d <= (window // dil) * dil) & (d % dil == 0)).astype(F32)
    return jnp.where(mult > 0, jnp.log(jnp.maximum(mult, 1.0)), _NEG), nrel


def _dilated_attention(qk, v, *, blk, q_off, k_off, v_off):
    b, s, _ = qk.shape
    h = DIL_HEADS
    bias, nrel = _dilated_bias(blk)
    return pl.pallas_call(
        functools.partial(_dilated_body, blk=blk, nrel=nrel),
        grid=(b, h, s // blk),
        in_specs=[pl.BlockSpec((1, blk, LANE), lambda b_, h_, i: (b_, i, q_off + h_)),
                  pl.BlockSpec((1, s, LANE), lambda b_, h_, i: (b_, 0, k_off + h_)),
                  pl.BlockSpec((1, s, LANE), lambda b_, h_, i: (b_, 0, v_off + h_)),
                  pl.BlockSpec((nrel, blk, blk), lambda b_, h_, i: (0, 0, 0))],
        out_specs=pl.BlockSpec((1, blk, LANE), lambda b_, h_, i: (b_, i, h_)),
        out_shape=jax.ShapeDtypeStruct((b, s, h * HEAD_DIM), _CD),
        scratch_shapes=_attn_scratch(blk),
        compiler_params=_params("parallel", "parallel", "arbitrary"),
        name="dilated_attention",
    )(qk, qk, v, bias)


_E_CQ, _E_CKV, _E_KPE, _E_Q, _E_K, _E_V, _E_QI, _E_KI, _E_WI = [
    int(o) for o in np.cumsum([0, MLA_Q_RANK, MLA_KV_RANK, MLA_ROPE, DSA_HEADS * HEAD_DIM, DSA_HEADS * HEAD_DIM,
                               DSA_HEADS * HEAD_DIM, IDX_HEADS * IDX_DIM, IDX_DIM])]
_R64_KIA, _R64_KIB, _R64_KPE, _R64_SLABS = 8, 9, 10, 12


def _take_cols(w, idx):
    idx = np.asarray(idx)
    cols = jnp.take(w, jnp.asarray(np.maximum(idx, 0)), axis=1)
    return jnp.where(jnp.asarray(idx >= 0)[None, :], cols, 0.0).astype(_CD)


def _r64_columns():
    half = IDX_DIM // 2
    a = np.arange(half)
    z = -np.ones(half, np.int64)
    cols = []
    for p in range(IDX_HEADS // 2):
        ha, hb = _E_QI + 2 * p * IDX_DIM, _E_QI + (2 * p + 1) * IDX_DIM
        cols += [ha + a, hb + a, ha + half + a, hb + half + a]
    cols += [_E_KI + a, z, _E_KI + half + a, z]
    cols += [z, _E_KI + a, z, _E_KI + half + a]
    cols += [_E_KPE + a, z, _E_KPE + half + a, z]
    cols += [z, z, z, z]
    return np.concatenate(cols)


def _uq_columns():
    half = MLA_ROPE // 2
    a = np.arange(half)
    z = -np.ones(half, np.int64)
    cols = []
    for h in range(MLA_HEADS):
        o = h * (MLA_NOPE + MLA_ROPE)
        cols += [o + np.arange(MLA_NOPE), o + MLA_NOPE + a, z, o + MLA_NOPE + half + a, z]
    return np.concatenate(cols)


def _ukv_columns():
    per = MLA_NOPE + MLA_V
    kn = [h * per + np.arange(MLA_NOPE) for h in range(MLA_HEADS)]
    vv = [h * per + MLA_NOPE + np.arange(MLA_V) for h in range(MLA_HEADS)]
    return np.concatenate(kn + vv)


def _rope_tables(seq, dim, scales):
    inv = ROPE_THETA ** (-jnp.arange(0, dim, 2, dtype=F32) / dim)
    ang = jnp.arange(seq, dtype=F32)[:, None] * inv[None, :]
    reps = (LANE // 2) // (dim // 2)
    cos = jnp.tile(jnp.cos(ang), (1, 2 * reps))
    sin = jnp.tile(jnp.sin(ang), (1, reps))
    sin = jnp.concatenate([-sin, sin], axis=1)
    sc = jnp.asarray(scales, F32)[:, None, None]
    return cos[None] * sc, sin[None] * sc


def _mlp_block(x2, g, w1, w2, *, tm):
    hm = _rmsnorm(x2, g, _CD)
    up = _matmul(hm, w1.astype(_CD), tm=tm, tn=1024, tk=w1.shape[0], out_dtype=_CD,
                 epilogue=_ep_relu2, name="mlp_up")
    res = pl.BlockSpec((tm, 1024), lambda i, j, k: (i, j))
    return _matmul(up, w2.astype(_CD), tm=tm, tn=1024, tk=2048, out_dtype=F32,
                   epilogue=_ep_residual, extra=(x2,), extra_specs=(res,), name="mlp_down")


def _out_proj(o, w_out, x2, *, tm):
    res = pl.BlockSpec((tm, 1024), lambda i, j, k: (i, j))
    return _matmul(o, w_out.astype(_CD), tm=tm, tn=1024, tk=w_out.shape[0], out_dtype=F32,
                   epilogue=_ep_residual, extra=(x2,), extra_specs=(res,), name="out_proj")


def _even_mixer(x2, g_mix, w_in, g_q, g_kv, w_uq, w_ukv, w_out, *, batch, seq, blk):
    t, d = x2.shape
    tm = blk
    hm = _rmsnorm(x2, g_mix, _CD)
    rope64 = _rope_tables(seq, IDX_DIM, (1.0, (MLA_NOPE + MLA_ROPE) ** -0.5))
    rope128 = _rope_tables(seq, HEAD_DIM, (HEAD_DIM ** -0.5, 1.0))
    nh = DSA_HEADS * HEAD_DIM

    r64 = _rope_matmul(hm, _take_cols(w_in, _r64_columns()), rope64, lambda j: 0, seq=seq, tm=tm, tn=512,
                       pattern=(True,) * 4, name="in_proj_rope64")
    qk = _rope_matmul(hm, w_in[:, _E_Q:_E_V].astype(_CD), rope128, lambda j: j // (nh // 512), seq=seq, tm=tm,
                      tn=512, pattern=(True,) * 4, name="in_proj_rope128")
    v = _matmul(hm, w_in[:, _E_V:_E_QI].astype(_CD), tm=tm, tn=512, tk=d, out_dtype=_CD,
                epilogue=_ep_scale(1.0), name="in_proj_v")
    w_idx = _matmul(hm, _take_cols(w_in, np.concatenate([_E_WI + np.arange(IDX_HEADS),
                                                          -np.ones(LANE - IDX_HEADS, np.int64)])),
                    tm=tm, tn=LANE, tk=d, out_dtype=F32,
                    epilogue=_ep_scale(IDX_HEADS ** -0.5 * IDX_DIM ** -0.5), name="in_proj_widx")
    gspec = lambda n: pl.BlockSpec((1, n), lambda i, j, k: (0, 0))
    c_q = _matmul(hm, w_in[:, _E_CQ:_E_CKV].astype(_CD), tm=tm, tn=MLA_Q_RANK, tk=d, out_dtype=_CD,
                  epilogue=_ep_rmsnorm, extra=(g_q.reshape(1, -1),), extra_specs=(gspec(MLA_Q_RANK),),
                  name="in_proj_cq")
    c_kv = _matmul(hm, w_in[:, _E_CKV:_E_KPE].astype(_CD), tm=tm, tn=MLA_KV_RANK, tk=d, out_dtype=_CD,
                   epilogue=_ep_rmsnorm, extra=(g_kv.reshape(1, -1),), extra_specs=(gspec(MLA_KV_RANK),),
                   name="in_proj_ckv")

    mla_scale = (MLA_NOPE + MLA_ROPE) ** -0.5
    q_mla = _rope_matmul(c_q, _take_cols(w_uq, _uq_columns()), rope64, lambda j: 1, seq=seq, tm=tm, tn=512,
                         pattern=(False, True) * 2, plain_scale=mla_scale, name="mla_q_up")
    kv_mla = _matmul(c_kv, _take_cols(w_ukv, _ukv_columns()), tm=tm, tn=512, tk=MLA_KV_RANK, out_dtype=_CD,
                     epilogue=_ep_scale(1.0), name="mla_kv_up")

    sh = lambda z: z.reshape(batch, seq, z.shape[-1])
    a = _mla_attention(sh(q_mla), sh(kv_mla), sh(r64), blk=blk)
    bsa = _dsa_attention(sh(r64), sh(w_idx), sh(qk), sh(v), blk=min(blk, 256))
    o = jnp.concatenate([a, bsa], axis=-1).reshape(t, -1)
    return _out_proj(o, w_out, x2, tm=tm)


def _odd_mixer(x2, g_mix, w_in, w_out, *, batch, seq, blk):
    t, d = x2.shape
    tm = blk
    hm = _rmsnorm(x2, g_mix, _CD)
    nh = MOBA_HEADS * HEAD_DIM
    rope128 = _rope_tables(seq, HEAD_DIM, (HEAD_DIM ** -0.5, 1.0))
    w_qk = jnp.concatenate([w_in[:, 0:2 * nh], w_in[:, 3 * nh:5 * nh]], axis=1).astype(_CD)
    w_v = jnp.concatenate([w_in[:, 2 * nh:3 * nh], w_in[:, 5 * nh:6 * nh]], axis=1).astype(_CD)
    qk = _rope_matmul(hm, w_qk, rope128, lambda j: (j // (nh // 512)) % 2, seq=seq, tm=tm, tn=512,
                      pattern=(True,) * 4, name="in_proj_rope128")
    v = _matmul(hm, w_v, tm=tm, tn=512, tk=d, out_dtype=_CD, epilogue=_ep_scale(1.0), name="in_proj_v")
    sh = lambda z: z.reshape(batch, seq, z.shape[-1])
    hs = MOBA_HEADS
    c = _moba_attention(sh(qk), sh(v), blk=blk, q_off=0, k_off=hs, v_off=0)
    dl = _dilated_attention(sh(qk), sh(v), blk=blk, q_off=2 * hs, k_off=3 * hs, v_off=hs)
    o = jnp.concatenate([c, dl], axis=-1).reshape(t, -1)
    return _out_proj(o, w_out, x2, tm=tm)


def kernel(x, ln_mix, ln_mlp, ln_final, e_w_in, e_g_q, e_g_kv, e_w_uq, e_w_ukv, e_w_out,
           o_w_in, o_w_out, mlp_w1, mlp_w2):
    batch, seq, d = x.shape
    blk = min(512, seq)
    assert seq % blk == 0 and blk % MOBA_BLOCK == 0
    x2 = x.reshape(batch * seq, d)
    depth = ln_mix.shape[0]
    for layer in range(depth):
        j = layer // 2
        if layer % 2 == 0:
            x2 = _even_mixer(x2, ln_mix[layer], e_w_in[j], e_g_q[j], e_g_kv[j], e_w_uq[j], e_w_ukv[j],
                             e_w_out[j], batch=batch, seq=seq, blk=blk)
        else:
            x2 = _odd_mixer(x2, ln_mix[layer], o_w_in[j], o_w_out[j], batch=batch, seq=seq, blk=blk)
        x2 = _mlp_block(x2, ln_mlp[layer], mlp_w1[layer], mlp_w2[layer], tm=blk)
    return _rmsnorm(x2, ln_final, x.dtype).reshape(batch, seq, d)
```

```python
import functools

import numpy as np
import jax
import jax.numpy as jnp
from jax import lax
from jax.experimental import pallas as pl
from jax.experimental.pallas import tpu as pltpu

HEAD_DIM = 128
ROPE_THETA = 10000.0
NORM_EPS = 1e-6
MLA_HEADS, MLA_Q_RANK, MLA_KV_RANK, MLA_NOPE, MLA_ROPE, MLA_V = 8, 512, 256, 128, 64, 128
DSA_HEADS, IDX_HEADS, IDX_DIM, DSA_TOPK_MAX = 8, 16, 64, 256
MOBA_HEADS, MOBA_BLOCK, MOBA_TOPK = 8, 256, 3
DIL_HEADS = 8
DIL_PATTERNS = ((128, 1), (512, 4), (2048, 16))

LANE = 128
SUBLANE = 8
VMEM_LIMIT_BYTES = 56 * 2**20

F32 = jnp.float32
_CD = jnp.bfloat16
_NEG = -1e30
_INT_MIN = -(2**31)
_INT_MAX = 2**31 - 1
_KEY_NEG_INF = int(np.int32(np.uint32(0xFF800000) ^ np.uint32(0x7FFFFFFF)))
_HEADS_PER_STEP = 2


def _params(*sem):
    return pltpu.CompilerParams(dimension_semantics=sem, vmem_limit_bytes=VMEM_LIMIT_BYTES)


def _rmsnorm_body(x_ref, g_ref, o_ref):
    x = x_ref[...].astype(F32)
    y = x * lax.rsqrt(jnp.mean(x * x, axis=-1, keepdims=True) + NORM_EPS)
    o_ref[...] = (y * g_ref[...]).astype(o_ref.dtype)


def _rmsnorm(x, g, out_dtype, tm=512):
    m, d = x.shape
    return pl.pallas_call(
        _rmsnorm_body,
        grid=(m // tm,),
        in_specs=[pl.BlockSpec((tm, d), lambda i: (i, 0)), pl.BlockSpec((1, d), lambda i: (0, 0))],
        out_specs=pl.BlockSpec((tm, d), lambda i: (i, 0)),
        out_shape=jax.ShapeDtypeStruct((m, d), out_dtype),
        compiler_params=_params("parallel"),
        name="rmsnorm",
    )(x, g.reshape(1, d).astype(F32))


def _rope_slab(y, cos, sin):
    return y * cos + pltpu.roll(y, LANE // 2, 1) * sin


def _matmul_body(*refs, n_extra, epilogue, out_dtype):
    a_ref, w_ref = refs[0], refs[1]
    extra = refs[2:2 + n_extra]
    o_ref, acc_ref = refs[2 + n_extra], refs[3 + n_extra]
    k = pl.program_id(2)

    @pl.when(k == 0)
    def _init():
        acc_ref[...] = jnp.zeros_like(acc_ref)

    acc_ref[...] += jnp.dot(a_ref[...], w_ref[...], preferred_element_type=F32)

    @pl.when(k == pl.num_programs(2) - 1)
    def _finish():
        o_ref[...] = epilogue(acc_ref[...], *extra).astype(out_dtype)


def _matmul(a, w, *, tm, tn, tk, out_dtype, epilogue, extra=(), extra_specs=(), name):
    m, kd = a.shape
    n = w.shape[1]
    assert m % tm == 0 and n % tn == 0 and kd % tk == 0, (a.shape, w.shape, tm, tn, tk)
    body = functools.partial(_matmul_body, n_extra=len(extra), epilogue=epilogue, out_dtype=out_dtype)
    return pl.pallas_call(
        body,
        grid=(m // tm, n // tn, kd // tk),
        in_specs=[pl.BlockSpec((tm, tk), lambda i, j, k: (i, k)),
                  pl.BlockSpec((tk, tn), lambda i, j, k: (k, j))] + list(extra_specs),
        out_specs=pl.BlockSpec((tm, tn), lambda i, j, k: (i, j)),
        out_shape=jax.ShapeDtypeStruct((m, n), out_dtype),
        scratch_shapes=[pltpu.VMEM((tm, tn), F32)],
        compiler_params=_params("parallel", "parallel", "arbitrary"),
        name=name,
    )(a, w, *extra)


def _ep_scale(scale):
    def ep(y):
        return y if scale == 1.0 else y * scale
    return ep


def _ep_relu2(y):
    return jnp.square(jnp.maximum(y, 0.0))


def _ep_residual(y, r_ref):
    return y + r_ref[...]


def _ep_rmsnorm(y, g_ref):
    return y * lax.rsqrt(jnp.mean(y * y, axis=-1, keepdims=True) + NORM_EPS) * g_ref[...]


def _ep_rope(pattern, plain_scale):
    def ep(y, cos_ref, sin_ref):
        cos, sin = cos_ref[0], sin_ref[0]
        out = []
        for c, rot in enumerate(pattern):
            slab = y[:, c * LANE:(c + 1) * LANE]
            out.append(_rope_slab(slab, cos, sin) if rot else slab * plain_scale)
        return jnp.concatenate(out, axis=1)
    return ep


def _rope_matmul(a, w, tabs, tab_of_tile, *, seq, tm, tn, pattern, plain_scale=1.0, name):
    nblk = seq // tm
    spec = pl.BlockSpec((1, tm, LANE), lambda i, j, k: (tab_of_tile(j), i % nblk, 0))
    return _matmul(a, w, tm=tm, tn=tn, tk=a.shape[1], out_dtype=_CD,
                   epilogue=_ep_rope(pattern, plain_scale), extra=tabs, extra_specs=(spec, spec), name=name)


def _attn_init(m_scr, l_scr, acc_scr):
    m_scr[...] = jnp.full(m_scr.shape, _NEG, F32)
    l_scr[...] = jnp.zeros(l_scr.shape, F32)
    acc_scr[...] = jnp.zeros(acc_scr.shape, F32)


def _tile_rows(j, blk):
    return pl.ds(pl.multiple_of(j * blk, blk), blk)


def _dot_tn(a, b):
    return lax.dot_general(a, b, (((0,), (0,)), ((), ())), preferred_element_type=F32)


def _key_major(q):
    return q.astype(F32).T.astype(q.dtype)


def _attn_scratch(blk, hg):
    per_head = [pltpu.VMEM((1, blk), F32), pltpu.VMEM((1, blk), F32), pltpu.VMEM((HEAD_DIM, blk), F32)]
    return [pltpu.VMEM((2, hg, blk, blk), F32)] + per_head * hg


def _head_state(scr, hg):
    return scr[0], [scr[1 + 3 * g:4 + 3 * g] for g in range(hg)]


def _attn_step(st, v, m_scr, l_scr, acc_scr):
    m_prev = m_scr[...]
    m_new = jnp.maximum(m_prev, jnp.max(st, axis=0, keepdims=True))
    alpha = jnp.exp(m_prev - m_new)
    p = jnp.exp(st - m_new)
    l_scr[...] = alpha * l_scr[...] + jnp.sum(p, axis=0, keepdims=True)
    acc_scr[...] = alpha * acc_scr[...] + _dot_tn(v, p.astype(v.dtype))
    m_scr[...] = m_new


def _attn_out(l_scr, acc_scr, dtype):
    return (acc_scr[...] / l_scr[...]).T.astype(dtype)


def _pipelined_key_tiles(last, scores, consume, first=0):
    n_past = last - first

    def pair(t, carry):
        j = first + 2 * t
        scores(j + 1, 1)
        consume(j, 0, False)
        scores(j + 2, 0)
        consume(j + 1, 1, False)
        return carry

    scores(first, 0)
    lax.fori_loop(0, n_past // 2, pair, 0)

    @pl.when(n_past % 2 == 1)
    def _odd():
        scores(last, 1)
        consume(last - 1, 0, False)
        consume(last, 1, True)

    @pl.when(n_past % 2 == 0)
    def _even():
        consume(last, 0, True)


def _causal(s):
    kpos = lax.broadcasted_iota(jnp.int32, s.shape, 0)
    qpos = lax.broadcasted_iota(jnp.int32, s.shape, 1)
    return jnp.where(kpos <= qpos, s, _NEG)


def _mla_body(q_ref, kn_ref, kp_ref, v_ref, o_ref, *scr, blk, hg):
    i = pl.program_id(2)
    s_scr, heads = _head_state(scr, hg)
    for st in heads:
        _attn_init(*st)
    qts = [_key_major(q_ref[0, :, g * 2 * LANE:(g + 1) * 2 * LANE]) for g in range(hg)]

    def scores(j, slot):
        rows = _tile_rows(j, blk)
        kp = kp_ref[0, rows, :]
        for g in range(hg):
            k = jnp.concatenate([kn_ref[0, rows, g * LANE:(g + 1) * LANE], kp], axis=1)
            s_scr[slot, g] = jnp.dot(k, qts[g], preferred_element_type=F32)

    def consume(j, slot, masked):
        rows = _tile_rows(j, blk)
        for g, st in enumerate(heads):
            s = s_scr[slot, g]
            _attn_step(_causal(s) if masked else s, v_ref[0, rows, g * LANE:(g + 1) * LANE], *st)

    _pipelined_key_tiles(i, scores, consume)
    for g, (_, l_scr, acc_scr) in enumerate(heads):
        o_ref[0, :, g * LANE:(g + 1) * LANE] = _attn_out(l_scr, acc_scr, o_ref.dtype)


def _mla_attention(q, kv, r64, *, blk, hg=_HEADS_PER_STEP):
    b, s, _ = q.shape
    h = MLA_HEADS
    return pl.pallas_call(
        functools.partial(_mla_body, blk=blk, hg=hg),
        grid=(b, h // hg, s // blk),
        in_specs=[pl.BlockSpec((1, blk, hg * 2 * LANE), lambda b_, h_, i: (b_, i, h_)),
                  pl.BlockSpec((1, s, hg * LANE), lambda b_, h_, i: (b_, 0, h_)),
                  pl.BlockSpec((1, s, LANE), lambda b_, h_, i: (b_, 0, _R64_KPE)),
                  pl.BlockSpec((1, s, hg * LANE), lambda b_, h_, i: (b_, 0, h // hg + h_))],
        out_specs=pl.BlockSpec((1, blk, hg * LANE), lambda b_, h_, i: (b_, i, h_)),
        out_shape=jax.ShapeDtypeStruct((b, s, h * MLA_V), _CD),
        scratch_shapes=_attn_scratch(blk, hg),
        compiler_params=_params("parallel", "parallel", "arbitrary"),
        name="mla_attention",
    )(q, kv, r64, kv)


def _dsa_select(i, qi_ref, ka_ref, kb_ref, w_ref, sel_scr, jc_scr, *, blk, topk, seq):
    wt = w_ref[0].T
    qits = [_key_major(qi_ref[0, :, p * LANE:(p + 1) * LANE]) for p in range(IDX_HEADS // 2)]
    kloc = lax.broadcasted_iota(jnp.int32, (blk, blk), 0)
    qpos = i * blk + lax.broadcasted_iota(jnp.int32, (blk, blk), 1)

    def score_tile(c, carry):
        rows = _tile_rows(c, blk)
        ka, kb = ka_ref[0, rows, :], kb_ref[0, rows, :]
        acc = jnp.zeros((blk, blk), F32)
        for p in range(IDX_HEADS // 2):
            da = jnp.dot(ka, qits[p], preferred_element_type=F32)
            db = jnp.dot(kb, qits[p], preferred_element_type=F32)
            acc = acc + wt[2 * p:2 * p + 1, :] * jnp.maximum(da, 0.0) + wt[2 * p + 1:2 * p + 2, :] * jnp.maximum(db, 0.0)
        bits = lax.bitcast_convert_type(acc, jnp.int32)
        key = bits ^ ((bits >> 31) & _INT_MAX)
        sel_scr[c] = jnp.where(c * blk + kloc <= qpos, key, _KEY_NEG_INF)
        return carry

    lax.fori_loop(0, i + 1, score_tile, 0)

    kk = jnp.minimum(topk, i * blk + lax.broadcasted_iota(jnp.int32, (1, blk), 1) + 1).astype(F32)

    def count(pred):
        def body(c, cnt):
            hit = pred(sel_scr[c], c * blk + kloc)
            return cnt + jnp.sum(jnp.where(hit, 1.0, 0.0), axis=0, keepdims=True)
        return lax.fori_loop(0, i + 1, body, jnp.zeros((1, blk), F32))

    def bit_step(b, t):
        cand = t + (jnp.int32(1) << (31 - b))
        return jnp.where(count(lambda keys, kpos: keys >= cand) >= kk, cand, t)

    t = lax.fori_loop(0, 32, bit_step, jnp.full((1, blk), _INT_MIN, jnp.int32))
    n_ge = count(lambda keys, kpos: keys >= t)
    jc_scr[...] = jnp.full((1, blk), seq, jnp.int32)

    @pl.when(jnp.max(n_ge - kk) > 0.5)
    def _ties():
        need = kk - count(lambda keys, kpos: keys > t)
        nbits = (seq - 1).bit_length()

        def pos_step(b, jc):
            cand = jc + (jnp.int32(1) << (nbits - 1 - b))
            n_lt = count(lambda keys, kpos: (keys == t) & (kpos < cand))
            return jnp.where(n_lt < need, cand, jc)

        jc_scr[...] = lax.fori_loop(0, nbits, pos_step, jnp.zeros((1, blk), jnp.int32))

    jc = jc_scr[...]

    def to_bias(c, carry):
        keys = sel_scr[c]
        chosen = (keys > t) | ((keys == t) & (c * blk + kloc <= jc))
        sel_scr[c] = lax.bitcast_convert_type(jnp.where(chosen, 0.0, _NEG).astype(F32), jnp.int32)
        return carry

    lax.fori_loop(0, i + 1, to_bias, 0)


def _dsa_body(qi_ref, ka_ref, kb_ref, w_ref, q_ref, k_ref, v_ref, o_ref, sel_scr, jc_scr, *scr,
              blk, hg, topk, seq):
    i = pl.program_id(1)

    @pl.when(pl.program_id(2) == 0)
    def _select():
        _dsa_select(i, qi_ref, ka_ref, kb_ref, w_ref, sel_scr, jc_scr, blk=blk, topk=topk, seq=seq)

    s_scr, heads = _head_state(scr, hg)
    for st in heads:
        _attn_init(*st)
    qts = [_key_major(q_ref[0, :, g * LANE:(g + 1) * LANE]) for g in range(hg)]

    def scores(j, slot):
        rows = _tile_rows(j, blk)
        for g in range(hg):
            s_scr[slot, g] = jnp.dot(k_ref[0, rows, g * LANE:(g + 1) * LANE], qts[g], preferred_element_type=F32)

    def consume(j, slot, masked):
        rows = _tile_rows(j, blk)
        bias = lax.bitcast_convert_type(sel_scr[j], F32)
        for g, st in enumerate(heads):
            _attn_step(s_scr[slot, g] + bias, v_ref[0, rows, g * LANE:(g + 1) * LANE], *st)

    _pipelined_key_tiles(i, scores, consume)
    for g, (_, l_scr, acc_scr) in enumerate(heads):
        o_ref[0, :, g * LANE:(g + 1) * LANE] = _attn_out(l_scr, acc_scr, o_ref.dtype)


def _dsa_attention(r64, w_idx, qk, v, *, blk, hg=_HEADS_PER_STEP):
    b, s, _ = v.shape
    h = DSA_HEADS
    topk = min(DSA_TOPK_MAX, s // 4)
    return pl.pallas_call(
        functools.partial(_dsa_body, blk=blk, hg=hg, topk=topk, seq=s),
        grid=(b, s // blk, h // hg),
        in_specs=[pl.BlockSpec((1, blk, 8 * LANE), lambda b_, i, h_: (b_, i, 0)),
                  pl.BlockSpec((1, s, LANE), lambda b_, i, h_: (b_, 0, _R64_KIA)),
                  pl.BlockSpec((1, s, LANE), lambda b_, i, h_: (b_, 0, _R64_KIB)),
                  pl.BlockSpec((1, blk, LANE), lambda b_, i, h_: (b_, i, 0)),
                  pl.BlockSpec((1, blk, hg * LANE), lambda b_, i, h_: (b_, i, h_)),
                  pl.BlockSpec((1, s, hg * LANE), lambda b_, i, h_: (b_, 0, h // hg + h_)),
                  pl.BlockSpec((1, s, hg * LANE), lambda b_, i, h_: (b_, 0, h_))],
        out_specs=pl.BlockSpec((1, blk, hg * LANE), lambda b_, i, h_: (b_, i, h_)),
        out_shape=jax.ShapeDtypeStruct((b, s, h * HEAD_DIM), _CD),
        scratch_shapes=[pltpu.VMEM((s // blk, blk, blk), jnp.int32),
                        pltpu.VMEM((1, blk), jnp.int32)] + _attn_scratch(blk, hg),
        compiler_params=_params("parallel", "arbitrary", "arbitrary"),
        name="dsa_attention",
    )(r64, r64, r64, w_idx, qk, qk, v)


def _moba_body(q_ref, k_ref, v_ref, o_ref, kmean_scr, pick_scr, *scr, blk, hg, seq, nbp):
    i = pl.program_id(2)
    nper = blk // MOBA_BLOCK
    shift = MOBA_BLOCK.bit_length() - 1
    s_scr, heads = _head_state(scr, hg)

    @pl.when(i == 0)
    def _block_means():
        r = lax.broadcasted_iota(jnp.int32, (nbp, seq), 0)
        c = lax.broadcasted_iota(jnp.int32, (nbp, seq), 1)
        avg = jnp.where((c >> shift) == r, 1.0 / MOBA_BLOCK, 0.0).astype(k_ref.dtype)
        for g in range(hg):
            kmean_scr[g] = jnp.dot(avg, k_ref[0, :, g * LANE:(g + 1) * LANE],
                                   preferred_element_type=F32).astype(kmean_scr.dtype)

    qts = [_key_major(q_ref[0, :, g * LANE:(g + 1) * LANE]) for g in range(hg)]
    kblk = lax.broadcasted_iota(jnp.int32, (nbp, blk), 0)
    own = (i * blk + lax.broadcasted_iota(jnp.int32, (nbp, blk), 1)) >> shift
    kblk_f = kblk.astype(F32)
    for g, st in enumerate(heads):
        _attn_init(*st)
        gate = jnp.where(kblk < own, jnp.dot(kmean_scr[g], qts[g], preferred_element_type=F32), _NEG)
        pick = jnp.zeros((nbp, blk), F32)
        for _ in range(MOBA_TOPK):
            best = jnp.max(gate, axis=0, keepdims=True)
            hit = (gate == best) & (best > 0.5 * _NEG)
            first = jnp.min(jnp.where(hit, kblk_f, float(nbp)), axis=0, keepdims=True)
            new = kblk_f == first
            pick = jnp.where(new, 1.0, pick)
            gate = jnp.where(new, _NEG, gate)
        pick_scr[g] = pick

    def scores(j, slot):
        rows = _tile_rows(j, blk)
        for g in range(hg):
            s_scr[slot, g] = jnp.dot(k_ref[0, rows, g * LANE:(g + 1) * LANE], qts[g], preferred_element_type=F32)

    kloc = lax.broadcasted_iota(jnp.int32, (MOBA_BLOCK, blk), 0)
    qloc = lax.broadcasted_iota(jnp.int32, (MOBA_BLOCK, blk), 1)

    def consume(j, slot, masked):
        rows = _tile_rows(j, blk)
        for g, st in enumerate(heads):
            parts = []
            for u in range(nper):
                s = s_scr[slot, g, u * MOBA_BLOCK:(u + 1) * MOBA_BLOCK, :]
                ok = pick_scr[g, pl.ds(j * nper + u, 1), :] > 0.5
                if masked:
                    kpos = u * MOBA_BLOCK + kloc
                    same = (qloc >> shift) == u
                    ok = (same & (kpos <= qloc)) | (jnp.logical_not(same) & ok & ((qloc >> shift) > u))
                parts.append(jnp.where(ok, s, _NEG))
            _attn_step(jnp.concatenate(parts, axis=0), v_ref[0, rows, g * LANE:(g + 1) * LANE], *st)

    _pipelined_key_tiles(i, scores, consume)
    for g, (_, l_scr, acc_scr) in enumerate(heads):
        o_ref[0, :, g * LANE:(g + 1) * LANE] = _attn_out(l_scr, acc_scr, o_ref.dtype)


def _moba_attention(qk, v, *, blk, q_off, k_off, v_off, hg=_HEADS_PER_STEP):
    b, s, _ = qk.shape
    h = MOBA_HEADS
    assert s % MOBA_BLOCK == 0 and blk % MOBA_BLOCK == 0
    nbp = -(-(s // MOBA_BLOCK) // SUBLANE) * SUBLANE
    return pl.pallas_call(
        functools.partial(_moba_body, blk=blk, hg=hg, seq=s, nbp=nbp),
        grid=(b, h // hg, s // blk),
        in_specs=[pl.BlockSpec((1, blk, hg * LANE), lambda b_, h_, i: (b_, i, q_off // hg + h_)),
                  pl.BlockSpec((1, s, hg * LANE), lambda b_, h_, i: (b_, 0, k_off // hg + h_)),
                  pl.BlockSpec((1, s, hg * LANE), lambda b_, h_, i: (b_, 0, v_off // hg + h_))],
        out_specs=pl.BlockSpec((1, blk, hg * LANE), lambda b_, h_, i: (b_, i, h_)),
        out_shape=jax.ShapeDtypeStruct((b, s, h * HEAD_DIM), _CD),
        scratch_shapes=[pltpu.VMEM((hg, nbp, HEAD_DIM), _CD),
                        pltpu.VMEM((hg, nbp, blk), F32)] + _attn_scratch(blk, hg),
        compiler_params=_params("parallel", "parallel", "arbitrary"),
        name="moba_attention",
    )(qk, qk, v)


def _dilated_body(q_ref, k_ref, v_ref, bias_ref, o_ref, *scr, blk, hg, nrel):
    i = pl.program_id(2)
    s_scr, heads = _head_state(scr, hg)
    for st in heads:
        _attn_init(*st)
    qts = [_key_major(q_ref[0, :, g * LANE:(g + 1) * LANE]) for g in range(hg)]

    def scores(j, slot):
        rows = _tile_rows(j, blk)
        for g in range(hg):
            s_scr[slot, g] = jnp.dot(k_ref[0, rows, g * LANE:(g + 1) * LANE], qts[g], preferred_element_type=F32)

    def consume(j, slot, masked):
        rows = _tile_rows(j, blk)
        bias = bias_ref[i - j]
        for g, st in enumerate(heads):
            _attn_step(s_scr[slot, g] + bias, v_ref[0, rows, g * LANE:(g + 1) * LANE], *st)

    _pipelined_key_tiles(i, scores, consume, first=jnp.maximum(i - (nrel - 1), 0))
    for g, (_, l_scr, acc_scr) in enumerate(heads):
        o_ref[0, :, g * LANE:(g + 1) * LANE] = _attn_out(l_scr, acc_scr, o_ref.dtype)


def _dilated_bias(blk):
    reach = max(w for w, _ in DIL_PATTERNS)
    nrel = -(-reach // blk) + 1
    rel = jnp.arange(nrel, dtype=jnp.int32)[:, None, None]
    k = jnp.arange(blk, dtype=jnp.int32)[None, :, None]
    q = jnp.arange(blk, dtype=jnp.int32)[None, None, :]
    d = rel * blk + q - k
    mult = jnp.zeros(d.shape, F32)
    for window, dil in DIL_PATTERNS:
        mult = mult + ((d >= 0) & (d <= (window // dil) * dil) & (d % dil == 0)).astype(F32)
    return jnp.where(mult > 0, jnp.log(jnp.maximum(mult, 1.0)), _NEG), nrel


def _dilated_attention(qk, v, *, blk, q_off, k_off, v_off, hg=_HEADS_PER_STEP):
    b, s, _ = qk.shape
    h = DIL_HEADS
    bias, nrel = _dilated_bias(blk)
    return pl.pallas_call(
        functools.partial(_dilated_body, blk=blk, hg=hg, nrel=nrel),
        grid=(b, h // hg, s // blk),
        in_specs=[pl.BlockSpec((1, blk, hg * LANE), lambda b_, h_, i: (b_, i, q_off // hg + h_)),
                  pl.BlockSpec((1, s, hg * LANE), lambda b_, h_, i: (b_, 0, k_off // hg + h_)),
                  pl.BlockSpec((1, s, hg * LANE), lambda b_, h_, i: (b_, 0, v_off // hg + h_)),
                  pl.BlockSpec((nrel, blk, blk), lambda b_, h_, i: (0, 0, 0))],
        out_specs=pl.BlockSpec((1, blk, hg * LANE), lambda b_, h_, i: (b_, i, h_)),
        out_shape=jax.ShapeDtypeStruct((b, s, h * HEAD_DIM), _CD),
        scratch_shapes=_attn_scratch(blk, hg),
        compiler_params=_params("parallel", "parallel", "arbitrary"),
        name="dilated_attention",
    )(qk, qk, v, bias)


_E_CQ, _E_CKV, _E_KPE, _E_Q, _E_K, _E_V, _E_QI, _E_KI, _E_WI = [
    int(o) for o in np.cumsum([0, MLA_Q_RANK, MLA_KV_RANK, MLA_ROPE, DSA_HEADS * HEAD_DIM, DSA_HEADS * HEAD_DIM,
                               DSA_HEADS * HEAD_DIM, IDX_HEADS * IDX_DIM, IDX_DIM])]
_R64_KIA, _R64_KIB, _R64_KPE, _R64_SLABS = 8, 9, 10, 12


def _take_cols(w, idx):
    idx = np.asarray(idx)
    cols = jnp.take(w, jnp.asarray(np.maximum(idx, 0)), axis=1)
    return jnp.where(jnp.asarray(idx >= 0)[None, :], cols, 0.0).astype(_CD)


def _r64_columns():
    half = IDX_DIM // 2
    a = np.arange(half)
    z = -np.ones(half, np.int64)
    cols = []
    for p in range(IDX_HEADS // 2):
        ha, hb = _E_QI + 2 * p * IDX_DIM, _E_QI + (2 * p + 1) * IDX_DIM
        cols += [ha + a, hb + a, ha + half + a, hb + half + a]
    cols += [_E_KI + a, z, _E_KI + half + a, z]
    cols += [z, _E_KI + a, z, _E_KI + half + a]
    cols += [_E_KPE + a, z, _E_KPE + half + a, z]
    cols += [z, z, z, z]
    return np.concatenate(cols)


def _uq_columns():
    half = MLA_ROPE // 2
    a = np.arange(half)
    z = -np.ones(half, np.int64)
    cols = []
    for h in range(MLA_HEADS):
        o = h * (MLA_NOPE + MLA_ROPE)
        cols += [o + np.arange(MLA_NOPE), o + MLA_NOPE + a, z, o + MLA_NOPE + half + a, z]
    return np.concatenate(cols)


def _ukv_columns():
    per = MLA_NOPE + MLA_V
    kn = [h * per + np.arange(MLA_NOPE) for h in range(MLA_HEADS)]
    vv = [h * per + MLA_NOPE + np.arange(MLA_V) for h in range(MLA_HEADS)]
    return np.concatenate(kn + vv)


def _rope_tables(seq, dim, scales):
    inv = ROPE_THETA ** (-jnp.arange(0, dim, 2, dtype=F32) / dim)
    ang = jnp.arange(seq, dtype=F32)[:, None] * inv[None, :]
    reps = (LANE // 2) // (dim // 2)
    cos = jnp.tile(jnp.cos(ang), (1, 2 * reps))
    sin = jnp.tile(jnp.sin(ang), (1, reps))
    sin = jnp.concatenate([-sin, sin], axis=1)
    sc = jnp.asarray(scales, F32)[:, None, None]
    return cos[None] * sc, sin[None] * sc


def _mlp_block(x2, g, w1, w2, *, tm):
    hm = _rmsnorm(x2, g, _CD)
    up = _matmul(hm, w1.astype(_CD), tm=tm, tn=1024, tk=w1.shape[0], out_dtype=_CD,
                 epilogue=_ep_relu2, name="mlp_up")
    res = pl.BlockSpec((tm, 1024), lambda i, j, k: (i, j))
    return _matmul(up, w2.astype(_CD), tm=tm, tn=1024, tk=2048, out_dtype=F32,
                   epilogue=_ep_residual, extra=(x2,), extra_specs=(res,), name="mlp_down")


def _out_proj(o, w_out, x2, *, tm):
    res = pl.BlockSpec((tm, 1024), lambda i, j, k: (i, j))
    return _matmul(o, w_out.astype(_CD), tm=tm, tn=1024, tk=w_out.shape[0], out_dtype=F32,
                   epilogue=_ep_residual, extra=(x2,), extra_specs=(res,), name="out_proj")


def _even_mixer(x2, g_mix, w_in, g_q, g_kv, w_uq, w_ukv, w_out, *, batch, seq, blk):
    t, d = x2.shape
    tm = blk
    hm = _rmsnorm(x2, g_mix, _CD)
    rope64 = _rope_tables(seq, IDX_DIM, (1.0, (MLA_NOPE + MLA_ROPE) ** -0.5))
    rope128 = _rope_tables(seq, HEAD_DIM, (HEAD_DIM ** -0.5, 1.0))
    nh = DSA_HEADS * HEAD_DIM

    r64 = _rope_matmul(hm, _take_cols(w_in, _r64_columns()), rope64, lambda j: 0, seq=seq, tm=tm, tn=512,
                       pattern=(True,) * 4, name="in_proj_rope64")
    qk = _rope_matmul(hm, w_in[:, _E_Q:_E_V].astype(_CD), rope128, lambda j: j // (nh // 512), seq=seq, tm=tm,
                      tn=512, pattern=(True,) * 4, name="in_proj_rope128")
    v = _matmul(hm, w_in[:, _E_V:_E_QI].astype(_CD), tm=tm, tn=512, tk=d, out_dtype=_CD,
                epilogue=_ep_scale(1.0), name="in_proj_v")
    w_idx = _matmul(hm, _take_cols(w_in, np.concatenate([_E_WI + np.arange(IDX_HEADS),
                                                          -np.ones(LANE - IDX_HEADS, np.int64)])),
                    tm=tm, tn=LANE, tk=d, out_dtype=F32,
                    epilogue=_ep_scale(IDX_HEADS ** -0.5 * IDX_DIM ** -0.5), name="in_proj_widx")
    gspec = lambda n: pl.BlockSpec((1, n), lambda i, j, k: (0, 0))
    c_q = _matmul(hm, w_in[:, _E_CQ:_E_CKV].astype(_CD), tm=tm, tn=MLA_Q_RANK, tk=d, out_dtype=_CD,
                  epilogue=_ep_rmsnorm, extra=(g_q.reshape(1, -1),), extra_specs=(gspec(MLA_Q_RANK),),
                  name="in_proj_cq")
    c_kv = _matmul(hm, w_in[:, _E_CKV:_E_KPE].astype(_CD), tm=tm, tn=MLA_KV_RANK, tk=d, out_dtype=_CD,
                   epilogue=_ep_rmsnorm, extra=(g_kv.reshape(1, -1),), extra_specs=(gspec(MLA_KV_RANK),),
                   name="in_proj_ckv")

    mla_scale = (MLA_NOPE + MLA_ROPE) ** -0.5
    q_mla = _rope_matmul(c_q, _take_cols(w_uq, _uq_columns()), rope64, lambda j: 1, seq=seq, tm=tm, tn=512,
                         pattern=(False, True) * 2, plain_scale=mla_scale, name="mla_q_up")
    kv_mla = _matmul(c_kv, _take_cols(w_ukv, _ukv_columns()), tm=tm, tn=512, tk=MLA_KV_RANK, out_dtype=_CD,
                     epilogue=_ep_scale(1.0), name="mla_kv_up")

    sh = lambda z: z.reshape(batch, seq, z.shape[-1])
    a = _mla_attention(sh(q_mla), sh(kv_mla), sh(r64), blk=blk)
    bsa = _dsa_attention(sh(r64), sh(w_idx), sh(qk), sh(v), blk=min(blk, 256))
    o = jnp.concatenate([a, bsa], axis=-1).reshape(t, -1)
    return _out_proj(o, w_out, x2, tm=tm)


def _odd_mixer(x2, g_mix, w_in, w_out, *, batch, seq, blk):
    t, d = x2.shape
    tm = blk
    hm = _rmsnorm(x2, g_mix, _CD)
    nh = MOBA_HEADS * HEAD_DIM
    rope128 = _rope_tables(seq, HEAD_DIM, (HEAD_DIM ** -0.5, 1.0))
    w_qk = jnp.concatenate([w_in[:, 0:2 * nh], w_in[:, 3 * nh:5 * nh]], axis=1).astype(_CD)
    w_v = jnp.concatenate([w_in[:, 2 * nh:3 * nh], w_in[:, 5 * nh:6 * nh]], axis=1).astype(_CD)
    qk = _rope_matmul(hm, w_qk, rope128, lambda j: (j // (nh // 512)) % 2, seq=seq, tm=tm, tn=512,
                      pattern=(True,) * 4, name="in_proj_rope128")
    v = _matmul(hm, w_v, tm=tm, tn=512, tk=d, out_dtype=_CD, epilogue=_ep_scale(1.0), name="in_proj_v")
    sh = lambda z: z.reshape(batch, seq, z.shape[-1])
    hs = MOBA_HEADS
    c = _moba_attention(sh(qk), sh(v), blk=blk, q_off=0, k_off=hs, v_off=0)
    dl = _dilated_attention(sh(qk), sh(v), blk=blk, q_off=2 * hs, k_off=3 * hs, v_off=hs)
    o = jnp.concatenate([c, dl], axis=-1).reshape(t, -1)
    return _out_proj(o, w_out, x2, tm=tm)


def kernel(x, ln_mix, ln_mlp, ln_final, e_w_in, e_g_q, e_g_kv, e_w_uq, e_w_ukv, e_w_out,
           o_w_in, o_w_out, mlp_w1, mlp_w2):
    batch, seq, d = x.shape
    blk = min(512, seq)
    assert seq % blk == 0 and blk % MOBA_BLOCK == 0
    x2 = x.reshape(batch * seq, d)
    depth = ln_mix.shape[0]
    for layer in range(depth):
        j = layer // 2
        if layer % 2 == 0:
            x2 = _even_mixer(x2, ln_mix[layer], e_w_in[j], e_g_q[j], e_g_kv[j], e_w_uq[j], e_w_ukv[j],
                             e_w_out[j], batch=batch, seq=seq, blk=blk)
        else:
            x2 = _odd_mixer(x2, ln_mix[layer], o_w_in[j], o_w_out[j], batch=batch, seq=seq, blk=blk)
        x2 = _mlp_block(x2, ln_mlp[layer], mlp_w1[layer], mlp_w2[layer], tm=blk)
    return _rmsnorm(x2, ln_final, x.dtype).reshape(batch, seq, d)
```

```python
import functools

import numpy as np
import jax
import jax.numpy as jnp
from jax import lax
from jax.experimental import pallas as pl
from jax.experimental.pallas import tpu as pltpu

HEAD_DIM = 128
ROPE_THETA = 10000.0
NORM_EPS = 1e-6
MLA_HEADS, MLA_Q_RANK, MLA_KV_RANK, MLA_NOPE, MLA_ROPE, MLA_V = 8, 512, 256, 128, 64, 128
DSA_HEADS, IDX_HEADS, IDX_DIM, DSA_TOPK_MAX = 8, 16, 64, 256
MOBA_HEADS, MOBA_BLOCK, MOBA_TOPK = 8, 256, 3
DIL_HEADS = 8
DIL_PATTERNS = ((128, 1), (512, 4), (2048, 16))

LANE = 128
SUBLANE = 8
VMEM_LIMIT_BYTES = 56 * 2**20

F32 = jnp.float32
_CD = jnp.bfloat16
_NEG = -1e30
_LOG2E = 1.4426950408889634
_INT_MIN = -(2**31)
_INT_MAX = 2**31 - 1
_KEY_NEG_INF = int(np.int32(np.uint32(0xFF800000) ^ np.uint32(0x7FFFFFFF)))
_HEADS_PER_STEP = 2


def _params(*sem):
    return pltpu.CompilerParams(dimension_semantics=sem, vmem_limit_bytes=VMEM_LIMIT_BYTES)


def _rmsnorm_body(x_ref, g_ref, o_ref):
    x = x_ref[...].astype(F32)
    y = x * lax.rsqrt(jnp.mean(x * x, axis=-1, keepdims=True) + NORM_EPS)
    o_ref[...] = (y * g_ref[...]).astype(o_ref.dtype)


def _rmsnorm(x, g, out_dtype, tm=512):
    m, d = x.shape
    return pl.pallas_call(
        _rmsnorm_body,
        grid=(m // tm,),
        in_specs=[pl.BlockSpec((tm, d), lambda i: (i, 0)), pl.BlockSpec((1, d), lambda i: (0, 0))],
        out_specs=pl.BlockSpec((tm, d), lambda i: (i, 0)),
        out_shape=jax.ShapeDtypeStruct((m, d), out_dtype),
        compiler_params=_params("parallel"),
        name="rmsnorm",
    )(x, g.reshape(1, d).astype(F32))


def _rope_slab(y, cos, sin):
    return y * cos + pltpu.roll(y, LANE // 2, 1) * sin


def _matmul_body(*refs, n_extra, epilogue, out_dtype):
    a_ref, w_ref = refs[0], refs[1]
    extra = refs[2:2 + n_extra]
    o_ref, acc_ref = refs[2 + n_extra], refs[3 + n_extra]
    k = pl.program_id(2)

    @pl.when(k == 0)
    def _init():
        acc_ref[...] = jnp.zeros_like(acc_ref)

    acc_ref[...] += jnp.dot(a_ref[...], w_ref[...], preferred_element_type=F32)

    @pl.when(k == pl.num_programs(2) - 1)
    def _finish():
        o_ref[...] = epilogue(acc_ref[...], *extra).astype(out_dtype)


def _matmul(a, w, *, tm, tn, tk, out_dtype, epilogue, extra=(), extra_specs=(), name):
    m, kd = a.shape
    n = w.shape[1]
    assert m % tm == 0 and n % tn == 0 and kd % tk == 0, (a.shape, w.shape, tm, tn, tk)
    body = functools.partial(_matmul_body, n_extra=len(extra), epilogue=epilogue, out_dtype=out_dtype)
    return pl.pallas_call(
        body,
        grid=(m // tm, n // tn, kd // tk),
        in_specs=[pl.BlockSpec((tm, tk), lambda i, j, k: (i, k)),
                  pl.BlockSpec((tk, tn), lambda i, j, k: (k, j))] + list(extra_specs),
        out_specs=pl.BlockSpec((tm, tn), lambda i, j, k: (i, j)),
        out_shape=jax.ShapeDtypeStruct((m, n), out_dtype),
        scratch_shapes=[pltpu.VMEM((tm, tn), F32)],
        compiler_params=_params("parallel", "parallel", "arbitrary"),
        name=name,
    )(a, w, *extra)


def _ep_scale(scale):
    def ep(y):
        return y if scale == 1.0 else y * scale
    return ep


def _ep_relu2(y):
    return jnp.square(jnp.maximum(y, 0.0))


def _ep_residual(y, r_ref):
    return y + r_ref[...]


def _ep_rmsnorm(y, g_ref):
    return y * lax.rsqrt(jnp.mean(y * y, axis=-1, keepdims=True) + NORM_EPS) * g_ref[...]


def _ep_rope(pattern, plain_scale):
    def ep(y, cos_ref, sin_ref):
        cos, sin = cos_ref[0], sin_ref[0]
        out = []
        for c, rot in enumerate(pattern):
            slab = y[:, c * LANE:(c + 1) * LANE]
            out.append(_rope_slab(slab, cos, sin) if rot else slab * plain_scale)
        return jnp.concatenate(out, axis=1)
    return ep


def _rope_matmul(a, w, tabs, tab_of_tile, *, seq, tm, tn, pattern, plain_scale=1.0, name):
    nblk = seq // tm
    spec = pl.BlockSpec((1, tm, LANE), lambda i, j, k: (tab_of_tile(j), i % nblk, 0))
    return _matmul(a, w, tm=tm, tn=tn, tk=a.shape[1], out_dtype=_CD,
                   epilogue=_ep_rope(pattern, plain_scale), extra=tabs, extra_specs=(spec, spec), name=name)


def _attn_init(m_scr, l_scr, acc_scr):
    m_scr[...] = jnp.full(m_scr.shape, _NEG, F32)
    l_scr[...] = jnp.zeros(l_scr.shape, F32)
    acc_scr[...] = jnp.zeros(acc_scr.shape, F32)


def _tile_rows(j, blk):
    return pl.ds(pl.multiple_of(j * blk, blk), blk)


def _dot_tn(a, b):
    return lax.dot_general(a, b, (((0,), (0,)), ((), ())), preferred_element_type=F32)


def _key_major(q):
    return q.astype(F32).T.astype(q.dtype)


def _attn_scratch(blk, hg):
    per_head = [pltpu.VMEM((1, blk), F32), pltpu.VMEM((1, blk), F32), pltpu.VMEM((HEAD_DIM, blk), F32)]
    return [pltpu.VMEM((2, hg, blk, blk), F32)] + per_head * hg


def _head_state(scr, hg):
    return scr[0], [scr[1 + 3 * g:4 + 3 * g] for g in range(hg)]


def _attn_step(st, v, m_scr, l_scr, acc_scr):
    m_prev = m_scr[...]
    m_new = jnp.maximum(m_prev, jnp.max(st, axis=0, keepdims=True))
    alpha = jnp.exp2(m_prev - m_new)
    p = jnp.exp2(st - m_new)
    l_scr[...] = alpha * l_scr[...] + jnp.sum(p, axis=0, keepdims=True)
    acc_scr[...] = alpha * acc_scr[...] + _dot_tn(v, p.astype(v.dtype))
    m_scr[...] = m_new


def _attn_out(l_scr, acc_scr, dtype):
    return (acc_scr[...] / l_scr[...]).T.astype(dtype)


def _pipelined_key_tiles(last, scores, consume, first=0):
    n_past = last - first

    def pair(t, carry):
        j = first + 2 * t
        scores(j + 1, 1)
        consume(j, 0, False)
        scores(j + 2, 0)
        consume(j + 1, 1, False)
        return carry

    scores(first, 0)
    lax.fori_loop(0, n_past // 2, pair, 0)

    @pl.when(n_past % 2 == 1)
    def _odd():
        scores(last, 1)
        consume(last - 1, 0, False)
        consume(last, 1, True)

    @pl.when(n_past % 2 == 0)
    def _even():
        consume(last, 0, True)


def _causal(s):
    kpos = lax.broadcasted_iota(jnp.int32, s.shape, 0)
    qpos = lax.broadcasted_iota(jnp.int32, s.shape, 1)
    return jnp.where(kpos <= qpos, s, _NEG)


def _mla_body(q_ref, kn_ref, kp_ref, v_ref, o_ref, *scr, blk, hg):
    i = pl.program_id(2)
    s_scr, heads = _head_state(scr, hg)
    for st in heads:
        _attn_init(*st)
    qts = [_key_major(q_ref[0, :, g * 2 * LANE:(g + 1) * 2 * LANE]) for g in range(hg)]

    def scores(j, slot):
        rows = _tile_rows(j, blk)
        kp = kp_ref[0, rows, :]
        for g in range(hg):
            k = jnp.concatenate([kn_ref[0, rows, g * LANE:(g + 1) * LANE], kp], axis=1)
            s_scr[slot, g] = jnp.dot(k, qts[g], preferred_element_type=F32)

    def consume(j, slot, masked):
        rows = _tile_rows(j, blk)
        for g, st in enumerate(heads):
            s = s_scr[slot, g]
            _attn_step(_causal(s) if masked else s, v_ref[0, rows, g * LANE:(g + 1) * LANE], *st)

    _pipelined_key_tiles(i, scores, consume)
    for g, (_, l_scr, acc_scr) in enumerate(heads):
        o_ref[0, :, g * LANE:(g + 1) * LANE] = _attn_out(l_scr, acc_scr, o_ref.dtype)


def _mla_attention(q, kv, r64, *, blk, hg=_HEADS_PER_STEP):
    b, s, _ = q.shape
    h = MLA_HEADS
    return pl.pallas_call(
        functools.partial(_mla_body, blk=blk, hg=hg),
        grid=(b, h // hg, s // blk),
        in_specs=[pl.BlockSpec((1, blk, hg * 2 * LANE), lambda b_, h_, i: (b_, i, h_)),
                  pl.BlockSpec((1, s, hg * LANE), lambda b_, h_, i: (b_, 0, h_)),
                  pl.BlockSpec((1, s, LANE), lambda b_, h_, i: (b_, 0, _R64_KPE)),
                  pl.BlockSpec((1, s, hg * LANE), lambda b_, h_, i: (b_, 0, h // hg + h_))],
        out_specs=pl.BlockSpec((1, blk, hg * LANE), lambda b_, h_, i: (b_, i, h_)),
        out_shape=jax.ShapeDtypeStruct((b, s, h * MLA_V), _CD),
        scratch_shapes=_attn_scratch(blk, hg),
        compiler_params=_params("parallel", "parallel", "arbitrary"),
        name="mla_attention",
    )(q, kv, r64, kv)


def _dsa_select(i, qi_ref, ka_ref, kb_ref, w_ref, sel_scr, jc_scr, *, blk, topk, seq):
    wt = w_ref[0].T
    qits = [_key_major(qi_ref[0, :, p * LANE:(p + 1) * LANE]) for p in range(IDX_HEADS // 2)]
    kloc = lax.broadcasted_iota(jnp.int32, (blk, blk), 0)
    qpos = i * blk + lax.broadcasted_iota(jnp.int32, (blk, blk), 1)

    def score_tile(c, carry):
        rows = _tile_rows(c, blk)
        ka, kb = ka_ref[0, rows, :], kb_ref[0, rows, :]
        acc = jnp.zeros((blk, blk), F32)
        for p in range(IDX_HEADS // 2):
            da = jnp.dot(ka, qits[p], preferred_element_type=F32)
            db = jnp.dot(kb, qits[p], preferred_element_type=F32)
            acc = acc + wt[2 * p:2 * p + 1, :] * jnp.maximum(da, 0.0) + wt[2 * p + 1:2 * p + 2, :] * jnp.maximum(db, 0.0)
        bits = lax.bitcast_convert_type(acc, jnp.int32)
        key = bits ^ ((bits >> 31) & _INT_MAX)
        sel_scr[c] = jnp.where(c * blk + kloc <= qpos, key, _KEY_NEG_INF)
        return carry

    lax.fori_loop(0, i + 1, score_tile, 0)

    kk = jnp.minimum(topk, i * blk + lax.broadcasted_iota(jnp.int32, (1, blk), 1) + 1).astype(F32)

    def count(pred):
        def body(c, cnt):
            hit = jnp.where(pred(sel_scr[c], c * blk + kloc), 1.0, 0.0)
            return cnt + jnp.sum(hit.reshape(blk // SUBLANE, SUBLANE, blk), axis=0)
        cnt = lax.fori_loop(0, i + 1, body, jnp.zeros((SUBLANE, blk), F32))
        return jnp.sum(cnt, axis=0, keepdims=True)

    def bit_step(b, t):
        cand = t + (jnp.int32(1) << (31 - b))
        return jnp.where(count(lambda keys, kpos: keys >= cand) >= kk, cand, t)

    t = lax.fori_loop(0, 32, bit_step, jnp.full((1, blk), _INT_MIN, jnp.int32))
    n_ge = count(lambda keys, kpos: keys >= t)
    jc_scr[...] = jnp.full((1, blk), seq, jnp.int32)

    @pl.when(jnp.max(n_ge - kk) > 0.5)
    def _ties():
        need = kk - count(lambda keys, kpos: keys > t)
        nbits = (seq - 1).bit_length()

        def pos_step(b, jc):
            cand = jc + (jnp.int32(1) << (nbits - 1 - b))
            n_lt = count(lambda keys, kpos: (keys == t) & (kpos < cand))
            return jnp.where(n_lt < need, cand, jc)

        jc_scr[...] = lax.fori_loop(0, nbits, pos_step, jnp.zeros((1, blk), jnp.int32))

    jc = jc_scr[...]

    def to_bias(c, carry):
        keys = sel_scr[c]
        chosen = (keys > t) | ((keys == t) & (c * blk + kloc <= jc))
        sel_scr[c] = lax.bitcast_convert_type(jnp.where(chosen, 0.0, _NEG).astype(F32), jnp.int32)
        return carry

    lax.fori_loop(0, i + 1, to_bias, 0)


def _dsa_body(qi_ref, ka_ref, kb_ref, w_ref, q_ref, k_ref, v_ref, o_ref, sel_scr, jc_scr, *scr,
              blk, hg, topk, seq):
    i = pl.program_id(1)

    @pl.when(pl.program_id(2) == 0)
    def _select():
        _dsa_select(i, qi_ref, ka_ref, kb_ref, w_ref, sel_scr, jc_scr, blk=blk, topk=topk, seq=seq)

    s_scr, heads = _head_state(scr, hg)
    for st in heads:
        _attn_init(*st)
    qts = [_key_major(q_ref[0, :, g * LANE:(g + 1) * LANE]) for g in range(hg)]

    def scores(j, slot):
        rows = _tile_rows(j, blk)
        for g in range(hg):
            s_scr[slot, g] = jnp.dot(k_ref[0, rows, g * LANE:(g + 1) * LANE], qts[g], preferred_element_type=F32)

    def consume(j, slot, masked):
        rows = _tile_rows(j, blk)
        bias = lax.bitcast_convert_type(sel_scr[j], F32)
        for g, st in enumerate(heads):
            _attn_step(s_scr[slot, g] + bias, v_ref[0, rows, g * LANE:(g + 1) * LANE], *st)

    _pipelined_key_tiles(i, scores, consume)
    for g, (_, l_scr, acc_scr) in enumerate(heads):
        o_ref[0, :, g * LANE:(g + 1) * LANE] = _attn_out(l_scr, acc_scr, o_ref.dtype)


def _dsa_attention(r64, w_idx, qk, v, *, blk, hg=_HEADS_PER_STEP):
    b, s, _ = v.shape
    h = DSA_HEADS
    topk = min(DSA_TOPK_MAX, s // 4)
    return pl.pallas_call(
        functools.partial(_dsa_body, blk=blk, hg=hg, topk=topk, seq=s),
        grid=(b, s // blk, h // hg),
        in_specs=[pl.BlockSpec((1, blk, 8 * LANE), lambda b_, i, h_: (b_, i, 0)),
                  pl.BlockSpec((1, s, LANE), lambda b_, i, h_: (b_, 0, _R64_KIA)),
                  pl.BlockSpec((1, s, LANE), lambda b_, i, h_: (b_, 0, _R64_KIB)),
                  pl.BlockSpec((1, blk, LANE), lambda b_, i, h_: (b_, i, 0)),
                  pl.BlockSpec((1, blk, hg * LANE), lambda b_, i, h_: (b_, i, h_)),
                  pl.BlockSpec((1, s, hg * LANE), lambda b_, i, h_: (b_, 0, h // hg + h_)),
                  pl.BlockSpec((1, s, hg * LANE), lambda b_, i, h_: (b_, 0, h_))],
        out_specs=pl.BlockSpec((1, blk, hg * LANE), lambda b_, i, h_: (b_, i, h_)),
        out_shape=jax.ShapeDtypeStruct((b, s, h * HEAD_DIM), _CD),
        scratch_shapes=[pltpu.VMEM((s // blk, blk, blk), jnp.int32),
                        pltpu.VMEM((1, blk), jnp.int32)] + _attn_scratch(blk, hg),
        compiler_params=_params("parallel", "arbitrary", "arbitrary"),
        name="dsa_attention",
    )(r64, r64, r64, w_idx, qk, qk, v)


def _moba_body(q_ref, k_ref, v_ref, o_ref, kmean_scr, pick_scr, *scr, blk, hg, seq, nbp):
    i = pl.program_id(2)
    nper = blk // MOBA_BLOCK
    shift = MOBA_BLOCK.bit_length() - 1
    s_scr, heads = _head_state(scr, hg)

    @pl.when(i == 0)
    def _block_means():
        r = lax.broadcasted_iota(jnp.int32, (nbp, seq), 0)
        c = lax.broadcasted_iota(jnp.int32, (nbp, seq), 1)
        avg = jnp.where((c >> shift) == r, 1.0 / MOBA_BLOCK, 0.0).astype(k_ref.dtype)
        for g in range(hg):
            kmean_scr[g] = jnp.dot(avg, k_ref[0, :, g * LANE:(g + 1) * LANE],
                                   preferred_element_type=F32).astype(kmean_scr.dtype)

    qts = [_key_major(q_ref[0, :, g * LANE:(g + 1) * LANE]) for g in range(hg)]
    kblk = lax.broadcasted_iota(jnp.int32, (nbp, blk), 0)
    own = (i * blk + lax.broadcasted_iota(jnp.int32, (nbp, blk), 1)) >> shift
    kblk_f = kblk.astype(F32)
    for g, st in enumerate(heads):
        _attn_init(*st)
        gate = jnp.where(kblk < own, jnp.dot(kmean_scr[g], qts[g], preferred_element_type=F32), _NEG)
        pick = jnp.zeros((nbp, blk), F32)
        for _ in range(MOBA_TOPK):
            best = jnp.max(gate, axis=0, keepdims=True)
            hit = (gate == best) & (best > 0.5 * _NEG)
            first = jnp.min(jnp.where(hit, kblk_f, float(nbp)), axis=0, keepdims=True)
            new = kblk_f == first
            pick = jnp.where(new, 1.0, pick)
            gate = jnp.where(new, _NEG, gate)
        pick_scr[g] = pick

    def scores(j, slot):
        rows = _tile_rows(j, blk)
        for g in range(hg):
            s_scr[slot, g] = jnp.dot(k_ref[0, rows, g * LANE:(g + 1) * LANE], qts[g], preferred_element_type=F32)

    kloc = lax.broadcasted_iota(jnp.int32, (MOBA_BLOCK, blk), 0)
    qloc = lax.broadcasted_iota(jnp.int32, (MOBA_BLOCK, blk), 1)

    def consume(j, slot, masked):
        rows = _tile_rows(j, blk)
        for g, st in enumerate(heads):
            parts = []
            for u in range(nper):
                s = s_scr[slot, g, u * MOBA_BLOCK:(u + 1) * MOBA_BLOCK, :]
                ok = pick_scr[g, pl.ds(j * nper + u, 1), :] > 0.5
                if masked:
                    kpos = u * MOBA_BLOCK + kloc
                    same = (qloc >> shift) == u
                    ok = (same & (kpos <= qloc)) | (jnp.logical_not(same) & ok & ((qloc >> shift) > u))
                parts.append(jnp.where(ok, s, _NEG))
            _attn_step(jnp.concatenate(parts, axis=0), v_ref[0, rows, g * LANE:(g + 1) * LANE], *st)

    _pipelined_key_tiles(i, scores, consume)
    for g, (_, l_scr, acc_scr) in enumerate(heads):
        o_ref[0, :, g * LANE:(g + 1) * LANE] = _attn_out(l_scr, acc_scr, o_ref.dtype)


def _moba_attention(qk, v, *, blk, q_off, k_off, v_off, hg=_HEADS_PER_STEP):
    b, s, _ = qk.shape
    h = MOBA_HEADS
    assert s % MOBA_BLOCK == 0 and blk % MOBA_BLOCK == 0
    nbp = -(-(s // MOBA_BLOCK) // SUBLANE) * SUBLANE
    return pl.pallas_call(
        functools.partial(_moba_body, blk=blk, hg=hg, seq=s, nbp=nbp),
        grid=(b, h // hg, s // blk),
        in_specs=[pl.BlockSpec((1, blk, hg * LANE), lambda b_, h_, i: (b_, i, q_off // hg + h_)),
                  pl.BlockSpec((1, s, hg * LANE), lambda b_, h_, i: (b_, 0, k_off // hg + h_)),
                  pl.BlockSpec((1, s, hg * LANE), lambda b_, h_, i: (b_, 0, v_off // hg + h_))],
        out_specs=pl.BlockSpec((1, blk, hg * LANE), lambda b_, h_, i: (b_, i, h_)),
        out_shape=jax.ShapeDtypeStruct((b, s, h * HEAD_DIM), _CD),
        scratch_shapes=[pltpu.VMEM((hg, nbp, HEAD_DIM), _CD),
                        pltpu.VMEM((hg, nbp, blk), F32)] + _attn_scratch(blk, hg),
        compiler_params=_params("parallel", "parallel", "arbitrary"),
        name="moba_attention",
    )(qk, qk, v)


def _dilated_body(q_ref, k_ref, v_ref, bias_ref, o_ref, *scr, blk, hg, nrel):
    i = pl.program_id(2)
    s_scr, heads = _head_state(scr, hg)
    for st in heads:
        _attn_init(*st)
    qts = [_key_major(q_ref[0, :, g * LANE:(g + 1) * LANE]) for g in range(hg)]

    def scores(j, slot):
        rows = _tile_rows(j, blk)
        for g in range(hg):
            s_scr[slot, g] = jnp.dot(k_ref[0, rows, g * LANE:(g + 1) * LANE], qts[g], preferred_element_type=F32)

    def consume(j, slot, masked):
        rows = _tile_rows(j, blk)
        bias = bias_ref[i - j]
        for g, st in enumerate(heads):
            _attn_step(s_scr[slot, g] + bias, v_ref[0, rows, g * LANE:(g + 1) * LANE], *st)

    _pipelined_key_tiles(i, scores, consume, first=jnp.maximum(i - (nrel - 1), 0))
    for g, (_, l_scr, acc_scr) in enumerate(heads):
        o_ref[0, :, g * LANE:(g + 1) * LANE] = _attn_out(l_scr, acc_scr, o_ref.dtype)


def _dilated_bias(blk):
    reach = max(w for w, _ in DIL_PATTERNS)
    nrel = -(-reach // blk) + 1
    rel = jnp.arange(nrel, dtype=jnp.int32)[:, None, None]
    k = jnp.arange(blk, dtype=jnp.int32)[None, :, None]
    q = jnp.arange(blk, dtype=jnp.int32)[None, None, :]
    d = rel * blk + q - k
    mult = jnp.zeros(d.shape, F32)
    for window, dil in DIL_PATTERNS:
        mult = mult + ((d >= 0) & (d <= (window // dil) * dil) & (d % dil == 0)).astype(F32)
    return jnp.where(mult > 0, jnp.log2(jnp.maximum(mult, 1.0)), _NEG), nrel


def _dilated_attention(qk, v, *, blk, q_off, k_off, v_off, hg=_HEADS_PER_STEP):
    b, s, _ = qk.shape
    h = DIL_HEADS
    bias, nrel = _dilated_bias(blk)
    return pl.pallas_call(
        functools.partial(_dilated_body, blk=blk, hg=hg, nrel=nrel),
        grid=(b, h // hg, s // blk),
        in_specs=[pl.BlockSpec((1, blk, hg * LANE), lambda b_, h_, i: (b_, i, q_off // hg + h_)),
                  pl.BlockSpec((1, s, hg * LANE), lambda b_, h_, i: (b_, 0, k_off // hg + h_)),
                  pl.BlockSpec((1, s, hg * LANE), lambda b_, h_, i: (b_, 0, v_off // hg + h_)),
                  pl.BlockSpec((nrel, blk, blk), lambda b_, h_, i: (0, 0, 0))],
        out_specs=pl.BlockSpec((1, blk, hg * LANE), lambda b_, h_, i: (b_, i, h_)),
        out_shape=jax.ShapeDtypeStruct((b, s, h * HEAD_DIM), _CD),
        scratch_shapes=_attn_scratch(blk, hg),
        compiler_params=_params("parallel", "parallel", "arbitrary"),
        name="dilated_attention",
    )(qk, qk, v, bias)


_E_CQ, _E_CKV, _E_KPE, _E_Q, _E_K, _E_V, _E_QI, _E_KI, _E_WI = [
    int(o) for o in np.cumsum([0, MLA_Q_RANK, MLA_KV_RANK, MLA_ROPE, DSA_HEADS * HEAD_DIM, DSA_HEADS * HEAD_DIM,
                               DSA_HEADS * HEAD_DIM, IDX_HEADS * IDX_DIM, IDX_DIM])]
_R64_KIA, _R64_KIB, _R64_KPE, _R64_SLABS = 8, 9, 10, 12


def _take_cols(w, idx):
    idx = np.asarray(idx)
    cols = jnp.take(w, jnp.asarray(np.maximum(idx, 0)), axis=1)
    return jnp.where(jnp.asarray(idx >= 0)[None, :], cols, 0.0).astype(_CD)


def _r64_columns():
    half = IDX_DIM // 2
    a = np.arange(half)
    z = -np.ones(half, np.int64)
    cols = []
    for p in range(IDX_HEADS // 2):
        ha, hb = _E_QI + 2 * p * IDX_DIM, _E_QI + (2 * p + 1) * IDX_DIM
        cols += [ha + a, hb + a, ha + half + a, hb + half + a]
    cols += [_E_KI + a, z, _E_KI + half + a, z]
    cols += [z, _E_KI + a, z, _E_KI + half + a]
    cols += [_E_KPE + a, z, _E_KPE + half + a, z]
    cols += [z, z, z, z]
    return np.concatenate(cols)


def _uq_columns():
    half = MLA_ROPE // 2
    a = np.arange(half)
    z = -np.ones(half, np.int64)
    cols = []
    for h in range(MLA_HEADS):
        o = h * (MLA_NOPE + MLA_ROPE)
        cols += [o + np.arange(MLA_NOPE), o + MLA_NOPE + a, z, o + MLA_NOPE + half + a, z]
    return np.concatenate(cols)


def _ukv_columns():
    per = MLA_NOPE + MLA_V
    kn = [h * per + np.arange(MLA_NOPE) for h in range(MLA_HEADS)]
    vv = [h * per + MLA_NOPE + np.arange(MLA_V) for h in range(MLA_HEADS)]
    return np.concatenate(kn + vv)


def _rope_tables(seq, dim, scales):
    inv = ROPE_THETA ** (-jnp.arange(0, dim, 2, dtype=F32) / dim)
    ang = jnp.arange(seq, dtype=F32)[:, None] * inv[None, :]
    reps = (LANE // 2) // (dim // 2)
    cos = jnp.tile(jnp.cos(ang), (1, 2 * reps))
    sin = jnp.tile(jnp.sin(ang), (1, reps))
    sin = jnp.concatenate([-sin, sin], axis=1)
    sc = jnp.asarray(scales, F32)[:, None, None]
    return cos[None] * sc, sin[None] * sc


def _mlp_block(x2, g, w1, w2, *, tm):
    hm = _rmsnorm(x2, g, _CD)
    up = _matmul(hm, w1.astype(_CD), tm=tm, tn=1024, tk=w1.shape[0], out_dtype=_CD,
                 epilogue=_ep_relu2, name="mlp_up")
    res = pl.BlockSpec((tm, 1024), lambda i, j, k: (i, j))
    return _matmul(up, w2.astype(_CD), tm=tm, tn=1024, tk=2048, out_dtype=F32,
                   epilogue=_ep_residual, extra=(x2,), extra_specs=(res,), name="mlp_down")


def _out_proj(o, w_out, x2, *, tm):
    res = pl.BlockSpec((tm, 1024), lambda i, j, k: (i, j))
    return _matmul(o, w_out.astype(_CD), tm=tm, tn=1024, tk=w_out.shape[0], out_dtype=F32,
                   epilogue=_ep_residual, extra=(x2,), extra_specs=(res,), name="out_proj")


def _even_mixer(x2, g_mix, w_in, g_q, g_kv, w_uq, w_ukv, w_out, *, batch, seq, blk):
    t, d = x2.shape
    tm = blk
    hm = _rmsnorm(x2, g_mix, _CD)
    mla_scale = _LOG2E * (MLA_NOPE + MLA_ROPE) ** -0.5
    rope64 = _rope_tables(seq, IDX_DIM, (1.0, mla_scale))
    rope128 = _rope_tables(seq, HEAD_DIM, (_LOG2E * HEAD_DIM ** -0.5, 1.0))
    nh = DSA_HEADS * HEAD_DIM

    r64 = _rope_matmul(hm, _take_cols(w_in, _r64_columns()), rope64, lambda j: 0, seq=seq, tm=tm, tn=512,
                       pattern=(True,) * 4, name="in_proj_rope64")
    qk = _rope_matmul(hm, w_in[:, _E_Q:_E_V].astype(_CD), rope128, lambda j: j // (nh // 512), seq=seq, tm=tm,
                      tn=512, pattern=(True,) * 4, name="in_proj_rope128")
    v = _matmul(hm, w_in[:, _E_V:_E_QI].astype(_CD), tm=tm, tn=512, tk=d, out_dtype=_CD,
                epilogue=_ep_scale(1.0), name="in_proj_v")
    w_idx = _matmul(hm, _take_cols(w_in, np.concatenate([_E_WI + np.arange(IDX_HEADS),
                                                          -np.ones(LANE - IDX_HEADS, np.int64)])),
                    tm=tm, tn=LANE, tk=d, out_dtype=F32,
                    epilogue=_ep_scale(IDX_HEADS ** -0.5 * IDX_DIM ** -0.5), name="in_proj_widx")
    gspec = lambda n: pl.BlockSpec((1, n), lambda i, j, k: (0, 0))
    c_q = _matmul(hm, w_in[:, _E_CQ:_E_CKV].astype(_CD), tm=tm, tn=MLA_Q_RANK, tk=d, out_dtype=_CD,
                  epilogue=_ep_rmsnorm, extra=(g_q.reshape(1, -1),), extra_specs=(gspec(MLA_Q_RANK),),
                  name="in_proj_cq")
    c_kv = _matmul(hm, w_in[:, _E_CKV:_E_KPE].astype(_CD), tm=tm, tn=MLA_KV_RANK, tk=d, out_dtype=_CD,
                   epilogue=_ep_rmsnorm, extra=(g_kv.reshape(1, -1),), extra_specs=(gspec(MLA_KV_RANK),),
                   name="in_proj_ckv")

    q_mla = _rope_matmul(c_q, _take_cols(w_uq, _uq_columns()), rope64, lambda j: 1, seq=seq, tm=tm, tn=512,
                         pattern=(False, True) * 2, plain_scale=mla_scale, name="mla_q_up")
    kv_mla = _matmul(c_kv, _take_cols(w_ukv, _ukv_columns()), tm=tm, tn=512, tk=MLA_KV_RANK, out_dtype=_CD,
                     epilogue=_ep_scale(1.0), name="mla_kv_up")

    sh = lambda z: z.reshape(batch, seq, z.shape[-1])
    a = _mla_attention(sh(q_mla), sh(kv_mla), sh(r64), blk=blk)
    bsa = _dsa_attention(sh(r64), sh(w_idx), sh(qk), sh(v), blk=blk)
    o = jnp.concatenate([a, bsa], axis=-1).reshape(t, -1)
    return _out_proj(o, w_out, x2, tm=tm)


def _odd_mixer(x2, g_mix, w_in, w_out, *, batch, seq, blk):
    t, d = x2.shape
    tm = blk
    hm = _rmsnorm(x2, g_mix, _CD)
    nh = MOBA_HEADS * HEAD_DIM
    rope128 = _rope_tables(seq, HEAD_DIM, (_LOG2E * HEAD_DIM ** -0.5, 1.0))
    w_qk = jnp.concatenate([w_in[:, 0:2 * nh], w_in[:, 3 * nh:5 * nh]], axis=1).astype(_CD)
    w_v = jnp.concatenate([w_in[:, 2 * nh:3 * nh], w_in[:, 5 * nh:6 * nh]], axis=1).astype(_CD)
    qk = _rope_matmul(hm, w_qk, rope128, lambda j: (j // (nh // 512)) % 2, seq=seq, tm=tm, tn=512,
                      pattern=(True,) * 4, name="in_proj_rope128")
    v = _matmul(hm, w_v, tm=tm, tn=512, tk=d, out_dtype=_CD, epilogue=_ep_scale(1.0), name="in_proj_v")
    sh = lambda z: z.reshape(batch, seq, z.shape[-1])
    hs = MOBA_HEADS
    c = _moba_attention(sh(qk), sh(v), blk=blk, q_off=0, k_off=hs, v_off=0)
    dl = _dilated_attention(sh(qk), sh(v), blk=blk, q_off=2 * hs, k_off=3 * hs, v_off=hs)
    o = jnp.concatenate([c, dl], axis=-1).reshape(t, -1)
    return _out_proj(o, w_out, x2, tm=tm)


def kernel(x, ln_mix, ln_mlp, ln_final, e_w_in, e_g_q, e_g_kv, e_w_uq, e_w_ukv, e_w_out,
           o_w_in, o_w_out, mlp_w1, mlp_w2):
    batch, seq, d = x.shape
    blk = min(512, seq)
    assert seq % blk == 0 and blk % MOBA_BLOCK == 0
    x2 = x.reshape(batch * seq, d)
    depth = ln_mix.shape[0]
    for layer in range(depth):
        j = layer // 2
        if layer % 2 == 0:
            x2 = _even_mixer(x2, ln_mix[layer], e_w_in[j], e_g_q[j], e_g_kv[j], e_w_uq[j], e_w_ukv[j],
                             e_w_out[j], batch=batch, seq=seq, blk=blk)
        else:
            x2 = _odd_mixer(x2, ln_mix[layer], o_w_in[j], o_w_out[j], batch=batch, seq=seq, blk=blk)
        x2 = _mlp_block(x2, ln_mlp[layer], mlp_w1[layer], mlp_w2[layer], tm=min(2 * blk, batch * seq))
    return _rmsnorm(x2, ln_final, x.dtype).reshape(batch, seq, d)
```

```python
import functools

import numpy as np
import jax
import jax.numpy as jnp
from jax import lax
from jax.experimental import pallas as pl
from jax.experimental.pallas import tpu as pltpu

HEAD_DIM = 128
ROPE_THETA = 10000.0
NORM_EPS = 1e-6
MLA_HEADS, MLA_Q_RANK, MLA_KV_RANK, MLA_NOPE, MLA_ROPE, MLA_V = 8, 512, 256, 128, 64, 128
DSA_HEADS, IDX_HEADS, IDX_DIM, DSA_TOPK_MAX = 8, 16, 64, 256
MOBA_HEADS, MOBA_BLOCK, MOBA_TOPK = 8, 256, 3
DIL_HEADS = 8
DIL_PATTERNS = ((128, 1), (512, 4), (2048, 16))

LANE = 128
SUBLANE = 8
VMEM_LIMIT_BYTES = 56 * 2**20

F32 = jnp.float32
_CD = jnp.bfloat16
_NEG = -1e30
_LOG2E = 1.4426950408889634
_INT_MIN = -(2**31)
_INT_MAX = 2**31 - 1
_KEY_NEG_INF = int(np.int32(np.uint32(0xFF800000) ^ np.uint32(0x7FFFFFFF)))
_HEADS_PER_STEP = 2
_INTERP_PROBES = 40


def _params(*sem):
    return pltpu.CompilerParams(dimension_semantics=sem, vmem_limit_bytes=VMEM_LIMIT_BYTES)


def _rmsnorm_body(x_ref, g_ref, o_ref):
    x = x_ref[...].astype(F32)
    y = x * lax.rsqrt(jnp.mean(x * x, axis=-1, keepdims=True) + NORM_EPS)
    o_ref[...] = (y * g_ref[...]).astype(o_ref.dtype)


def _rmsnorm(x, g, out_dtype, tm=512):
    m, d = x.shape
    return pl.pallas_call(
        _rmsnorm_body,
        grid=(m // tm,),
        in_specs=[pl.BlockSpec((tm, d), lambda i: (i, 0)), pl.BlockSpec((1, d), lambda i: (0, 0))],
        out_specs=pl.BlockSpec((tm, d), lambda i: (i, 0)),
        out_shape=jax.ShapeDtypeStruct((m, d), out_dtype),
        compiler_params=_params("parallel"),
        name="rmsnorm",
    )(x, g.reshape(1, d).astype(F32))


def _rope_slab(y, cos, sin):
    return y * cos + pltpu.roll(y, LANE // 2, 1) * sin


def _matmul_body(*refs, n_extra, epilogue, out_dtype):
    a_ref, w_ref = refs[0], refs[1]
    extra = refs[2:2 + n_extra]
    o_ref, acc_ref = refs[2 + n_extra], refs[3 + n_extra]
    k = pl.program_id(2)

    @pl.when(k == 0)
    def _init():
        acc_ref[...] = jnp.zeros_like(acc_ref)

    acc_ref[...] += jnp.dot(a_ref[...], w_ref[...], preferred_element_type=F32)

    @pl.when(k == pl.num_programs(2) - 1)
    def _finish():
        o_ref[...] = epilogue(acc_ref[...], *extra).astype(out_dtype)


def _matmul(a, w, *, tm, tn, tk, out_dtype, epilogue, extra=(), extra_specs=(), name):
    m, kd = a.shape
    n = w.shape[1]
    assert m % tm == 0 and n % tn == 0 and kd % tk == 0, (a.shape, w.shape, tm, tn, tk)
    body = functools.partial(_matmul_body, n_extra=len(extra), epilogue=epilogue, out_dtype=out_dtype)
    return pl.pallas_call(
        body,
        grid=(m // tm, n // tn, kd // tk),
        in_specs=[pl.BlockSpec((tm, tk), lambda i, j, k: (i, k)),
                  pl.BlockSpec((tk, tn), lambda i, j, k: (k, j))] + list(extra_specs),
        out_specs=pl.BlockSpec((tm, tn), lambda i, j, k: (i, j)),
        out_shape=jax.ShapeDtypeStruct((m, n), out_dtype),
        scratch_shapes=[pltpu.VMEM((tm, tn), F32)],
        compiler_params=_params("parallel", "parallel", "arbitrary"),
        name=name,
    )(a, w, *extra)


def _norm_matmul_body(*refs, n_extra, epilogue, out_dtype):
    x_ref, g_ref, w_ref = refs[:3]
    extra = refs[3:3 + n_extra]
    o_ref, h_scr = refs[3 + n_extra], refs[4 + n_extra]

    @pl.when(pl.program_id(1) == 0)
    def _norm():
        x = x_ref[...]
        y = x * lax.rsqrt(jnp.mean(x * x, axis=-1, keepdims=True) + NORM_EPS)
        h_scr[...] = (y * g_ref[...]).astype(h_scr.dtype)

    y = jnp.dot(h_scr[...], w_ref[...], preferred_element_type=F32)
    o_ref[...] = epilogue(y, *extra).astype(out_dtype)


def _norm_matmul(x, g, w, *, tm, tn, out_dtype, epilogue, extra=(), extra_specs=(), name):
    m, d = x.shape
    n = w.shape[1]
    assert m % tm == 0 and n % tn == 0, (x.shape, w.shape, tm, tn)
    body = functools.partial(_norm_matmul_body, n_extra=len(extra), epilogue=epilogue, out_dtype=out_dtype)
    return pl.pallas_call(
        body,
        grid=(m // tm, n // tn),
        in_specs=[pl.BlockSpec((tm, d), lambda i, j: (i, 0)),
                  pl.BlockSpec((1, d), lambda i, j: (0, 0)),
                  pl.BlockSpec((d, tn), lambda i, j: (0, j))] + list(extra_specs),
        out_specs=pl.BlockSpec((tm, tn), lambda i, j: (i, j)),
        out_shape=jax.ShapeDtypeStruct((m, n), out_dtype),
        scratch_shapes=[pltpu.VMEM((tm, d), _CD)],
        compiler_params=_params("parallel", "arbitrary"),
        name=name,
    )(x, g.reshape(1, d).astype(F32), w, *extra)


def _ep_scale(scale):
    def ep(y):
        return y if scale == 1.0 else y * scale
    return ep


def _ep_relu2(y):
    return jnp.square(jnp.maximum(y, 0.0))


def _ep_residual(y, r_ref):
    return y + r_ref[...]


def _ep_rmsnorm(y, g_ref):
    return y * lax.rsqrt(jnp.mean(y * y, axis=-1, keepdims=True) + NORM_EPS) * g_ref[...]


def _ep_rope(pattern, plain_scale):
    def ep(y, cos_ref, sin_ref):
        cos, sin = cos_ref[0], sin_ref[0]
        out = []
        for c, rot in enumerate(pattern):
            slab = y[:, c * LANE:(c + 1) * LANE]
            out.append(_rope_slab(slab, cos, sin) if rot else slab * plain_scale)
        return jnp.concatenate(out, axis=1)
    return ep


def _rope_matmul(a, w, tabs, tab_of_tile, *, seq, tm, tn, pattern, plain_scale=1.0, norm_gain=None, name):
    nblk = seq // tm
    spec = pl.BlockSpec((1, tm, LANE), lambda i, j, *_: (tab_of_tile(j), i % nblk, 0))
    common = dict(tm=tm, tn=tn, out_dtype=_CD, epilogue=_ep_rope(pattern, plain_scale), extra=tabs,
                  extra_specs=(spec, spec), name=name)
    if norm_gain is None:
        return _matmul(a, w, tk=a.shape[1], **common)
    return _norm_matmul(a, norm_gain, w, **common)


def _attn_init(m_scr, l_scr, acc_scr):
    m_scr[...] = jnp.full(m_scr.shape, _NEG, F32)
    l_scr[...] = jnp.zeros(l_scr.shape, F32)
    acc_scr[...] = jnp.zeros(acc_scr.shape, F32)


def _tile_rows(j, blk):
    return pl.ds(pl.multiple_of(j * blk, blk), blk)


def _dot_tn(a, b):
    return lax.dot_general(a, b, (((0,), (0,)), ((), ())), preferred_element_type=F32)


def _key_major(q):
    return q.astype(F32).T.astype(q.dtype)


def _attn_scratch(blk, hg):
    per_head = [pltpu.VMEM((1, blk), F32), pltpu.VMEM((1, blk), F32), pltpu.VMEM((HEAD_DIM, blk), F32)]
    return [pltpu.VMEM((2, hg, blk, blk), F32)] + per_head * hg


def _head_state(scr, hg):
    return scr[0], [scr[1 + 3 * g:4 + 3 * g] for g in range(hg)]


def _attn_step(st, v, m_scr, l_scr, acc_scr):
    m_prev = m_scr[...]
    m_new = jnp.maximum(m_prev, jnp.max(st, axis=0, keepdims=True))
    alpha = jnp.exp2(m_prev - m_new)
    p = jnp.exp2(st - m_new)
    l_scr[...] = alpha * l_scr[...] + jnp.sum(p, axis=0, keepdims=True)
    acc_scr[...] = alpha * acc_scr[...] + _dot_tn(v, p.astype(v.dtype))
    m_scr[...] = m_new


def _attn_out(l_scr, acc_scr, dtype):
    return (acc_scr[...] / l_scr[...]).T.astype(dtype)


def _pipelined_key_tiles(last, scores, consume, first=0):
    n_past = last - first

    def pair(t, carry):
        j = first + 2 * t
        scores(j + 1, 1)
        consume(j, 0, False)
        scores(j + 2, 0)
        consume(j + 1, 1, False)
        return carry

    scores(first, 0)
    lax.fori_loop(0, n_past // 2, pair, 0)

    @pl.when(n_past % 2 == 1)
    def _odd():
        scores(last, 1)
        consume(last - 1, 0, False)
        consume(last, 1, True)

    @pl.when(n_past % 2 == 0)
    def _even():
        consume(last, 0, True)


def _causal(s):
    kpos = lax.broadcasted_iota(jnp.int32, s.shape, 0)
    qpos = lax.broadcasted_iota(jnp.int32, s.shape, 1)
    return jnp.where(kpos <= qpos, s, _NEG)


def _mla_body(q_ref, kn_ref, kp_ref, v_ref, o_ref, *scr, blk, hg):
    i = pl.program_id(2)
    s_scr, heads = _head_state(scr, hg)
    for st in heads:
        _attn_init(*st)
    qts = [_key_major(q_ref[0, :, g * 2 * LANE:(g + 1) * 2 * LANE]) for g in range(hg)]

    def scores(j, slot):
        rows = _tile_rows(j, blk)
        kp = kp_ref[0, rows, :]
        for g in range(hg):
            k = jnp.concatenate([kn_ref[0, rows, g * LANE:(g + 1) * LANE], kp], axis=1)
            s_scr[slot, g] = jnp.dot(k, qts[g], preferred_element_type=F32)

    def consume(j, slot, masked):
        rows = _tile_rows(j, blk)
        for g, st in enumerate(heads):
            s = s_scr[slot, g]
            _attn_step(_causal(s) if masked else s, v_ref[0, rows, g * LANE:(g + 1) * LANE], *st)

    _pipelined_key_tiles(i, scores, consume)
    for g, (_, l_scr, acc_scr) in enumerate(heads):
        o_ref[0, :, g * LANE:(g + 1) * LANE] = _attn_out(l_scr, acc_scr, o_ref.dtype)


def _mla_attention(q, kv, r64, *, blk, hg=_HEADS_PER_STEP):
    b, s, _ = q.shape
    h = MLA_HEADS
    return pl.pallas_call(
        functools.partial(_mla_body, blk=blk, hg=hg),
        grid=(b, h // hg, s // blk),
        in_specs=[pl.BlockSpec((1, blk, hg * 2 * LANE), lambda b_, h_, i: (b_, i, h_)),
                  pl.BlockSpec((1, s, hg * LANE), lambda b_, h_, i: (b_, 0, h_)),
                  pl.BlockSpec((1, s, LANE), lambda b_, h_, i: (b_, 0, _R64_KPE)),
                  pl.BlockSpec((1, s, hg * LANE), lambda b_, h_, i: (b_, 0, h // hg + h_))],
        out_specs=pl.BlockSpec((1, blk, hg * LANE), lambda b_, h_, i: (b_, i, h_)),
        out_shape=jax.ShapeDtypeStruct((b, s, h * MLA_V), _CD),
        scratch_shapes=_attn_scratch(blk, hg),
        compiler_params=_params("parallel", "parallel", "arbitrary"),
        name="mla_attention",
    )(q, kv, r64, kv)


def _order_key(x):
    bits = lax.bitcast_convert_type(x, jnp.int32)
    return bits ^ ((bits >> 31) & _INT_MAX)


def _key_score(key):
    return lax.bitcast_convert_type(key ^ ((key >> 31) & _INT_MAX), F32)


def _dsa_select(i, qi_ref, ka_ref, kb_ref, w_ref, sel_scr, jc_scr, *, blk, topk, seq):
    wt = w_ref[0].T
    qits = [_key_major(qi_ref[0, :, p * LANE:(p + 1) * LANE]) for p in range(IDX_HEADS // 2)]
    kloc = lax.broadcasted_iota(jnp.int32, (blk, blk), 0)
    qpos = i * blk + lax.broadcasted_iota(jnp.int32, (blk, blk), 1)

    def score_tile(c, carry):
        rows = _tile_rows(c, blk)
        ka, kb = ka_ref[0, rows, :], kb_ref[0, rows, :]
        acc = jnp.zeros((blk, blk), F32)
        for p in range(IDX_HEADS // 2):
            da = jnp.dot(ka, qits[p], preferred_element_type=F32)
            db = jnp.dot(kb, qits[p], preferred_element_type=F32)
            acc = acc + wt[2 * p:2 * p + 1, :] * jnp.maximum(da, 0.0) + wt[2 * p + 1:2 * p + 2, :] * jnp.maximum(db, 0.0)
        key = _order_key(acc)
        causal = c * blk + kloc <= qpos
        sel_scr[c] = jnp.where(causal, key, _KEY_NEG_INF)
        lo8, hi8 = carry
        fold = lambda x: x.reshape(blk // SUBLANE, SUBLANE, blk)
        lo8 = jnp.minimum(lo8, jnp.min(fold(jnp.where(causal, key, _INT_MAX)), axis=0))
        hi8 = jnp.maximum(hi8, jnp.max(fold(jnp.where(causal, key, _INT_MIN)), axis=0))
        return lo8, hi8

    lo8, hi8 = lax.fori_loop(0, i + 1, score_tile, (jnp.full((SUBLANE, blk), _INT_MAX, jnp.int32),
                                                    jnp.full((SUBLANE, blk), _INT_MIN, jnp.int32)))
    key_min, key_max = lo8[0:1], hi8[0:1]
    for r in range(1, SUBLANE):
        key_min = jnp.minimum(key_min, lo8[r:r + 1])
        key_max = jnp.maximum(key_max, hi8[r:r + 1])

    n_causal = i * blk + lax.broadcasted_iota(jnp.int32, (1, blk), 1) + 1
    kk = jnp.minimum(topk, n_causal).astype(F32)

    def count(pred):
        def body(c, cnt):
            hit = jnp.where(pred(sel_scr[c], c * blk + kloc), 1.0, 0.0)
            return cnt + jnp.sum(hit.reshape(blk // SUBLANE, SUBLANE, blk), axis=0)
        cnt = lax.fori_loop(0, i + 1, body, jnp.zeros((SUBLANE, blk), F32))
        return jnp.sum(cnt, axis=0, keepdims=True)

    def unfinished(lo, hi, n_lo):
        return (n_lo > kk) & (hi > lo + 1)

    def probe(state):
        lo, hi, n_lo, n_hi, w_lo, w_hi, last, it, _ = state
        lo_f, hi_f = _key_score(lo), _key_score(hi)
        a, b = (n_lo - kk + 0.5) * w_lo, (kk - 0.5 - n_hi) * w_hi
        interp = _order_key(lo_f + (hi_f - lo_f) * (a / (a + b)))
        mid = (lo >> 1) + (hi >> 1) + (lo & hi & 1)
        cand = jnp.clip(jnp.where(it < _INTERP_PROBES, interp, mid), lo + 1, hi - 1)
        n = count(lambda keys, kpos: keys >= cand)
        live = unfinished(lo, hi, n_lo)
        up = live & (n >= kk)
        down = live & (n < kk)
        lo, n_lo = jnp.where(up, cand, lo), jnp.where(up, n, n_lo)
        hi, n_hi = jnp.where(down, cand, hi), jnp.where(down, n, n_hi)
        w_hi = jnp.where(up, jnp.where(last > 0.5, 0.5 * w_hi, 1.0), jnp.where(down, 1.0, w_hi))
        w_lo = jnp.where(down, jnp.where(last < -0.5, 0.5 * w_lo, 1.0), jnp.where(up, 1.0, w_lo))
        last = jnp.where(up, 1.0, jnp.where(down, -1.0, last))
        more = jnp.max(jnp.where(unfinished(lo, hi, n_lo), 1.0, 0.0))
        return lo, hi, n_lo, n_hi, w_lo, w_hi, last, it + 1, more

    n_all = n_causal.astype(F32)
    one, zero = jnp.ones((1, blk), F32), jnp.zeros((1, blk), F32)
    start = (key_min, key_max + 1, n_all, zero, one, one, zero, jnp.int32(0),
             jnp.max(jnp.where(unfinished(key_min, key_max + 1, n_all), 1.0, 0.0)))
    t, _, n_ge = lax.while_loop(lambda s: s[8] > 0.5, probe, start)[:3]
    jc_scr[...] = jnp.full((1, blk), seq, jnp.int32)

    @pl.when(jnp.max(n_ge - kk) > 0.5)
    def _ties():
        need = kk - count(lambda keys, kpos: keys > t)
        nbits = (seq - 1).bit_length()

        def pos_step(b, jc):
            cand = jc + (jnp.int32(1) << (nbits - 1 - b))
            n_lt = count(lambda keys, kpos: (keys == t) & (kpos < cand))
            return jnp.where(n_lt < need, cand, jc)

        jc_scr[...] = lax.fori_loop(0, nbits, pos_step, jnp.zeros((1, blk), jnp.int32))

    jc = jc_scr[...]

    def to_bias(c, carry):
        keys = sel_scr[c]
        chosen = (keys > t) | ((keys == t) & (c * blk + kloc <= jc))
        sel_scr[c] = lax.bitcast_convert_type(jnp.where(chosen, 0.0, _NEG).astype(F32), jnp.int32)
        return carry

    lax.fori_loop(0, i + 1, to_bias, 0)


def _dsa_body(qi_ref, ka_ref, kb_ref, w_ref, q_ref, k_ref, v_ref, o_ref, sel_scr, jc_scr, *scr,
              blk, hg, topk, seq):
    i = pl.program_id(1)

    @pl.when(pl.program_id(2) == 0)
    def _select():
        _dsa_select(i, qi_ref, ka_ref, kb_ref, w_ref, sel_scr, jc_scr, blk=blk, topk=topk, seq=seq)

    s_scr, heads = _head_state(scr, hg)
    for st in heads:
        _attn_init(*st)
    qts = [_key_major(q_ref[0, :, g * LANE:(g + 1) * LANE]) for g in range(hg)]

    def scores(j, slot):
        rows = _tile_rows(j, blk)
        for g in range(hg):
            s_scr[slot, g] = jnp.dot(k_ref[0, rows, g * LANE:(g + 1) * LANE], qts[g], preferred_element_type=F32)

    def consume(j, slot, masked):
        rows = _tile_rows(j, blk)
        bias = lax.bitcast_convert_type(sel_scr[j], F32)
        for g, st in enumerate(heads):
            _attn_step(s_scr[slot, g] + bias, v_ref[0, rows, g * LANE:(g + 1) * LANE], *st)

    _pipelined_key_tiles(i, scores, consume)
    for g, (_, l_scr, acc_scr) in enumerate(heads):
        o_ref[0, :, g * LANE:(g + 1) * LANE] = _attn_out(l_scr, acc_scr, o_ref.dtype)


def _dsa_attention(r64, w_idx, qkv, *, blk, hg=_HEADS_PER_STEP):
    b, s, _ = qkv.shape
    h = DSA_HEADS
    topk = min(DSA_TOPK_MAX, s // 4)
    return pl.pallas_call(
        functools.partial(_dsa_body, blk=blk, hg=hg, topk=topk, seq=s),
        grid=(b, s // blk, h // hg),
        in_specs=[pl.BlockSpec((1, blk, 8 * LANE), lambda b_, i, h_: (b_, i, 0)),
                  pl.BlockSpec((1, s, LANE), lambda b_, i, h_: (b_, 0, _R64_KIA)),
                  pl.BlockSpec((1, s, LANE), lambda b_, i, h_: (b_, 0, _R64_KIB)),
                  pl.BlockSpec((1, blk, LANE), lambda b_, i, h_: (b_, i, 0)),
                  pl.BlockSpec((1, blk, hg * LANE), lambda b_, i, h_: (b_, i, h_)),
                  pl.BlockSpec((1, s, hg * LANE), lambda b_, i, h_: (b_, 0, h // hg + h_)),
                  pl.BlockSpec((1, s, hg * LANE), lambda b_, i, h_: (b_, 0, 2 * (h // hg) + h_))],
        out_specs=pl.BlockSpec((1, blk, hg * LANE), lambda b_, i, h_: (b_, i, h_)),
        out_shape=jax.ShapeDtypeStruct((b, s, h * HEAD_DIM), _CD),
        scratch_shapes=[pltpu.VMEM((s // blk, blk, blk), jnp.int32),
                        pltpu.VMEM((1, blk), jnp.int32)] + _attn_scratch(blk, hg),
        compiler_params=_params("parallel", "arbitrary", "arbitrary"),
        name="dsa_attention",
    )(r64, r64, r64, w_idx, qkv, qkv, qkv)


def _moba_body(q_ref, k_ref, v_ref, o_ref, kmean_scr, pick_scr, *scr, blk, hg, seq, nbp):
    i = pl.program_id(2)
    nper = blk // MOBA_BLOCK
    shift = MOBA_BLOCK.bit_length() - 1
    s_scr, heads = _head_state(scr, hg)

    @pl.when(i == 0)
    def _block_means():
        r = lax.broadcasted_iota(jnp.int32, (nbp, seq), 0)
        c = lax.broadcasted_iota(jnp.int32, (nbp, seq), 1)
        avg = jnp.where((c >> shift) == r, 1.0 / MOBA_BLOCK, 0.0).astype(k_ref.dtype)
        for g in range(hg):
            kmean_scr[g] = jnp.dot(avg, k_ref[0, :, g * LANE:(g + 1) * LANE],
                                   preferred_element_type=F32).astype(kmean_scr.dtype)

    qts = [_key_major(q_ref[0, :, g * LANE:(g + 1) * LANE]) for g in range(hg)]
    kblk = lax.broadcasted_iota(jnp.int32, (nbp, blk), 0)
    own = (i * blk + lax.broadcasted_iota(jnp.int32, (nbp, blk), 1)) >> shift
    kblk_f = kblk.astype(F32)
    for g, st in enumerate(heads):
        _attn_init(*st)
        gate = jnp.where(kblk < own, jnp.dot(kmean_scr[g], qts[g], preferred_element_type=F32), _NEG)
        pick = jnp.zeros((nbp, blk), F32)
        for _ in range(MOBA_TOPK):
            best = jnp.max(gate, axis=0, keepdims=True)
            hit = (gate == best) & (best > 0.5 * _NEG)
            first = jnp.min(jnp.where(hit, kblk_f, float(nbp)), axis=0, keepdims=True)
            new = kblk_f == first
            pick = jnp.where(new, 1.0, pick)
            gate = jnp.where(new, _NEG, gate)
        pick_scr[g] = pick

    def scores(j, slot):
        rows = _tile_rows(j, blk)
        for g in range(hg):
            s_scr[slot, g] = jnp.dot(k_ref[0, rows, g * LANE:(g + 1) * LANE], qts[g], preferred_element_type=F32)

    kloc = lax.broadcasted_iota(jnp.int32, (MOBA_BLOCK, blk), 0)
    qloc = lax.broadcasted_iota(jnp.int32, (MOBA_BLOCK, blk), 1)

    def consume(j, slot, masked):
        rows = _tile_rows(j, blk)
        for g, st in enumerate(heads):
            parts = []
            for u in range(nper):
                s = s_scr[slot, g, u * MOBA_BLOCK:(u + 1) * MOBA_BLOCK, :]
                ok = pick_scr[g, pl.ds(j * nper + u, 1), :] > 0.5
                if masked:
                    kpos = u * MOBA_BLOCK + kloc
                    same = (qloc >> shift) == u
                    ok = (same & (kpos <= qloc)) | (jnp.logical_not(same) & ok & ((qloc >> shift) > u))
                parts.append(jnp.where(ok, s, _NEG))
            _attn_step(jnp.concatenate(parts, axis=0), v_ref[0, rows, g * LANE:(g + 1) * LANE], *st)

    _pipelined_key_tiles(i, scores, consume)
    for g, (_, l_scr, acc_scr) in enumerate(heads):
        o_ref[0, :, g * LANE:(g + 1) * LANE] = _attn_out(l_scr, acc_scr, o_ref.dtype)


def _moba_attention(qk, v, *, blk, q_off, k_off, v_off, hg=_HEADS_PER_STEP):
    b, s, _ = qk.shape
    h = MOBA_HEADS
    assert s % MOBA_BLOCK == 0 and blk % MOBA_BLOCK == 0
    nbp = -(-(s // MOBA_BLOCK) // SUBLANE) * SUBLANE
    return pl.pallas_call(
        functools.partial(_moba_body, blk=blk, hg=hg, seq=s, nbp=nbp),
        grid=(b, h // hg, s // blk),
        in_specs=[pl.BlockSpec((1, blk, hg * LANE), lambda b_, h_, i: (b_, i, q_off // hg + h_)),
                  pl.BlockSpec((1, s, hg * LANE), lambda b_, h_, i: (b_, 0, k_off // hg + h_)),
                  pl.BlockSpec((1, s, hg * LANE), lambda b_, h_, i: (b_, 0, v_off // hg + h_))],
        out_specs=pl.BlockSpec((1, blk, hg * LANE), lambda b_, h_, i: (b_, i, h_)),
        out_shape=jax.ShapeDtypeStruct((b, s, h * HEAD_DIM), _CD),
        scratch_shapes=[pltpu.VMEM((hg, nbp, HEAD_DIM), _CD),
                        pltpu.VMEM((hg, nbp, blk), F32)] + _attn_scratch(blk, hg),
        compiler_params=_params("parallel", "parallel", "arbitrary"),
        name="moba_attention",
    )(qk, qk, v)


def _dilated_body(q_ref, k_ref, v_ref, bias_ref, o_ref, *scr, blk, hg, nrel):
    i = pl.program_id(2)
    s_scr, heads = _head_state(scr, hg)
    for st in heads:
        _attn_init(*st)
    qts = [_key_major(q_ref[0, :, g * LANE:(g + 1) * LANE]) for g in range(hg)]

    def scores(j, slot):
        rows = _tile_rows(j, blk)
        for g in range(hg):
            s_scr[slot, g] = jnp.dot(k_ref[0, rows, g * LANE:(g + 1) * LANE], qts[g], preferred_element_type=F32)

    def consume(j, slot, masked):
        rows = _tile_rows(j, blk)
        bias = bias_ref[i - j]
        for g, st in enumerate(heads):
            _attn_step(s_scr[slot, g] + bias, v_ref[0, rows, g * LANE:(g + 1) * LANE], *st)

    _pipelined_key_tiles(i, scores, consume, first=jnp.maximum(i - (nrel - 1), 0))
    for g, (_, l_scr, acc_scr) in enumerate(heads):
        o_ref[0, :, g * LANE:(g + 1) * LANE] = _attn_out(l_scr, acc_scr, o_ref.dtype)


def _dilated_bias(blk):
    reach = max(w for w, _ in DIL_PATTERNS)
    nrel = -(-reach // blk) + 1
    rel = jnp.arange(nrel, dtype=jnp.int32)[:, None, None]
    k = jnp.arange(blk, dtype=jnp.int32)[None, :, None]
    q = jnp.arange(blk, dtype=jnp.int32)[None, None, :]
    d = rel * blk + q - k
    mult = jnp.zeros(d.shape, F32)
    for window, dil in DIL_PATTERNS:
        mult = mult + ((d >= 0) & (d <= (window // dil) * dil) & (d % dil == 0)).astype(F32)
    return jnp.where(mult > 0, jnp.log2(jnp.maximum(mult, 1.0)), _NEG), nrel


def _dilated_attention(qk, v, *, blk, q_off, k_off, v_off, hg=_HEADS_PER_STEP):
    b, s, _ = qk.shape
    h = DIL_HEADS
    bias, nrel = _dilated_bias(blk)
    return pl.pallas_call(
        functools.partial(_dilated_body, blk=blk, hg=hg, nrel=nrel),
        grid=(b, h // hg, s // blk),
        in_specs=[pl.BlockSpec((1, blk, hg * LANE), lambda b_, h_, i: (b_, i, q_off // hg + h_)),
                  pl.BlockSpec((1, s, hg * LANE), lambda b_, h_, i: (b_, 0, k_off // hg + h_)),
                  pl.BlockSpec((1, s, hg * LANE), lambda b_, h_, i: (b_, 0, v_off // hg + h_)),
                  pl.BlockSpec((nrel, blk, blk), lambda b_, h_, i: (0, 0, 0))],
        out_specs=pl.BlockSpec((1, blk, hg * LANE), lambda b_, h_, i: (b_, i, h_)),
        out_shape=jax.ShapeDtypeStruct((b, s, h * HEAD_DIM), _CD),
        scratch_shapes=_attn_scratch(blk, hg),
        compiler_params=_params("parallel", "parallel", "arbitrary"),
        name="dilated_attention",
    )(qk, qk, v, bias)


_E_CQ, _E_CKV, _E_KPE, _E_Q, _E_K, _E_V, _E_QI, _E_KI, _E_WI = [
    int(o) for o in np.cumsum([0, MLA_Q_RANK, MLA_KV_RANK, MLA_ROPE, DSA_HEADS * HEAD_DIM, DSA_HEADS * HEAD_DIM,
                               DSA_HEADS * HEAD_DIM, IDX_HEADS * IDX_DIM, IDX_DIM])]
_R64_KIA, _R64_KIB, _R64_KPE, _R64_SLABS = 8, 9, 10, 12


def _take_cols(w, idx):
    idx = np.asarray(idx)
    cols = jnp.take(w, jnp.asarray(np.maximum(idx, 0)), axis=1)
    return jnp.where(jnp.asarray(idx >= 0)[None, :], cols, 0.0).astype(_CD)


def _r64_columns():
    half = IDX_DIM // 2
    a = np.arange(half)
    z = -np.ones(half, np.int64)
    cols = []
    for p in range(IDX_HEADS // 2):
        ha, hb = _E_QI + 2 * p * IDX_DIM, _E_QI + (2 * p + 1) * IDX_DIM
        cols += [ha + a, hb + a, ha + half + a, hb + half + a]
    cols += [_E_KI + a, z, _E_KI + half + a, z]
    cols += [z, _E_KI + a, z, _E_KI + half + a]
    cols += [_E_KPE + a, z, _E_KPE + half + a, z]
    cols += [z, z, z, z]
    return np.concatenate(cols)


def _uq_columns():
    half = MLA_ROPE // 2
    a = np.arange(half)
    z = -np.ones(half, np.int64)
    cols = []
    for h in range(MLA_HEADS):
        o = h * (MLA_NOPE + MLA_ROPE)
        cols += [o + np.arange(MLA_NOPE), o + MLA_NOPE + a, z, o + MLA_NOPE + half + a, z]
    return np.concatenate(cols)


def _ukv_columns():
    per = MLA_NOPE + MLA_V
    kn = [h * per + np.arange(MLA_NOPE) for h in range(MLA_HEADS)]
    vv = [h * per + MLA_NOPE + np.arange(MLA_V) for h in range(MLA_HEADS)]
    return np.concatenate(kn + vv)


def _rope_tables(seq, dim, scales, with_identity=False):
    inv = ROPE_THETA ** (-jnp.arange(0, dim, 2, dtype=F32) / dim)
    ang = jnp.arange(seq, dtype=F32)[:, None] * inv[None, :]
    reps = (LANE // 2) // (dim // 2)
    cos = jnp.tile(jnp.cos(ang), (1, 2 * reps))
    sin = jnp.tile(jnp.sin(ang), (1, reps))
    sin = jnp.concatenate([-sin, sin], axis=1)
    sc = jnp.asarray(scales, F32)[:, None, None]
    cos, sin = cos[None] * sc, sin[None] * sc
    if with_identity:
        cos = jnp.concatenate([cos, jnp.ones((1, seq, LANE), F32)], axis=0)
        sin = jnp.concatenate([sin, jnp.zeros((1, seq, LANE), F32)], axis=0)
    return cos, sin


def _mlp_block(x2, g, w1, w2, *, tm):
    up = _norm_matmul(x2, g, w1.astype(_CD), tm=tm, tn=1024, out_dtype=_CD, epilogue=_ep_relu2, name="mlp_up")
    res = pl.BlockSpec((tm, 1024), lambda i, j, k: (i, j))
    return _matmul(up, w2.astype(_CD), tm=tm, tn=1024, tk=2048, out_dtype=F32,
                   epilogue=_ep_residual, extra=(x2,), extra_specs=(res,), name="mlp_down")


def _out_proj(o, w_out, x2, *, tm):
    res = pl.BlockSpec((tm, 1024), lambda i, j, k: (i, j))
    return _matmul(o, w_out.astype(_CD), tm=tm, tn=1024, tk=w_out.shape[0], out_dtype=F32,
                   epilogue=_ep_residual, extra=(x2,), extra_specs=(res,), name="out_proj")


def _even_mixer(x2, g_mix, w_in, g_q, g_kv, w_uq, w_ukv, w_out, *, batch, seq, blk):
    t, d = x2.shape
    tm = min(2 * blk, seq)
    hm = _rmsnorm(x2, g_mix, _CD)
    mla_scale = _LOG2E * (MLA_NOPE + MLA_ROPE) ** -0.5
    rope64 = _rope_tables(seq, IDX_DIM, (1.0, mla_scale))
    rope128 = _rope_tables(seq, HEAD_DIM, (_LOG2E * HEAD_DIM ** -0.5, 1.0), with_identity=True)
    nh = DSA_HEADS * HEAD_DIM

    r64 = _rope_matmul(hm, _take_cols(w_in, _r64_columns()), rope64, lambda j: 0, seq=seq, tm=tm,
                       tn=_R64_SLABS // 2 * LANE, pattern=(True,) * (_R64_SLABS // 2), name="in_proj_rope64")
    qkv = _rope_matmul(hm, w_in[:, _E_Q:_E_QI].astype(_CD), rope128, lambda j: j, seq=seq, tm=tm,
                       tn=nh, pattern=(True,) * DSA_HEADS, name="in_proj_qkv")
    w_idx = _matmul(hm, _take_cols(w_in, np.concatenate([_E_WI + np.arange(IDX_HEADS),
                                                          -np.ones(LANE - IDX_HEADS, np.int64)])),
                    tm=tm, tn=LANE, tk=d, out_dtype=F32,
                    epilogue=_ep_scale(IDX_HEADS ** -0.5 * IDX_DIM ** -0.5), name="in_proj_widx")
    gspec = lambda n: pl.BlockSpec((1, n), lambda i, j, k: (0, 0))
    c_q = _matmul(hm, w_in[:, _E_CQ:_E_CKV].astype(_CD), tm=tm, tn=MLA_Q_RANK, tk=d, out_dtype=_CD,
                  epilogue=_ep_rmsnorm, extra=(g_q.reshape(1, -1),), extra_specs=(gspec(MLA_Q_RANK),),
                  name="in_proj_cq")
    c_kv = _matmul(hm, w_in[:, _E_CKV:_E_KPE].astype(_CD), tm=tm, tn=MLA_KV_RANK, tk=d, out_dtype=_CD,
                   epilogue=_ep_rmsnorm, extra=(g_kv.reshape(1, -1),), extra_specs=(gspec(MLA_KV_RANK),),
                   name="in_proj_ckv")

    q_mla = _rope_matmul(c_q, _take_cols(w_uq, _uq_columns()), rope64, lambda j: 1, seq=seq, tm=tm, tn=1024,
                         pattern=(False, True) * 4, plain_scale=mla_scale, name="mla_q_up")
    kv_mla = _matmul(c_kv, _take_cols(w_ukv, _ukv_columns()), tm=tm, tn=1024, tk=MLA_KV_RANK, out_dtype=_CD,
                     epilogue=_ep_scale(1.0), name="mla_kv_up")

    sh = lambda z: z.reshape(batch, seq, z.shape[-1])
    a = _mla_attention(sh(q_mla), sh(kv_mla), sh(r64), blk=blk)
    bsa = _dsa_attention(sh(r64), sh(w_idx), sh(qkv), blk=blk)
    o = jnp.concatenate([a, bsa], axis=-1).reshape(t, -1)
    return _out_proj(o, w_out, x2, tm=tm)


def _odd_mixer(x2, g_mix, w_in, w_out, *, batch, seq, blk):
    t, d = x2.shape
    tm = min(2 * blk, seq)
    nh = MOBA_HEADS * HEAD_DIM
    rope128 = _rope_tables(seq, HEAD_DIM, (_LOG2E * HEAD_DIM ** -0.5, 1.0), with_identity=True)
    qkv = _rope_matmul(x2, w_in.astype(_CD), rope128, lambda j: j % 3, seq=seq, tm=tm, tn=nh,
                       pattern=(True,) * MOBA_HEADS, norm_gain=g_mix, name="in_proj_odd")
    qkv = qkv.reshape(batch, seq, qkv.shape[-1])
    hs = MOBA_HEADS
    c = _moba_attention(qkv, qkv, blk=blk, q_off=0, k_off=hs, v_off=2 * hs)
    dl = _dilated_attention(qkv, qkv, blk=blk, q_off=3 * hs, k_off=4 * hs, v_off=5 * hs)
    o = jnp.concatenate([c, dl], axis=-1).reshape(t, -1)
    return _out_proj(o, w_out, x2, tm=tm)


def kernel(x, ln_mix, ln_mlp, ln_final, e_w_in, e_g_q, e_g_kv, e_w_uq, e_w_ukv, e_w_out,
           o_w_in, o_w_out, mlp_w1, mlp_w2):
    batch, seq, d = x.shape
    blk = min(512, seq)
    assert seq % blk == 0 and blk % MOBA_BLOCK == 0
    x2 = x.reshape(batch * seq, d)
    depth = ln_mix.shape[0]
    for layer in range(depth):
        j = layer // 2
        if layer % 2 == 0:
            x2 = _even_mixer(x2, ln_mix[layer], e_w_in[j], e_g_q[j], e_g_kv[j], e_w_uq[j], e_w_ukv[j],
                             e_w_out[j], batch=batch, seq=seq, blk=blk)
        else:
            x2 = _odd_mixer(x2, ln_mix[layer], o_w_in[j], o_w_out[j], batch=batch, seq=seq, blk=blk)
        x2 = _mlp_block(x2, ln_mlp[layer], mlp_w1[layer], mlp_w2[layer], tm=min(2 * blk, batch * seq))
    return _rmsnorm(x2, ln_final, x.dtype).reshape(batch, seq, d)
```

```python
import functools

import numpy as np
import jax
import jax.numpy as jnp
from jax import lax
from jax.experimental import pallas as pl
from jax.experimental.pallas import tpu as pltpu

HEAD_DIM = 128
ROPE_THETA = 10000.0
NORM_EPS = 1e-6
MLA_HEADS, MLA_Q_RANK, MLA_KV_RANK, MLA_NOPE, MLA_ROPE, MLA_V = 8, 512, 256, 128, 64, 128
DSA_HEADS, IDX_HEADS, IDX_DIM, DSA_TOPK_MAX = 8, 16, 64, 256
MOBA_HEADS, MOBA_BLOCK, MOBA_TOPK = 8, 256, 3
DIL_HEADS = 8
DIL_PATTERNS = ((128, 1), (512, 4), (2048, 16))

LANE = 128
SUBLANE = 8
VMEM_LIMIT_BYTES = 56 * 2**20

F32 = jnp.float32
_CD = jnp.bfloat16
_NEG = -1e30
_LOG2E = 1.4426950408889634
_INT_MIN = -(2**31)
_INT_MAX = 2**31 - 1
_KEY_NEG_INF = int(np.int32(np.uint32(0xFF800000) ^ np.uint32(0x7FFFFFFF)))
_HEADS_PER_STEP = 2
_INTERP_PROBES = 40


def _params(*sem):
    return pltpu.CompilerParams(dimension_semantics=sem, vmem_limit_bytes=VMEM_LIMIT_BYTES)


def _rmsnorm_body(x_ref, g_ref, o_ref):
    x = x_ref[...].astype(F32)
    y = x * lax.rsqrt(jnp.mean(x * x, axis=-1, keepdims=True) + NORM_EPS)
    o_ref[...] = (y * g_ref[...]).astype(o_ref.dtype)


def _rmsnorm(x, g, out_dtype, tm=512):
    m, d = x.shape
    return pl.pallas_call(
        _rmsnorm_body,
        grid=(m // tm,),
        in_specs=[pl.BlockSpec((tm, d), lambda i: (i, 0)), pl.BlockSpec((1, d), lambda i: (0, 0))],
        out_specs=pl.BlockSpec((tm, d), lambda i: (i, 0)),
        out_shape=jax.ShapeDtypeStruct((m, d), out_dtype),
        compiler_params=_params("parallel"),
        name="rmsnorm",
    )(x, g.reshape(1, d).astype(F32))


def _rope_slab(y, cos, sin):
    return y * cos + pltpu.roll(y, LANE // 2, 1) * sin


def _matmul_body(*refs, n_extra, epilogue, out_dtype):
    a_ref, w_ref = refs[0], refs[1]
    extra = refs[2:2 + n_extra]
    o_ref, acc_ref = refs[2 + n_extra], refs[3 + n_extra]
    k = pl.program_id(2)

    @pl.when(k == 0)
    def _init():
        acc_ref[...] = jnp.zeros_like(acc_ref)

    acc_ref[...] += jnp.dot(a_ref[...], w_ref[...].astype(a_ref.dtype), preferred_element_type=F32)

    @pl.when(k == pl.num_programs(2) - 1)
    def _finish():
        o_ref[...] = epilogue(acc_ref[...], *extra).astype(out_dtype)


def _matmul(a, w, *, tm, tn, tk, out_dtype, epilogue, extra=(), extra_specs=(), name):
    m, kd = a.shape
    n = w.shape[1]
    assert m % tm == 0 and n % tn == 0 and kd % tk == 0, (a.shape, w.shape, tm, tn, tk)
    body = functools.partial(_matmul_body, n_extra=len(extra), epilogue=epilogue, out_dtype=out_dtype)
    return pl.pallas_call(
        body,
        grid=(m // tm, n // tn, kd // tk),
        in_specs=[pl.BlockSpec((tm, tk), lambda i, j, k: (i, k)),
                  pl.BlockSpec((tk, tn), lambda i, j, k: (k, j))] + list(extra_specs),
        out_specs=pl.BlockSpec((tm, tn), lambda i, j, k: (i, j)),
        out_shape=jax.ShapeDtypeStruct((m, n), out_dtype),
        scratch_shapes=[pltpu.VMEM((tm, tn), F32)],
        compiler_params=_params("parallel", "parallel", "arbitrary"),
        name=name,
    )(a, w, *extra)


def _norm_matmul_body(*refs, n_extra, epilogue, out_dtype):
    x_ref, g_ref, w_ref = refs[:3]
    extra = refs[3:3 + n_extra]
    o_ref, h_scr = refs[3 + n_extra], refs[4 + n_extra]

    @pl.when(pl.program_id(1) == 0)
    def _norm():
        x = x_ref[...]
        y = x * lax.rsqrt(jnp.mean(x * x, axis=-1, keepdims=True) + NORM_EPS)
        h_scr[...] = (y * g_ref[...]).astype(h_scr.dtype)

    y = jnp.dot(h_scr[...], w_ref[...].astype(h_scr.dtype), preferred_element_type=F32)
    o_ref[...] = epilogue(y, *extra).astype(out_dtype)


def _norm_matmul(x, g, w, *, tm, tn, out_dtype, epilogue, extra=(), extra_specs=(), name):
    m, d = x.shape
    n = w.shape[1]
    assert m % tm == 0 and n % tn == 0, (x.shape, w.shape, tm, tn)
    body = functools.partial(_norm_matmul_body, n_extra=len(extra), epilogue=epilogue, out_dtype=out_dtype)
    return pl.pallas_call(
        body,
        grid=(m // tm, n // tn),
        in_specs=[pl.BlockSpec((tm, d), lambda i, j: (i, 0)),
                  pl.BlockSpec((1, d), lambda i, j: (0, 0)),
                  pl.BlockSpec((d, tn), lambda i, j: (0, j))] + list(extra_specs),
        out_specs=pl.BlockSpec((tm, tn), lambda i, j: (i, j)),
        out_shape=jax.ShapeDtypeStruct((m, n), out_dtype),
        scratch_shapes=[pltpu.VMEM((tm, d), _CD)],
        compiler_params=_params("parallel", "arbitrary"),
        name=name,
    )(x, g.reshape(1, d).astype(F32), w, *extra)


def _ep_scale(scale):
    def ep(y):
        return y if scale == 1.0 else y * scale
    return ep


def _ep_relu2(y):
    return jnp.square(jnp.maximum(y, 0.0))


def _ep_residual(y, r_ref):
    return y + r_ref[...]


def _ep_rmsnorm(y, g_ref):
    return y * lax.rsqrt(jnp.mean(y * y, axis=-1, keepdims=True) + NORM_EPS) * g_ref[...]


def _ep_rope(pattern, plain_scale):
    def ep(y, cos_ref, sin_ref):
        cos, sin = cos_ref[0], sin_ref[0]
        out = []
        for c, rot in enumerate(pattern):
            slab = y[:, c * LANE:(c + 1) * LANE]
            out.append(_rope_slab(slab, cos, sin) if rot else slab * plain_scale)
        return jnp.concatenate(out, axis=1)
    return ep


def _rope_matmul(a, w, tabs, tab_of_tile, *, seq, tm, tn, pattern, plain_scale=1.0, norm_gain=None, name):
    nblk = seq // tm
    spec = pl.BlockSpec((1, tm, LANE), lambda i, j, *_: (tab_of_tile(j), i % nblk, 0))
    common = dict(tm=tm, tn=tn, out_dtype=_CD, epilogue=_ep_rope(pattern, plain_scale), extra=tabs,
                  extra_specs=(spec, spec), name=name)
    if norm_gain is None:
        return _matmul(a, w, tk=a.shape[1], **common)
    return _norm_matmul(a, norm_gain, w, **common)


def _attn_init(m_scr, l_scr, acc_scr):
    m_scr[...] = jnp.full(m_scr.shape, _NEG, F32)
    l_scr[...] = jnp.zeros(l_scr.shape, F32)
    acc_scr[...] = jnp.zeros(acc_scr.shape, F32)


def _tile_rows(j, blk):
    return pl.ds(pl.multiple_of(j * blk, blk), blk)


def _dot_tn(a, b):
    return lax.dot_general(a, b, (((0,), (0,)), ((), ())), preferred_element_type=F32)


def _key_major(q):
    return q.astype(F32).T.astype(q.dtype)


def _attn_scratch(blk, hg):
    per_head = [pltpu.VMEM((1, blk), F32), pltpu.VMEM((1, blk), F32), pltpu.VMEM((HEAD_DIM, blk), F32)]
    return [pltpu.VMEM((2, hg, blk, blk), F32)] + per_head * hg


def _head_state(scr, hg):
    return scr[0], [scr[1 + 3 * g:4 + 3 * g] for g in range(hg)]


def _attn_step(st, v, m_scr, l_scr, acc_scr):
    m_prev = m_scr[...]
    m_new = jnp.maximum(m_prev, jnp.max(st, axis=0, keepdims=True))
    alpha = jnp.exp2(m_prev - m_new)
    p = jnp.exp2(st - m_new)
    l_scr[...] = alpha * l_scr[...] + jnp.sum(p, axis=0, keepdims=True)
    acc_scr[...] = alpha * acc_scr[...] + _dot_tn(v, p.astype(v.dtype))
    m_scr[...] = m_new


def _attn_out(l_scr, acc_scr, dtype):
    return (acc_scr[...] / l_scr[...]).T.astype(dtype)


def _pipelined_key_tiles(last, scores, consume, first=0):
    n_past = last - first

    def pair(t, carry):
        j = first + 2 * t
        scores(j + 1, 1)
        consume(j, 0, False)
        scores(j + 2, 0)
        consume(j + 1, 1, False)
        return carry

    scores(first, 0)
    lax.fori_loop(0, n_past // 2, pair, 0)

    @pl.when(n_past % 2 == 1)
    def _odd():
        scores(last, 1)
        consume(last - 1, 0, False)
        consume(last, 1, True)

    @pl.when(n_past % 2 == 0)
    def _even():
        consume(last, 0, True)


def _causal(s):
    kpos = lax.broadcasted_iota(jnp.int32, s.shape, 0)
    qpos = lax.broadcasted_iota(jnp.int32, s.shape, 1)
    return jnp.where(kpos <= qpos, s, _NEG)


def _mla_body(q_ref, kn_ref, kp_ref, v_ref, o_ref, *scr, blk, hg):
    i = pl.program_id(2)
    s_scr, heads = _head_state(scr, hg)
    for st in heads:
        _attn_init(*st)
    qts = [_key_major(q_ref[0, :, g * 2 * LANE:(g + 1) * 2 * LANE]) for g in range(hg)]

    def scores(j, slot):
        rows = _tile_rows(j, blk)
        kp = kp_ref[0, rows, :]
        for g in range(hg):
            k = jnp.concatenate([kn_ref[0, rows, g * LANE:(g + 1) * LANE], kp], axis=1)
            s_scr[slot, g] = jnp.dot(k, qts[g], preferred_element_type=F32)

    def consume(j, slot, masked):
        rows = _tile_rows(j, blk)
        for g, st in enumerate(heads):
            s = s_scr[slot, g]
            _attn_step(_causal(s) if masked else s, v_ref[0, rows, g * LANE:(g + 1) * LANE], *st)

    _pipelined_key_tiles(i, scores, consume)
    for g, (_, l_scr, acc_scr) in enumerate(heads):
        o_ref[0, :, g * LANE:(g + 1) * LANE] = _attn_out(l_scr, acc_scr, o_ref.dtype)


def _mla_attention(q, kv, r64, *, blk, hg=_HEADS_PER_STEP):
    b, s, _ = q.shape
    h = MLA_HEADS
    return pl.pallas_call(
        functools.partial(_mla_body, blk=blk, hg=hg),
        grid=(b, h // hg, s // blk),
        in_specs=[pl.BlockSpec((1, blk, hg * 2 * LANE), lambda b_, h_, i: (b_, i, h_)),
                  pl.BlockSpec((1, s, hg * LANE), lambda b_, h_, i: (b_, 0, h_)),
                  pl.BlockSpec((1, s, LANE), lambda b_, h_, i: (b_, 0, _R64_KPE)),
                  pl.BlockSpec((1, s, hg * LANE), lambda b_, h_, i: (b_, 0, h // hg + h_))],
        out_specs=pl.BlockSpec((1, blk, hg * LANE), lambda b_, h_, i: (b_, i, h_)),
        out_shape=jax.ShapeDtypeStruct((b, s, h * MLA_V), _CD),
        scratch_shapes=_attn_scratch(blk, hg),
        compiler_params=_params("parallel", "parallel", "arbitrary"),
        name="mla_attention",
    )(q, kv, r64, kv)


def _order_key(x):
    bits = lax.bitcast_convert_type(x, jnp.int32)
    return bits ^ ((bits >> 31) & _INT_MAX)


def _key_score(key):
    return lax.bitcast_convert_type(key ^ ((key >> 31) & _INT_MAX), F32)


def _dsa_select(i, qi_ref, ka_ref, kb_ref, w_ref, sel_scr, jc_scr, *, blk, topk, seq):
    wt = w_ref[0].T
    qits = [_key_major(qi_ref[0, :, p * LANE:(p + 1) * LANE]) for p in range(IDX_HEADS // 2)]
    kloc = lax.broadcasted_iota(jnp.int32, (blk, blk), 0)
    qpos = i * blk + lax.broadcasted_iota(jnp.int32, (blk, blk), 1)

    def score_tile(c, carry):
        rows = _tile_rows(c, blk)
        ka, kb = ka_ref[0, rows, :], kb_ref[0, rows, :]
        acc = jnp.zeros((blk, blk), F32)
        for p in range(IDX_HEADS // 2):
            da = jnp.dot(ka, qits[p], preferred_element_type=F32)
            db = jnp.dot(kb, qits[p], preferred_element_type=F32)
            acc = acc + wt[2 * p:2 * p + 1, :] * jnp.maximum(da, 0.0) + wt[2 * p + 1:2 * p + 2, :] * jnp.maximum(db, 0.0)
        key = _order_key(acc)
        causal = c * blk + kloc <= qpos
        sel_scr[c] = jnp.where(causal, key, _KEY_NEG_INF)
        lo8, hi8 = carry
        fold = lambda x: x.reshape(blk // SUBLANE, SUBLANE, blk)
        lo8 = jnp.minimum(lo8, jnp.min(fold(jnp.where(causal, key, _INT_MAX)), axis=0))
        hi8 = jnp.maximum(hi8, jnp.max(fold(jnp.where(causal, key, _INT_MIN)), axis=0))
        return lo8, hi8

    lo8, hi8 = lax.fori_loop(0, i + 1, score_tile, (jnp.full((SUBLANE, blk), _INT_MAX, jnp.int32),
                                                    jnp.full((SUBLANE, blk), _INT_MIN, jnp.int32)))
    key_min, key_max = lo8[0:1], hi8[0:1]
    for r in range(1, SUBLANE):
        key_min = jnp.minimum(key_min, lo8[r:r + 1])
        key_max = jnp.maximum(key_max, hi8[r:r + 1])

    n_causal = i * blk + lax.broadcasted_iota(jnp.int32, (1, blk), 1) + 1
    kk = jnp.minimum(topk, n_causal).astype(F32)

    def count(pred):
        def body(c, cnt):
            hit = jnp.where(pred(sel_scr[c], c * blk + kloc), 1.0, 0.0)
            return cnt + jnp.sum(hit.reshape(blk // SUBLANE, SUBLANE, blk), axis=0)
        cnt = lax.fori_loop(0, i + 1, body, jnp.zeros((SUBLANE, blk), F32))
        return jnp.sum(cnt, axis=0, keepdims=True)

    def unfinished(lo, hi, n_lo):
        return (n_lo > kk) & (hi > lo + 1)

    def probe(state):
        lo, hi, n_lo, n_hi, w_lo, w_hi, last, it, _ = state
        lo_f, hi_f = _key_score(lo), _key_score(hi)
        a, b = (n_lo - kk + 0.5) * w_lo, (kk - 0.5 - n_hi) * w_hi
        interp = _order_key(lo_f + (hi_f - lo_f) * (a / (a + b)))
        mid = (lo >> 1) + (hi >> 1) + (lo & hi & 1)
        cand = jnp.clip(jnp.where(it < _INTERP_PROBES, interp, mid), lo + 1, hi - 1)
        n = count(lambda keys, kpos: keys >= cand)
        live = unfinished(lo, hi, n_lo)
        up = live & (n >= kk)
        down = live & (n < kk)
        lo, n_lo = jnp.where(up, cand, lo), jnp.where(up, n, n_lo)
        hi, n_hi = jnp.where(down, cand, hi), jnp.where(down, n, n_hi)
        w_hi = jnp.where(up, jnp.where(last > 0.5, 0.5 * w_hi, 1.0), jnp.where(down, 1.0, w_hi))
        w_lo = jnp.where(down, jnp.where(last < -0.5, 0.5 * w_lo, 1.0), jnp.where(up, 1.0, w_lo))
        last = jnp.where(up, 1.0, jnp.where(down, -1.0, last))
        more = jnp.max(jnp.where(unfinished(lo, hi, n_lo), 1.0, 0.0))
        return lo, hi, n_lo, n_hi, w_lo, w_hi, last, it + 1, more

    n_all = n_causal.astype(F32)
    one, zero = jnp.ones((1, blk), F32), jnp.zeros((1, blk), F32)
    start = (key_min, key_max + 1, n_all, zero, one, one, zero, jnp.int32(0),
             jnp.max(jnp.where(unfinished(key_min, key_max + 1, n_all), 1.0, 0.0)))
    t, _, n_ge = lax.while_loop(lambda s: s[8] > 0.5, probe, start)[:3]
    jc_scr[...] = jnp.full((1, blk), seq, jnp.int32)

    @pl.when(jnp.max(n_ge - kk) > 0.5)
    def _ties():
        need = kk - count(lambda keys, kpos: keys > t)
        nbits = (seq - 1).bit_length()

        def pos_step(b, jc):
            cand = jc + (jnp.int32(1) << (nbits - 1 - b))
            n_lt = count(lambda keys, kpos: (keys == t) & (kpos < cand))
            return jnp.where(n_lt < need, cand, jc)

        jc_scr[...] = lax.fori_loop(0, nbits, pos_step, jnp.zeros((1, blk), jnp.int32))

    jc = jc_scr[...]

    def to_bias(c, carry):
        keys = sel_scr[c]
        chosen = (keys > t) | ((keys == t) & (c * blk + kloc <= jc))
        sel_scr[c] = lax.bitcast_convert_type(jnp.where(chosen, 0.0, _NEG).astype(F32), jnp.int32)
        return carry

    lax.fori_loop(0, i + 1, to_bias, 0)


def _dsa_body(qi_ref, ka_ref, kb_ref, w_ref, q_ref, k_ref, v_ref, o_ref, sel_scr, jc_scr, *scr,
              blk, hg, topk, seq):
    i = pl.program_id(1)

    @pl.when(pl.program_id(2) == 0)
    def _select():
        _dsa_select(i, qi_ref, ka_ref, kb_ref, w_ref, sel_scr, jc_scr, blk=blk, topk=topk, seq=seq)

    s_scr, heads = _head_state(scr, hg)
    for st in heads:
        _attn_init(*st)
    qts = [_key_major(q_ref[0, :, g * LANE:(g + 1) * LANE]) for g in range(hg)]

    def scores(j, slot):
        rows = _tile_rows(j, blk)
        for g in range(hg):
            s_scr[slot, g] = jnp.dot(k_ref[0, rows, g * LANE:(g + 1) * LANE], qts[g], preferred_element_type=F32)

    def consume(j, slot, masked):
        rows = _tile_rows(j, blk)
        bias = lax.bitcast_convert_type(sel_scr[j], F32)
        for g, st in enumerate(heads):
            _attn_step(s_scr[slot, g] + bias, v_ref[0, rows, g * LANE:(g + 1) * LANE], *st)

    _pipelined_key_tiles(i, scores, consume)
    for g, (_, l_scr, acc_scr) in enumerate(heads):
        o_ref[0, :, g * LANE:(g + 1) * LANE] = _attn_out(l_scr, acc_scr, o_ref.dtype)


def _dsa_attention(r64, w_idx, qkv, *, blk, hg=_HEADS_PER_STEP):
    b, s, _ = qkv.shape
    h = DSA_HEADS
    topk = min(DSA_TOPK_MAX, s // 4)
    return pl.pallas_call(
        functools.partial(_dsa_body, blk=blk, hg=hg, topk=topk, seq=s),
        grid=(b, s // blk, h // hg),
        in_specs=[pl.BlockSpec((1, blk, 8 * LANE), lambda b_, i, h_: (b_, i, 0)),
                  pl.BlockSpec((1, s, LANE), lambda b_, i, h_: (b_, 0, _R64_KIA)),
                  pl.BlockSpec((1, s, LANE), lambda b_, i, h_: (b_, 0, _R64_KIB)),
                  pl.BlockSpec((1, blk, LANE), lambda b_, i, h_: (b_, i, 0)),
                  pl.BlockSpec((1, blk, hg * LANE), lambda b_, i, h_: (b_, i, h_)),
                  pl.BlockSpec((1, s, hg * LANE), lambda b_, i, h_: (b_, 0, h // hg + h_)),
                  pl.BlockSpec((1, s, hg * LANE), lambda b_, i, h_: (b_, 0, 2 * (h // hg) + h_))],
        out_specs=pl.BlockSpec((1, blk, hg * LANE), lambda b_, i, h_: (b_, i, h_)),
        out_shape=jax.ShapeDtypeStruct((b, s, h * HEAD_DIM), _CD),
        scratch_shapes=[pltpu.VMEM((s // blk, blk, blk), jnp.int32),
                        pltpu.VMEM((1, blk), jnp.int32)] + _attn_scratch(blk, hg),
        compiler_params=_params("parallel", "arbitrary", "arbitrary"),
        name="dsa_attention",
    )(r64, r64, r64, w_idx, qkv, qkv, qkv)


def _moba_body(q_ref, k_ref, v_ref, o_ref, kmean_scr, pick_scr, *scr, blk, hg, seq, nbp):
    i = pl.program_id(2)
    nper = blk // MOBA_BLOCK
    shift = MOBA_BLOCK.bit_length() - 1
    s_scr, heads = _head_state(scr, hg)

    @pl.when(i == 0)
    def _block_means():
        r = lax.broadcasted_iota(jnp.int32, (nbp, seq), 0)
        c = lax.broadcasted_iota(jnp.int32, (nbp, seq), 1)
        avg = jnp.where((c >> shift) == r, 1.0 / MOBA_BLOCK, 0.0).astype(k_ref.dtype)
        for g in range(hg):
            kmean_scr[g] = jnp.dot(avg, k_ref[0, :, g * LANE:(g + 1) * LANE],
                                   preferred_element_type=F32).astype(kmean_scr.dtype)

    qts = [_key_major(q_ref[0, :, g * LANE:(g + 1) * LANE]) for g in range(hg)]
    kblk = lax.broadcasted_iota(jnp.int32, (nbp, blk), 0)
    own = (i * blk + lax.broadcasted_iota(jnp.int32, (nbp, blk), 1)) >> shift
    kblk_f = kblk.astype(F32)
    for g, st in enumerate(heads):
        _attn_init(*st)
        gate = jnp.where(kblk < own, jnp.dot(kmean_scr[g], qts[g], preferred_element_type=F32), _NEG)
        pick = jnp.zeros((nbp, blk), F32)
        for _ in range(MOBA_TOPK):
            best = jnp.max(gate, axis=0, keepdims=True)
            hit = (gate == best) & (best > 0.5 * _NEG)
            first = jnp.min(jnp.where(hit, kblk_f, float(nbp)), axis=0, keepdims=True)
            new = kblk_f == first
            pick = jnp.where(new, 1.0, pick)
            gate = jnp.where(new, _NEG, gate)
        pick_scr[g] = pick

    def scores(j, slot):
        rows = _tile_rows(j, blk)
        for g in range(hg):
            s_scr[slot, g] = jnp.dot(k_ref[0, rows, g * LANE:(g + 1) * LANE], qts[g], preferred_element_type=F32)

    kloc = lax.broadcasted_iota(jnp.int32, (MOBA_BLOCK, blk), 0)
    qloc = lax.broadcasted_iota(jnp.int32, (MOBA_BLOCK, blk), 1)

    def consume(j, slot, masked):
        rows = _tile_rows(j, blk)
        for g, st in enumerate(heads):
            parts = []
            for u in range(nper):
                s = s_scr[slot, g, u * MOBA_BLOCK:(u + 1) * MOBA_BLOCK, :]
                ok = pick_scr[g, pl.ds(j * nper + u, 1), :] > 0.5
                if masked:
                    kpos = u * MOBA_BLOCK + kloc
                    same = (qloc >> shift) == u
                    ok = (same & (kpos <= qloc)) | (jnp.logical_not(same) & ok & ((qloc >> shift) > u))
                parts.append(jnp.where(ok, s, _NEG))
            _attn_step(jnp.concatenate(parts, axis=0), v_ref[0, rows, g * LANE:(g + 1) * LANE], *st)

    _pipelined_key_tiles(i, scores, consume)
    for g, (_, l_scr, acc_scr) in enumerate(heads):
        o_ref[0, :, g * LANE:(g + 1) * LANE] = _attn_out(l_scr, acc_scr, o_ref.dtype)


def _moba_attention(qk, v, *, blk, q_off, k_off, v_off, hg=_HEADS_PER_STEP):
    b, s, _ = qk.shape
    h = MOBA_HEADS
    assert s % MOBA_BLOCK == 0 and blk % MOBA_BLOCK == 0
    nbp = -(-(s // MOBA_BLOCK) // SUBLANE) * SUBLANE
    return pl.pallas_call(
        functools.partial(_moba_body, blk=blk, hg=hg, seq=s, nbp=nbp),
        grid=(b, h // hg, s // blk),
        in_specs=[pl.BlockSpec((1, blk, hg * LANE), lambda b_, h_, i: (b_, i, q_off // hg + h_)),
                  pl.BlockSpec((1, s, hg * LANE), lambda b_, h_, i: (b_, 0, k_off // hg + h_)),
                  pl.BlockSpec((1, s, hg * LANE), lambda b_, h_, i: (b_, 0, v_off // hg + h_))],
        out_specs=pl.BlockSpec((1, blk, hg * LANE), lambda b_, h_, i: (b_, i, h_)),
        out_shape=jax.ShapeDtypeStruct((b, s, h * HEAD_DIM), _CD),
        scratch_shapes=[pltpu.VMEM((hg, nbp, HEAD_DIM), _CD),
                        pltpu.VMEM((hg, nbp, blk), F32)] + _attn_scratch(blk, hg),
        compiler_params=_params("parallel", "parallel", "arbitrary"),
        name="moba_attention",
    )(qk, qk, v)


def _dilated_body(q_ref, k_ref, v_ref, bias_ref, o_ref, *scr, blk, hg, nrel):
    i = pl.program_id(2)
    s_scr, heads = _head_state(scr, hg)
    for st in heads:
        _attn_init(*st)
    qts = [_key_major(q_ref[0, :, g * LANE:(g + 1) * LANE]) for g in range(hg)]

    def scores(j, slot):
        rows = _tile_rows(j, blk)
        for g in range(hg):
            s_scr[slot, g] = jnp.dot(k_ref[0, rows, g * LANE:(g + 1) * LANE], qts[g], preferred_element_type=F32)

    def consume(j, slot, masked):
        rows = _tile_rows(j, blk)
        bias = bias_ref[i - j]
        for g, st in enumerate(heads):
            _attn_step(s_scr[slot, g] + bias, v_ref[0, rows, g * LANE:(g + 1) * LANE], *st)

    _pipelined_key_tiles(i, scores, consume, first=jnp.maximum(i - (nrel - 1), 0))
    for g, (_, l_scr, acc_scr) in enumerate(heads):
        o_ref[0, :, g * LANE:(g + 1) * LANE] = _attn_out(l_scr, acc_scr, o_ref.dtype)


def _dilated_bias(blk):
    reach = max(w for w, _ in DIL_PATTERNS)
    nrel = -(-reach // blk) + 1
    rel = jnp.arange(nrel, dtype=jnp.int32)[:, None, None]
    k = jnp.arange(blk, dtype=jnp.int32)[None, :, None]
    q = jnp.arange(blk, dtype=jnp.int32)[None, None, :]
    d = rel * blk + q - k
    mult = jnp.zeros(d.shape, F32)
    for window, dil in DIL_PATTERNS:
        mult = mult + ((d >= 0) & (d <= (window // dil) * dil) & (d % dil == 0)).astype(F32)
    return jnp.where(mult > 0, jnp.log2(jnp.maximum(mult, 1.0)), _NEG), nrel


def _dilated_attention(qk, v, *, blk, q_off, k_off, v_off, hg=_HEADS_PER_STEP):
    b, s, _ = qk.shape
    h = DIL_HEADS
    bias, nrel = _dilated_bias(blk)
    return pl.pallas_call(
        functools.partial(_dilated_body, blk=blk, hg=hg, nrel=nrel),
        grid=(b, h // hg, s // blk),
        in_specs=[pl.BlockSpec((1, blk, hg * LANE), lambda b_, h_, i: (b_, i, q_off // hg + h_)),
                  pl.BlockSpec((1, s, hg * LANE), lambda b_, h_, i: (b_, 0, k_off // hg + h_)),
                  pl.BlockSpec((1, s, hg * LANE), lambda b_, h_, i: (b_, 0, v_off // hg + h_)),
                  pl.BlockSpec((nrel, blk, blk), lambda b_, h_, i: (0, 0, 0))],
        out_specs=pl.BlockSpec((1, blk, hg * LANE), lambda b_, h_, i: (b_, i, h_)),
        out_shape=jax.ShapeDtypeStruct((b, s, h * HEAD_DIM), _CD),
        scratch_shapes=_attn_scratch(blk, hg),
        compiler_params=_params("parallel", "parallel", "arbitrary"),
        name="dilated_attention",
    )(qk, qk, v, bias)


_E_CQ, _E_CKV, _E_KPE, _E_Q, _E_K, _E_V, _E_QI, _E_KI, _E_WI = [
    int(o) for o in np.cumsum([0, MLA_Q_RANK, MLA_KV_RANK, MLA_ROPE, DSA_HEADS * HEAD_DIM, DSA_HEADS * HEAD_DIM,
                               DSA_HEADS * HEAD_DIM, IDX_HEADS * IDX_DIM, IDX_DIM])]
_R64_KIA, _R64_KIB, _R64_KPE, _R64_SLABS = 8, 9, 10, 12


def _take_cols(w, idx):
    idx = np.asarray(idx)
    cols = jnp.take(w, jnp.asarray(np.maximum(idx, 0)), axis=1)
    return jnp.where(jnp.asarray(idx >= 0)[None, :], cols, 0.0).astype(_CD)


def _r64_columns():
    half = IDX_DIM // 2
    a = np.arange(half)
    z = -np.ones(half, np.int64)
    cols = []
    for p in range(IDX_HEADS // 2):
        ha, hb = _E_QI + 2 * p * IDX_DIM, _E_QI + (2 * p + 1) * IDX_DIM
        cols += [ha + a, hb + a, ha + half + a, hb + half + a]
    cols += [_E_KI + a, z, _E_KI + half + a, z]
    cols += [z, _E_KI + a, z, _E_KI + half + a]
    cols += [_E_KPE + a, z, _E_KPE + half + a, z]
    cols += [z, z, z, z]
    return np.concatenate(cols)


def _uq_columns():
    half = MLA_ROPE // 2
    a = np.arange(half)
    z = -np.ones(half, np.int64)
    cols = []
    for h in range(MLA_HEADS):
        o = h * (MLA_NOPE + MLA_ROPE)
        cols += [o + np.arange(MLA_NOPE), o + MLA_NOPE + a, z, o + MLA_NOPE + half + a, z]
    return np.concatenate(cols)


def _ukv_columns():
    per = MLA_NOPE + MLA_V
    kn = [h * per + np.arange(MLA_NOPE) for h in range(MLA_HEADS)]
    vv = [h * per + MLA_NOPE + np.arange(MLA_V) for h in range(MLA_HEADS)]
    return np.concatenate(kn + vv)


def _rope_tables(seq, dim, scales, with_identity=False):
    inv = ROPE_THETA ** (-jnp.arange(0, dim, 2, dtype=F32) / dim)
    ang = jnp.arange(seq, dtype=F32)[:, None] * inv[None, :]
    reps = (LANE // 2) // (dim // 2)
    cos = jnp.tile(jnp.cos(ang), (1, 2 * reps))
    sin = jnp.tile(jnp.sin(ang), (1, reps))
    sin = jnp.concatenate([-sin, sin], axis=1)
    sc = jnp.asarray(scales, F32)[:, None, None]
    cos, sin = cos[None] * sc, sin[None] * sc
    if with_identity:
        cos = jnp.concatenate([cos, jnp.ones((1, seq, LANE), F32)], axis=0)
        sin = jnp.concatenate([sin, jnp.zeros((1, seq, LANE), F32)], axis=0)
    return cos, sin


def _mlp_block(x2, g, w1, w2, *, tm):
    up = _norm_matmul(x2, g, w1, tm=tm, tn=1024, out_dtype=_CD, epilogue=_ep_relu2, name="mlp_up")
    res = pl.BlockSpec((tm, 1024), lambda i, j, k: (i, j))
    return _matmul(up, w2, tm=tm, tn=1024, tk=2048, out_dtype=F32,
                   epilogue=_ep_residual, extra=(x2,), extra_specs=(res,), name="mlp_down")


def _out_proj_body(a_ref, b_ref, w_ref, r_ref, o_ref):
    half = a_ref.shape[1]
    y = jnp.dot(a_ref[...], w_ref[:half, :].astype(a_ref.dtype), preferred_element_type=F32)
    y = y + jnp.dot(b_ref[...], w_ref[half:, :].astype(b_ref.dtype), preferred_element_type=F32)
    o_ref[...] = y + r_ref[...]


def _out_proj(a, b, w_out, x2, *, tm, tn=1024):
    t, half = a.shape
    d = w_out.shape[1]
    return pl.pallas_call(
        _out_proj_body,
        grid=(t // tm, d // tn),
        in_specs=[pl.BlockSpec((tm, half), lambda i, j: (i, 0)),
                  pl.BlockSpec((tm, half), lambda i, j: (i, 0)),
                  pl.BlockSpec((2 * half, tn), lambda i, j: (0, j)),
                  pl.BlockSpec((tm, tn), lambda i, j: (i, j))],
        out_specs=pl.BlockSpec((tm, tn), lambda i, j: (i, j)),
        out_shape=jax.ShapeDtypeStruct((t, d), F32),
        compiler_params=_params("parallel", "parallel"),
        name="out_proj",
    )(a, b, w_out, x2)


def _even_mixer(x2, g_mix, w_in, g_q, g_kv, w_uq, w_ukv, w_out, *, batch, seq, blk):
    t, d = x2.shape
    tm = min(2 * blk, seq)
    hm = _rmsnorm(x2, g_mix, _CD)
    mla_scale = _LOG2E * (MLA_NOPE + MLA_ROPE) ** -0.5
    rope64 = _rope_tables(seq, IDX_DIM, (1.0, mla_scale))
    rope128 = _rope_tables(seq, HEAD_DIM, (_LOG2E * HEAD_DIM ** -0.5, 1.0), with_identity=True)
    nh = DSA_HEADS * HEAD_DIM

    r64 = _rope_matmul(hm, _take_cols(w_in, _r64_columns()), rope64, lambda j: 0, seq=seq, tm=tm,
                       tn=_R64_SLABS // 2 * LANE, pattern=(True,) * (_R64_SLABS // 2), name="in_proj_rope64")
    qkv = _rope_matmul(hm, w_in[:, _E_Q:_E_QI], rope128, lambda j: j, seq=seq, tm=tm,
                       tn=nh, pattern=(True,) * DSA_HEADS, name="in_proj_qkv")
    w_idx = _matmul(hm, _take_cols(w_in, np.concatenate([_E_WI + np.arange(IDX_HEADS),
                                                          -np.ones(LANE - IDX_HEADS, np.int64)])),
                    tm=tm, tn=LANE, tk=d, out_dtype=F32,
                    epilogue=_ep_scale(IDX_HEADS ** -0.5 * IDX_DIM ** -0.5), name="in_proj_widx")
    gspec = lambda n: pl.BlockSpec((1, n), lambda i, j, k: (0, 0))
    c_q = _matmul(hm, w_in[:, _E_CQ:_E_CKV].astype(_CD), tm=tm, tn=MLA_Q_RANK, tk=d, out_dtype=_CD,
                  epilogue=_ep_rmsnorm, extra=(g_q.reshape(1, -1),), extra_specs=(gspec(MLA_Q_RANK),),
                  name="in_proj_cq")
    c_kv = _matmul(hm, w_in[:, _E_CKV:_E_KPE].astype(_CD), tm=tm, tn=MLA_KV_RANK, tk=d, out_dtype=_CD,
                   epilogue=_ep_rmsnorm, extra=(g_kv.reshape(1, -1),), extra_specs=(gspec(MLA_KV_RANK),),
                   name="in_proj_ckv")

    q_mla = _rope_matmul(c_q, _take_cols(w_uq, _uq_columns()), rope64, lambda j: 1, seq=seq, tm=tm, tn=1024,
                         pattern=(False, True) * 4, plain_scale=mla_scale, name="mla_q_up")
    kv_mla = _matmul(c_kv, _take_cols(w_ukv, _ukv_columns()), tm=tm, tn=1024, tk=MLA_KV_RANK, out_dtype=_CD,
                     epilogue=_ep_scale(1.0), name="mla_kv_up")

    sh = lambda z: z.reshape(batch, seq, z.shape[-1])
    a = _mla_attention(sh(q_mla), sh(kv_mla), sh(r64), blk=blk)
    bsa = _dsa_attention(sh(r64), sh(w_idx), sh(qkv), blk=blk)
    return _out_proj(a.reshape(t, -1), bsa.reshape(t, -1), w_out, x2, tm=tm)


def _odd_mixer(x2, g_mix, w_in, w_out, *, batch, seq, blk):
    t, d = x2.shape
    tm = min(2 * blk, seq)
    nh = MOBA_HEADS * HEAD_DIM
    rope128 = _rope_tables(seq, HEAD_DIM, (_LOG2E * HEAD_DIM ** -0.5, 1.0), with_identity=True)
    qkv = _rope_matmul(x2, w_in, rope128, lambda j: j % 3, seq=seq, tm=tm, tn=nh,
                       pattern=(True,) * MOBA_HEADS, norm_gain=g_mix, name="in_proj_odd")
    qkv = qkv.reshape(batch, seq, qkv.shape[-1])
    hs = MOBA_HEADS
    c = _moba_attention(qkv, qkv, blk=blk, q_off=0, k_off=hs, v_off=2 * hs)
    dl = _dilated_attention(qkv, qkv, blk=blk, q_off=3 * hs, k_off=4 * hs, v_off=5 * hs)
    return _out_proj(c.reshape(t, -1), dl.reshape(t, -1), w_out, x2, tm=tm)


def kernel(x, ln_mix, ln_mlp, ln_final, e_w_in, e_g_q, e_g_kv, e_w_uq, e_w_ukv, e_w_out,
           o_w_in, o_w_out, mlp_w1, mlp_w2):
    batch, seq, d = x.shape
    blk = min(512, seq)
    assert seq % blk == 0 and blk % MOBA_BLOCK == 0
    x2 = x.reshape(batch * seq, d)
    depth = ln_mix.shape[0]
    for layer in range(depth):
        j = layer // 2
        if layer % 2 == 0:
            x2 = _even_mixer(x2, ln_mix[layer], e_w_in[j], e_g_q[j], e_g_kv[j], e_w_uq[j], e_w_ukv[j],
                             e_w_out[j], batch=batch, seq=seq, blk=blk)
        else:
            x2 = _odd_mixer(x2, ln_mix[layer], o_w_in[j], o_w_out[j], batch=batch, seq=seq, blk=blk)
        x2 = _mlp_block(x2, ln_mlp[layer], mlp_w1[layer], mlp_w2[layer], tm=min(2 * blk, batch * seq))
    return _rmsnorm(x2, ln_final, x.dtype).reshape(batch, seq, d)
```

```python
import functools

import numpy as np
import jax
import jax.numpy as jnp
from jax import lax
from jax.experimental import pallas as pl
from jax.experimental.pallas import tpu as pltpu

HEAD_DIM = 128
ROPE_THETA = 10000.0
NORM_EPS = 1e-6
MLA_HEADS, MLA_Q_RANK, MLA_KV_RANK, MLA_NOPE, MLA_ROPE, MLA_V = 8, 512, 256, 128, 64, 128
DSA_HEADS, IDX_HEADS, IDX_DIM, DSA_TOPK_MAX = 8, 16, 64, 256
MOBA_HEADS, MOBA_BLOCK, MOBA_TOPK = 8, 256, 3
DIL_HEADS = 8
DIL_PATTERNS = ((128, 1), (512, 4), (2048, 16))

LANE = 128
SUBLANE = 8
VMEM_LIMIT_BYTES = 56 * 2**20

F32 = jnp.float32
_CD = jnp.bfloat16
_NEG = -1e30
_LOG2E = 1.4426950408889634
_INT_MIN = -(2**31)
_INT_MAX = 2**31 - 1
_KEY_NEG_INF = int(np.int32(np.uint32(0xFF800000) ^ np.uint32(0x7FFFFFFF)))
_HEADS_PER_STEP = 2
_INTERP_PROBES = 40


def _params(*sem):
    return pltpu.CompilerParams(dimension_semantics=sem, vmem_limit_bytes=VMEM_LIMIT_BYTES)


def _rmsnorm_body(x_ref, g_ref, o_ref):
    x = x_ref[...].astype(F32)
    y = x * lax.rsqrt(jnp.mean(x * x, axis=-1, keepdims=True) + NORM_EPS)
    o_ref[...] = (y * g_ref[...]).astype(o_ref.dtype)


def _rmsnorm(x, g, out_dtype, tm=512):
    m, d = x.shape
    return pl.pallas_call(
        _rmsnorm_body,
        grid=(m // tm,),
        in_specs=[pl.BlockSpec((tm, d), lambda i: (i, 0)), pl.BlockSpec((1, d), lambda i: (0, 0))],
        out_specs=pl.BlockSpec((tm, d), lambda i: (i, 0)),
        out_shape=jax.ShapeDtypeStruct((m, d), out_dtype),
        compiler_params=_params("parallel"),
        name="rmsnorm",
    )(x, g.reshape(1, d).astype(F32))


def _rope_slab(y, cos, sin):
    return y * cos + pltpu.roll(y, LANE // 2, 1) * sin


def _matmul_body(*refs, n_extra, epilogue, out_dtype):
    a_ref, w_ref = refs[0], refs[1]
    extra = refs[2:2 + n_extra]
    o_ref, acc_ref = refs[2 + n_extra], refs[3 + n_extra]
    k = pl.program_id(2)

    @pl.when(k == 0)
    def _init():
        acc_ref[...] = jnp.zeros_like(acc_ref)

    acc_ref[...] += jnp.dot(a_ref[...], w_ref[...], preferred_element_type=F32)

    @pl.when(k == pl.num_programs(2) - 1)
    def _finish():
        o_ref[...] = epilogue(acc_ref[...], *extra).astype(out_dtype)


def _matmul(a, w, *, tm, tn, tk, out_dtype, epilogue, extra=(), extra_specs=(), name):
    m, kd = a.shape
    n = w.shape[1]
    assert m % tm == 0 and n % tn == 0 and kd % tk == 0, (a.shape, w.shape, tm, tn, tk)
    body = functools.partial(_matmul_body, n_extra=len(extra), epilogue=epilogue, out_dtype=out_dtype)
    return pl.pallas_call(
        body,
        grid=(m // tm, n // tn, kd // tk),
        in_specs=[pl.BlockSpec((tm, tk), lambda i, j, k: (i, k)),
                  pl.BlockSpec((tk, tn), lambda i, j, k: (k, j))] + list(extra_specs),
        out_specs=pl.BlockSpec((tm, tn), lambda i, j, k: (i, j)),
        out_shape=jax.ShapeDtypeStruct((m, n), out_dtype),
        scratch_shapes=[pltpu.VMEM((tm, tn), F32)],
        compiler_params=_params("parallel", "parallel", "arbitrary"),
        name=name,
    )(a, w, *extra)


def _norm_matmul_body(*refs, n_extra, epilogue, out_dtype):
    x_ref, g_ref, w_ref = refs[:3]
    extra = refs[3:3 + n_extra]
    o_ref, h_scr = refs[3 + n_extra], refs[4 + n_extra]

    @pl.when(pl.program_id(1) == 0)
    def _norm():
        x = x_ref[...]
        y = x * lax.rsqrt(jnp.mean(x * x, axis=-1, keepdims=True) + NORM_EPS)
        h_scr[...] = (y * g_ref[...]).astype(h_scr.dtype)

    y = jnp.dot(h_scr[...], w_ref[...], preferred_element_type=F32)
    o_ref[...] = epilogue(y, *extra).astype(out_dtype)


def _norm_matmul(x, g, w, *, tm, tn, out_dtype, epilogue, extra=(), extra_specs=(), name):
    m, d = x.shape
    n = w.shape[1]
    assert m % tm == 0 and n % tn == 0, (x.shape, w.shape, tm, tn)
    body = functools.partial(_norm_matmul_body, n_extra=len(extra), epilogue=epilogue, out_dtype=out_dtype)
    return pl.pallas_call(
        body,
        grid=(m // tm, n // tn),
        in_specs=[pl.BlockSpec((tm, d), lambda i, j: (i, 0)),
                  pl.BlockSpec((1, d), lambda i, j: (0, 0)),
                  pl.BlockSpec((d, tn), lambda i, j: (0, j))] + list(extra_specs),
        out_specs=pl.BlockSpec((tm, tn), lambda i, j: (i, j)),
        out_shape=jax.ShapeDtypeStruct((m, n), out_dtype),
        scratch_shapes=[pltpu.VMEM((tm, d), _CD)],
        compiler_params=_params("parallel", "arbitrary"),
        name=name,
    )(x, g.reshape(1, d).astype(F32), w, *extra)


def _ep_scale(scale):
    def ep(y):
        return y if scale == 1.0 else y * scale
    return ep


def _ep_relu2(y):
    return jnp.square(jnp.maximum(y, 0.0))


def _ep_residual(y, r_ref):
    return y + r_ref[...]


def _ep_rmsnorm(y, g_ref):
    return y * lax.rsqrt(jnp.mean(y * y, axis=-1, keepdims=True) + NORM_EPS) * g_ref[...]


def _ep_rope(pattern, plain_scale):
    def ep(y, cos_ref, sin_ref):
        cos, sin = cos_ref[0], sin_ref[0]
        out = []
        for c, rot in enumerate(pattern):
            slab = y[:, c * LANE:(c + 1) * LANE]
            out.append(_rope_slab(slab, cos, sin) if rot else slab * plain_scale)
        return jnp.concatenate(out, axis=1)
    return ep


def _rope_matmul(a, w, tabs, tab_of_tile, *, seq, tm, tn, pattern, plain_scale=1.0, norm_gain=None, name):
    nblk = seq // tm
    spec = pl.BlockSpec((1, tm, LANE), lambda i, j, *_: (tab_of_tile(j), i % nblk, 0))
    common = dict(tm=tm, tn=tn, out_dtype=_CD, epilogue=_ep_rope(pattern, plain_scale), extra=tabs,
                  extra_specs=(spec, spec), name=name)
    if norm_gain is None:
        return _matmul(a, w, tk=a.shape[1], **common)
    return _norm_matmul(a, norm_gain, w, **common)


def _attn_init(m_scr, l_scr, acc_scr):
    m_scr[...] = jnp.full(m_scr.shape, _NEG, F32)
    l_scr[...] = jnp.zeros(l_scr.shape, F32)
    acc_scr[...] = jnp.zeros(acc_scr.shape, F32)


def _tile_rows(j, blk):
    return pl.ds(pl.multiple_of(j * blk, blk), blk)


def _dot_tn(a, b):
    return lax.dot_general(a, b, (((0,), (0,)), ((), ())), preferred_element_type=F32)


def _key_major(q):
    return q.astype(F32).T.astype(q.dtype)


def _attn_scratch(blk, hg):
    per_head = [pltpu.VMEM((1, blk), F32), pltpu.VMEM((1, blk), F32), pltpu.VMEM((HEAD_DIM, blk), F32)]
    return [pltpu.VMEM((2, hg, blk, blk), F32)] + per_head * hg


def _head_state(scr, hg):
    return scr[0], [scr[1 + 3 * g:4 + 3 * g] for g in range(hg)]


def _attn_step(st, v, m_scr, l_scr, acc_scr):
    m_prev = m_scr[...]
    m_new = jnp.maximum(m_prev, jnp.max(st, axis=0, keepdims=True))
    alpha = jnp.exp2(m_prev - m_new)
    p = jnp.exp2(st - m_new)
    l_scr[...] = alpha * l_scr[...] + jnp.sum(p, axis=0, keepdims=True)
    acc_scr[...] = alpha * acc_scr[...] + _dot_tn(v, p.astype(v.dtype))
    m_scr[...] = m_new


def _attn_out(l_scr, acc_scr, dtype):
    return (acc_scr[...] / l_scr[...]).T.astype(dtype)


def _pipelined_key_tiles(last, scores, consume, first=0):
    n_past = last - first

    def pair(t, carry):
        j = first + 2 * t
        scores(j + 1, 1)
        consume(j, 0, False)
        scores(j + 2, 0)
        consume(j + 1, 1, False)
        return carry

    scores(first, 0)
    lax.fori_loop(0, n_past // 2, pair, 0)

    @pl.when(n_past % 2 == 1)
    def _odd():
        scores(last, 1)
        consume(last - 1, 0, False)
        consume(last, 1, True)

    @pl.when(n_past % 2 == 0)
    def _even():
        consume(last, 0, True)


def _causal(s):
    kpos = lax.broadcasted_iota(jnp.int32, s.shape, 0)
    qpos = lax.broadcasted_iota(jnp.int32, s.shape, 1)
    return jnp.where(kpos <= qpos, s, _NEG)


def _mla_body(q_ref, kn_ref, kp_ref, v_ref, o_ref, *scr, blk, hg):
    i = pl.program_id(2)
    s_scr, heads = _head_state(scr, hg)
    for st in heads:
        _attn_init(*st)
    qts = [_key_major(q_ref[0, :, g * 2 * LANE:(g + 1) * 2 * LANE]) for g in range(hg)]

    def scores(j, slot):
        rows = _tile_rows(j, blk)
        kp = kp_ref[0, rows, :]
        for g in range(hg):
            k = jnp.concatenate([kn_ref[0, rows, g * LANE:(g + 1) * LANE], kp], axis=1)
            s_scr[slot, g] = jnp.dot(k, qts[g], preferred_element_type=F32)

    def consume(j, slot, masked):
        rows = _tile_rows(j, blk)
        for g, st in enumerate(heads):
            s = s_scr[slot, g]
            _attn_step(_causal(s) if masked else s, v_ref[0, rows, g * LANE:(g + 1) * LANE], *st)

    _pipelined_key_tiles(i, scores, consume)
    for g, (_, l_scr, acc_scr) in enumerate(heads):
        o_ref[0, :, g * LANE:(g + 1) * LANE] = _attn_out(l_scr, acc_scr, o_ref.dtype)


def _mla_attention(q, kv, r64, *, blk, hg=_HEADS_PER_STEP):
    b, s, _ = q.shape
    h = MLA_HEADS
    return pl.pallas_call(
        functools.partial(_mla_body, blk=blk, hg=hg),
        grid=(b, h // hg, s // blk),
        in_specs=[pl.BlockSpec((1, blk, hg * 2 * LANE), lambda b_, h_, i: (b_, i, h_)),
                  pl.BlockSpec((1, s, hg * LANE), lambda b_, h_, i: (b_, 0, h_)),
                  pl.BlockSpec((1, s, LANE), lambda b_, h_, i: (b_, 0, _R64_KPE)),
                  pl.BlockSpec((1, s, hg * LANE), lambda b_, h_, i: (b_, 0, h // hg + h_))],
        out_specs=pl.BlockSpec((1, blk, hg * LANE), lambda b_, h_, i: (b_, i, h_)),
        out_shape=jax.ShapeDtypeStruct((b, s, h * MLA_V), _CD),
        scratch_shapes=_attn_scratch(blk, hg),
        compiler_params=_params("parallel", "parallel", "arbitrary"),
        name="mla_attention",
    )(q, kv, r64, kv)


def _order_key(x):
    bits = lax.bitcast_convert_type(x, jnp.int32)
    return bits ^ ((bits >> 31) & _INT_MAX)


def _key_score(key):
    return lax.bitcast_convert_type(key ^ ((key >> 31) & _INT_MAX), F32)


def _dsa_select(i, qi_ref, ka_ref, kb_ref, w_ref, sel_scr, jc_scr, *, blk, topk, seq):
    wt = w_ref[0].T
    qits = [_key_major(qi_ref[0, :, p * LANE:(p + 1) * LANE]) for p in range(IDX_HEADS // 2)]
    kloc = lax.broadcasted_iota(jnp.int32, (blk, blk), 0)
    qpos = i * blk + lax.broadcasted_iota(jnp.int32, (blk, blk), 1)

    def score_tile(c, carry):
        rows = _tile_rows(c, blk)
        ka, kb = ka_ref[0, rows, :], kb_ref[0, rows, :]
        acc = jnp.zeros((blk, blk), F32)
        for p in range(IDX_HEADS // 2):
            da = jnp.dot(ka, qits[p], preferred_element_type=F32)
            db = jnp.dot(kb, qits[p], preferred_element_type=F32)
            acc = acc + wt[2 * p:2 * p + 1, :] * jnp.maximum(da, 0.0) + wt[2 * p + 1:2 * p + 2, :] * jnp.maximum(db, 0.0)
        key = _order_key(acc)
        causal = c * blk + kloc <= qpos
        sel_scr[c] = jnp.where(causal, key, _KEY_NEG_INF)
        lo8, hi8 = carry
        fold = lambda x: x.reshape(blk // SUBLANE, SUBLANE, blk)
        lo8 = jnp.minimum(lo8, jnp.min(fold(jnp.where(causal, key, _INT_MAX)), axis=0))
        hi8 = jnp.maximum(hi8, jnp.max(fold(jnp.where(causal, key, _INT_MIN)), axis=0))
        return lo8, hi8

    lo8, hi8 = lax.fori_loop(0, i + 1, score_tile, (jnp.full((SUBLANE, blk), _INT_MAX, jnp.int32),
                                                    jnp.full((SUBLANE, blk), _INT_MIN, jnp.int32)))
    key_min, key_max = lo8[0:1], hi8[0:1]
    for r in range(1, SUBLANE):
        key_min = jnp.minimum(key_min, lo8[r:r + 1])
        key_max = jnp.maximum(key_max, hi8[r:r + 1])

    n_causal = i * blk + lax.broadcasted_iota(jnp.int32, (1, blk), 1) + 1
    kk = jnp.minimum(topk, n_causal).astype(F32)

    def count(pred):
        def body(c, cnt):
            hit = jnp.where(pred(sel_scr[c], c * blk + kloc), 1.0, 0.0)
            return cnt + jnp.sum(hit.reshape(blk // SUBLANE, SUBLANE, blk), axis=0)
        cnt = lax.fori_loop(0, i + 1, body, jnp.zeros((SUBLANE, blk), F32))
        return jnp.sum(cnt, axis=0, keepdims=True)

    def unfinished(lo, hi, n_lo):
        return (n_lo > kk) & (hi > lo + 1)

    def probe(state):
        lo, hi, n_lo, n_hi, w_lo, w_hi, last, it, _ = state
        lo_f, hi_f = _key_score(lo), _key_score(hi)
        a, b = (n_lo - kk + 0.5) * w_lo, (kk - 0.5 - n_hi) * w_hi
        interp = _order_key(lo_f + (hi_f - lo_f) * (a / (a + b)))
        mid = (lo >> 1) + (hi >> 1) + (lo & hi & 1)
        cand = jnp.clip(jnp.where(it < _INTERP_PROBES, interp, mid), lo + 1, hi - 1)
        n = count(lambda keys, kpos: keys >= cand)
        live = unfinished(lo, hi, n_lo)
        up = live & (n >= kk)
        down = live & (n < kk)
        lo, n_lo = jnp.where(up, cand, lo), jnp.where(up, n, n_lo)
        hi, n_hi = jnp.where(down, cand, hi), jnp.where(down, n, n_hi)
        w_hi = jnp.where(up, jnp.where(last > 0.5, 0.5 * w_hi, 1.0), jnp.where(down, 1.0, w_hi))
        w_lo = jnp.where(down, jnp.where(last < -0.5, 0.5 * w_lo, 1.0), jnp.where(up, 1.0, w_lo))
        last = jnp.where(up, 1.0, jnp.where(down, -1.0, last))
        more = jnp.max(jnp.where(unfinished(lo, hi, n_lo), 1.0, 0.0))
        return lo, hi, n_lo, n_hi, w_lo, w_hi, last, it + 1, more

    n_all = n_causal.astype(F32)
    one, zero = jnp.ones((1, blk), F32), jnp.zeros((1, blk), F32)
    start = (key_min, key_max + 1, n_all, zero, one, one, zero, jnp.int32(0),
             jnp.max(jnp.where(unfinished(key_min, key_max + 1, n_all), 1.0, 0.0)))
    t, _, n_ge = lax.while_loop(lambda s: s[8] > 0.5, probe, start)[:3]
    jc_scr[...] = jnp.full((1, blk), seq, jnp.int32)

    @pl.when(jnp.max(n_ge - kk) > 0.5)
    def _ties():
        need = kk - count(lambda keys, kpos: keys > t)
        nbits = (seq - 1).bit_length()

        def pos_step(b, jc):
            cand = jc + (jnp.int32(1) << (nbits - 1 - b))
            n_lt = count(lambda keys, kpos: (keys == t) & (kpos < cand))
            return jnp.where(n_lt < need, cand, jc)

        jc_scr[...] = lax.fori_loop(0, nbits, pos_step, jnp.zeros((1, blk), jnp.int32))

    jc = jc_scr[...]

    def to_bias(c, carry):
        keys = sel_scr[c]
        chosen = (keys > t) | ((keys == t) & (c * blk + kloc <= jc))
        sel_scr[c] = lax.bitcast_convert_type(jnp.where(chosen, 0.0, _NEG).astype(F32), jnp.int32)
        return carry

    lax.fori_loop(0, i + 1, to_bias, 0)


def _dsa_body(qi_ref, ka_ref, kb_ref, w_ref, q_ref, k_ref, v_ref, o_ref, sel_scr, jc_scr, *scr,
              blk, hg, topk, seq):
    i = pl.program_id(1)

    @pl.when(pl.program_id(2) == 0)
    def _select():
        _dsa_select(i, qi_ref, ka_ref, kb_ref, w_ref, sel_scr, jc_scr, blk=blk, topk=topk, seq=seq)

    s_scr, heads = _head_state(scr, hg)
    for st in heads:
        _attn_init(*st)
    qts = [_key_major(q_ref[0, :, g * LANE:(g + 1) * LANE]) for g in range(hg)]

    def scores(j, slot):
        rows = _tile_rows(j, blk)
        for g in range(hg):
            s_scr[slot, g] = jnp.dot(k_ref[0, rows, g * LANE:(g + 1) * LANE], qts[g], preferred_element_type=F32)

    def consume(j, slot, masked):
        rows = _tile_rows(j, blk)
        bias = lax.bitcast_convert_type(sel_scr[j], F32)
        for g, st in enumerate(heads):
            _attn_step(s_scr[slot, g] + bias, v_ref[0, rows, g * LANE:(g + 1) * LANE], *st)

    _pipelined_key_tiles(i, scores, consume)
    for g, (_, l_scr, acc_scr) in enumerate(heads):
        o_ref[0, :, g * LANE:(g + 1) * LANE] = _attn_out(l_scr, acc_scr, o_ref.dtype)


def _dsa_attention(r64, w_idx, qkv, *, blk, hg=_HEADS_PER_STEP):
    b, s, _ = qkv.shape
    h = DSA_HEADS
    topk = min(DSA_TOPK_MAX, s // 4)
    return pl.pallas_call(
        functools.partial(_dsa_body, blk=blk, hg=hg, topk=topk, seq=s),
        grid=(b, s // blk, h // hg),
        in_specs=[pl.BlockSpec((1, blk, 8 * LANE), lambda b_, i, h_: (b_, i, 0)),
                  pl.BlockSpec((1, s, LANE), lambda b_, i, h_: (b_, 0, _R64_KIA)),
                  pl.BlockSpec((1, s, LANE), lambda b_, i, h_: (b_, 0, _R64_KIB)),
                  pl.BlockSpec((1, blk, LANE), lambda b_, i, h_: (b_, i, 0)),
                  pl.BlockSpec((1, blk, hg * LANE), lambda b_, i, h_: (b_, i, h_)),
                  pl.BlockSpec((1, s, hg * LANE), lambda b_, i, h_: (b_, 0, h // hg + h_)),
                  pl.BlockSpec((1, s, hg * LANE), lambda b_, i, h_: (b_, 0, 2 * (h // hg) + h_))],
        out_specs=pl.BlockSpec((1, blk, hg * LANE), lambda b_, i, h_: (b_, i, h_)),
        out_shape=jax.ShapeDtypeStruct((b, s, h * HEAD_DIM), _CD),
        scratch_shapes=[pltpu.VMEM((s // blk, blk, blk), jnp.int32),
                        pltpu.VMEM((1, blk), jnp.int32)] + _attn_scratch(blk, hg),
        compiler_params=_params("parallel", "arbitrary", "arbitrary"),
        name="dsa_attention",
    )(r64, r64, r64, w_idx, qkv, qkv, qkv)


def _moba_body(q_ref, k_ref, v_ref, o_ref, kmean_scr, pick_scr, *scr, blk, hg, seq, nbp):
    i = pl.program_id(2)
    nper = blk // MOBA_BLOCK
    shift = MOBA_BLOCK.bit_length() - 1
    s_scr, heads = _head_state(scr, hg)

    @pl.when(i == 0)
    def _block_means():
        r = lax.broadcasted_iota(jnp.int32, (nbp, seq), 0)
        c = lax.broadcasted_iota(jnp.int32, (nbp, seq), 1)
        avg = jnp.where((c >> shift) == r, 1.0 / MOBA_BLOCK, 0.0).astype(k_ref.dtype)
        for g in range(hg):
            kmean_scr[g] = jnp.dot(avg, k_ref[0, :, g * LANE:(g + 1) * LANE],
                                   preferred_element_type=F32).astype(kmean_scr.dtype)

    qts = [_key_major(q_ref[0, :, g * LANE:(g + 1) * LANE]) for g in range(hg)]
    kblk = lax.broadcasted_iota(jnp.int32, (nbp, blk), 0)
    own = (i * blk + lax.broadcasted_iota(jnp.int32, (nbp, blk), 1)) >> shift
    kblk_f = kblk.astype(F32)
    for g, st in enumerate(heads):
        _attn_init(*st)
        gate = jnp.where(kblk < own, jnp.dot(kmean_scr[g], qts[g], preferred_element_type=F32), _NEG)
        pick = jnp.zeros((nbp, blk), F32)
        for _ in range(MOBA_TOPK):
            best = jnp.max(gate, axis=0, keepdims=True)
            hit = (gate == best) & (best > 0.5 * _NEG)
            first = jnp.min(jnp.where(hit, kblk_f, float(nbp)), axis=0, keepdims=True)
            new = kblk_f == first
            pick = jnp.where(new, 1.0, pick)
            gate = jnp.where(new, _NEG, gate)
        pick_scr[g] = pick

    def scores(j, slot):
        rows = _tile_rows(j, blk)
        for g in range(hg):
            s_scr[slot, g] = jnp.dot(k_ref[0, rows, g * LANE:(g + 1) * LANE], qts[g], preferred_element_type=F32)

    kloc = lax.broadcasted_iota(jnp.int32, (MOBA_BLOCK, blk), 0)
    qloc = lax.broadcasted_iota(jnp.int32, (MOBA_BLOCK, blk), 1)

    def consume(j, slot, masked):
        rows = _tile_rows(j, blk)
        for g, st in enumerate(heads):
            parts = []
            for u in range(nper):
                s = s_scr[slot, g, u * MOBA_BLOCK:(u + 1) * MOBA_BLOCK, :]
                ok = pick_scr[g, pl.ds(j * nper + u, 1), :] > 0.5
                if masked:
                    kpos = u * MOBA_BLOCK + kloc
                    same = (qloc >> shift) == u
                    ok = (same & (kpos <= qloc)) | (jnp.logical_not(same) & ok & ((qloc >> shift) > u))
                parts.append(jnp.where(ok, s, _NEG))
            _attn_step(jnp.concatenate(parts, axis=0), v_ref[0, rows, g * LANE:(g + 1) * LANE], *st)

    _pipelined_key_tiles(i, scores, consume)
    for g, (_, l_scr, acc_scr) in enumerate(heads):
        o_ref[0, :, g * LANE:(g + 1) * LANE] = _attn_out(l_scr, acc_scr, o_ref.dtype)


def _moba_attention(qk, v, *, blk, q_off, k_off, v_off, hg=_HEADS_PER_STEP):
    b, s, _ = qk.shape
    h = MOBA_HEADS
    assert s % MOBA_BLOCK == 0 and blk % MOBA_BLOCK == 0
    nbp = -(-(s // MOBA_BLOCK) // SUBLANE) * SUBLANE
    return pl.pallas_call(
        functools.partial(_moba_body, blk=blk, hg=hg, seq=s, nbp=nbp),
        grid=(b, h // hg, s // blk),
        in_specs=[pl.BlockSpec((1, blk, hg * LANE), lambda b_, h_, i: (b_, i, q_off // hg + h_)),
                  pl.BlockSpec((1, s, hg * LANE), lambda b_, h_, i: (b_, 0, k_off // hg + h_)),
                  pl.BlockSpec((1, s, hg * LANE), lambda b_, h_, i: (b_, 0, v_off // hg + h_))],
        out_specs=pl.BlockSpec((1, blk, hg * LANE), lambda b_, h_, i: (b_, i, h_)),
        out_shape=jax.ShapeDtypeStruct((b, s, h * HEAD_DIM), _CD),
        scratch_shapes=[pltpu.VMEM((hg, nbp, HEAD_DIM), _CD),
                        pltpu.VMEM((hg, nbp, blk), F32)] + _attn_scratch(blk, hg),
        compiler_params=_params("parallel", "parallel", "arbitrary"),
        name="moba_attention",
    )(qk, qk, v)


def _dilated_body(q_ref, k_ref, v_ref, bias_ref, o_ref, *scr, blk, hg, nrel):
    i = pl.program_id(2)
    s_scr, heads = _head_state(scr, hg)
    for st in heads:
        _attn_init(*st)
    qts = [_key_major(q_ref[0, :, g * LANE:(g + 1) * LANE]) for g in range(hg)]

    def scores(j, slot):
        rows = _tile_rows(j, blk)
        for g in range(hg):
            s_scr[slot, g] = jnp.dot(k_ref[0, rows, g * LANE:(g + 1) * LANE], qts[g], preferred_element_type=F32)

    def consume(j, slot, masked):
        rows = _tile_rows(j, blk)
        bias = bias_ref[i - j]
        for g, st in enumerate(heads):
            _attn_step(s_scr[slot, g] + bias, v_ref[0, rows, g * LANE:(g + 1) * LANE], *st)

    _pipelined_key_tiles(i, scores, consume, first=jnp.maximum(i - (nrel - 1), 0))
    for g, (_, l_scr, acc_scr) in enumerate(heads):
        o_ref[0, :, g * LANE:(g + 1) * LANE] = _attn_out(l_scr, acc_scr, o_ref.dtype)


def _dilated_bias(blk):
    reach = max(w for w, _ in DIL_PATTERNS)
    nrel = -(-reach // blk) + 1
    rel = jnp.arange(nrel, dtype=jnp.int32)[:, None, None]
    k = jnp.arange(blk, dtype=jnp.int32)[None, :, None]
    q = jnp.arange(blk, dtype=jnp.int32)[None, None, :]
    d = rel * blk + q - k
    mult = jnp.zeros(d.shape, F32)
    for window, dil in DIL_PATTERNS:
        mult = mult + ((d >= 0) & (d <= (window // dil) * dil) & (d % dil == 0)).astype(F32)
    return jnp.where(mult > 0, jnp.log2(jnp.maximum(mult, 1.0)), _NEG), nrel


def _dilated_attention(qk, v, *, blk, q_off, k_off, v_off, hg=_HEADS_PER_STEP):
    b, s, _ = qk.shape
    h = DIL_HEADS
    bias, nrel = _dilated_bias(blk)
    return pl.pallas_call(
        functools.partial(_dilated_body, blk=blk, hg=hg, nrel=nrel),
        grid=(b, h // hg, s // blk),
        in_specs=[pl.BlockSpec((1, blk, hg * LANE), lambda b_, h_, i: (b_, i, q_off // hg + h_)),
                  pl.BlockSpec((1, s, hg * LANE), lambda b_, h_, i: (b_, 0, k_off // hg + h_)),
                  pl.BlockSpec((1, s, hg * LANE), lambda b_, h_, i: (b_, 0, v_off // hg + h_)),
                  pl.BlockSpec((nrel, blk, blk), lambda b_, h_, i: (0, 0, 0))],
        out_specs=pl.BlockSpec((1, blk, hg * LANE), lambda b_, h_, i: (b_, i, h_)),
        out_shape=jax.ShapeDtypeStruct((b, s, h * HEAD_DIM), _CD),
        scratch_shapes=_attn_scratch(blk, hg),
        compiler_params=_params("parallel", "parallel", "arbitrary"),
        name="dilated_attention",
    )(qk, qk, v, bias)


_E_CQ, _E_CKV, _E_KPE, _E_Q, _E_K, _E_V, _E_QI, _E_KI, _E_WI = [
    int(o) for o in np.cumsum([0, MLA_Q_RANK, MLA_KV_RANK, MLA_ROPE, DSA_HEADS * HEAD_DIM, DSA_HEADS * HEAD_DIM,
                               DSA_HEADS * HEAD_DIM, IDX_HEADS * IDX_DIM, IDX_DIM])]
_R64_KIA, _R64_KIB, _R64_KPE, _R64_SLABS = 8, 9, 10, 12


def _take_cols(w, idx):
    idx = np.asarray(idx)
    cols = jnp.take(w, jnp.asarray(np.maximum(idx, 0)), axis=1)
    return jnp.where(jnp.asarray(idx >= 0)[None, :], cols, 0.0).astype(_CD)


def _r64_columns():
    half = IDX_DIM // 2
    a = np.arange(half)
    z = -np.ones(half, np.int64)
    cols = []
    for p in range(IDX_HEADS // 2):
        ha, hb = _E_QI + 2 * p * IDX_DIM, _E_QI + (2 * p + 1) * IDX_DIM
        cols += [ha + a, hb + a, ha + half + a, hb + half + a]
    cols += [_E_KI + a, z, _E_KI + half + a, z]
    cols += [z, _E_KI + a, z, _E_KI + half + a]
    cols += [_E_KPE + a, z, _E_KPE + half + a, z]
    cols += [z, z, z, z]
    return np.concatenate(cols)


def _uq_columns():
    half = MLA_ROPE // 2
    a = np.arange(half)
    z = -np.ones(half, np.int64)
    cols = []
    for h in range(MLA_HEADS):
        o = h * (MLA_NOPE + MLA_ROPE)
        cols += [o + np.arange(MLA_NOPE), o + MLA_NOPE + a, z, o + MLA_NOPE + half + a, z]
    return np.concatenate(cols)


def _ukv_columns():
    per = MLA_NOPE + MLA_V
    kn = [h * per + np.arange(MLA_NOPE) for h in range(MLA_HEADS)]
    vv = [h * per + MLA_NOPE + np.arange(MLA_V) for h in range(MLA_HEADS)]
    return np.concatenate(kn + vv)


def _rope_tables(seq, dim, scales, with_identity=False):
    inv = ROPE_THETA ** (-jnp.arange(0, dim, 2, dtype=F32) / dim)
    ang = jnp.arange(seq, dtype=F32)[:, None] * inv[None, :]
    reps = (LANE // 2) // (dim // 2)
    cos = jnp.tile(jnp.cos(ang), (1, 2 * reps))
    sin = jnp.tile(jnp.sin(ang), (1, reps))
    sin = jnp.concatenate([-sin, sin], axis=1)
    sc = jnp.asarray(scales, F32)[:, None, None]
    cos, sin = cos[None] * sc, sin[None] * sc
    if with_identity:
        cos = jnp.concatenate([cos, jnp.ones((1, seq, LANE), F32)], axis=0)
        sin = jnp.concatenate([sin, jnp.zeros((1, seq, LANE), F32)], axis=0)
    return cos, sin


def _mlp_block(x2, g, w1, w2, *, tm):
    up = _norm_matmul(x2, g, w1.astype(_CD), tm=tm, tn=1024, out_dtype=_CD, epilogue=_ep_relu2, name="mlp_up")
    res = pl.BlockSpec((tm, 1024), lambda i, j, k: (i, j))
    return _matmul(up, w2.astype(_CD), tm=tm, tn=1024, tk=2048, out_dtype=F32,
                   epilogue=_ep_residual, extra=(x2,), extra_specs=(res,), name="mlp_down")


def _out_proj_body(a_ref, b_ref, w_ref, r_ref, o_ref):
    half = a_ref.shape[1]
    y = jnp.dot(a_ref[...], w_ref[:half, :], preferred_element_type=F32)
    y = y + jnp.dot(b_ref[...], w_ref[half:, :], preferred_element_type=F32)
    o_ref[...] = y + r_ref[...]


def _out_proj(a, b, w_out, x2, *, tm, tn=1024):
    t, half = a.shape
    d = w_out.shape[1]
    return pl.pallas_call(
        _out_proj_body,
        grid=(t // tm, d // tn),
        in_specs=[pl.BlockSpec((tm, half), lambda i, j: (i, 0)),
                  pl.BlockSpec((tm, half), lambda i, j: (i, 0)),
                  pl.BlockSpec((2 * half, tn), lambda i, j: (0, j)),
                  pl.BlockSpec((tm, tn), lambda i, j: (i, j))],
        out_specs=pl.BlockSpec((tm, tn), lambda i, j: (i, j)),
        out_shape=jax.ShapeDtypeStruct((t, d), F32),
        compiler_params=_params("parallel", "parallel"),
        name="out_proj",
    )(a, b, w_out.astype(a.dtype), x2)


def _even_mixer(x2, g_mix, w_in, g_q, g_kv, w_uq, w_ukv, w_out, *, batch, seq, blk):
    t, d = x2.shape
    tm = min(2 * blk, seq)
    hm = _rmsnorm(x2, g_mix, _CD)
    mla_scale = _LOG2E * (MLA_NOPE + MLA_ROPE) ** -0.5
    rope64 = _rope_tables(seq, IDX_DIM, (1.0, mla_scale))
    rope128 = _rope_tables(seq, HEAD_DIM, (_LOG2E * HEAD_DIM ** -0.5, 1.0), with_identity=True)
    nh = DSA_HEADS * HEAD_DIM

    r64 = _rope_matmul(hm, _take_cols(w_in, _r64_columns()), rope64, lambda j: 0, seq=seq, tm=tm,
                       tn=_R64_SLABS // 2 * LANE, pattern=(True,) * (_R64_SLABS // 2), name="in_proj_rope64")
    qkv = _rope_matmul(hm, w_in[:, _E_Q:_E_QI].astype(_CD), rope128, lambda j: j, seq=seq, tm=tm,
                       tn=nh, pattern=(True,) * DSA_HEADS, name="in_proj_qkv")
    w_idx = _matmul(hm, _take_cols(w_in, np.concatenate([_E_WI + np.arange(IDX_HEADS),
                                                          -np.ones(LANE - IDX_HEADS, np.int64)])),
                    tm=tm, tn=LANE, tk=d, out_dtype=F32,
                    epilogue=_ep_scale(IDX_HEADS ** -0.5 * IDX_DIM ** -0.5), name="in_proj_widx")
    gspec = lambda n: pl.BlockSpec((1, n), lambda i, j, k: (0, 0))
    c_q = _matmul(hm, w_in[:, _E_CQ:_E_CKV].astype(_CD), tm=tm, tn=MLA_Q_RANK, tk=d, out_dtype=_CD,
                  epilogue=_ep_rmsnorm, extra=(g_q.reshape(1, -1),), extra_specs=(gspec(MLA_Q_RANK),),
                  name="in_proj_cq")
    c_kv = _matmul(hm, w_in[:, _E_CKV:_E_KPE].astype(_CD), tm=tm, tn=MLA_KV_RANK, tk=d, out_dtype=_CD,
                   epilogue=_ep_rmsnorm, extra=(g_kv.reshape(1, -1),), extra_specs=(gspec(MLA_KV_RANK),),
                   name="in_proj_ckv")

    q_mla = _rope_matmul(c_q, _take_cols(w_uq, _uq_columns()), rope64, lambda j: 1, seq=seq, tm=tm, tn=1024,
                         pattern=(False, True) * 4, plain_scale=mla_scale, name="mla_q_up")
    kv_mla = _matmul(c_kv, _take_cols(w_ukv, _ukv_columns()), tm=tm, tn=1024, tk=MLA_KV_RANK, out_dtype=_CD,
                     epilogue=_ep_scale(1.0), name="mla_kv_up")

    sh = lambda z: z.reshape(batch, seq, z.shape[-1])
    a = _mla_attention(sh(q_mla), sh(kv_mla), sh(r64), blk=blk)
    bsa = _dsa_attention(sh(r64), sh(w_idx), sh(qkv), blk=blk)
    return _out_proj(a.reshape(t, -1), bsa.reshape(t, -1), w_out, x2, tm=tm)


def _odd_mixer(x2, g_mix, w_in, w_out, *, batch, seq, blk):
    t, d = x2.shape
    tm = min(2 * blk, seq)
    nh = MOBA_HEADS * HEAD_DIM
    rope128 = _rope_tables(seq, HEAD_DIM, (_LOG2E * HEAD_DIM ** -0.5, 1.0), with_identity=True)
    qkv = _rope_matmul(x2, w_in.astype(_CD), rope128, lambda j: j % 3, seq=seq, tm=tm, tn=nh,
                       pattern=(True,) * MOBA_HEADS, norm_gain=g_mix, name="in_proj_odd")
    qkv = qkv.reshape(batch, seq, qkv.shape[-1])
    hs = MOBA_HEADS
    c = _moba_attention(qkv, qkv, blk=blk, q_off=0, k_off=hs, v_off=2 * hs)
    dl = _dilated_attention(qkv, qkv, blk=blk, q_off=3 * hs, k_off=4 * hs, v_off=5 * hs)
    return _out_proj(c.reshape(t, -1), dl.reshape(t, -1), w_out, x2, tm=tm)


def kernel(x, ln_mix, ln_mlp, ln_final, e_w_in, e_g_q, e_g_kv, e_w_uq, e_w_ukv, e_w_out,
           o_w_in, o_w_out, mlp_w1, mlp_w2):
    batch, seq, d = x.shape
    blk = min(512, seq)
    assert seq % blk == 0 and blk % MOBA_BLOCK == 0
    x2 = x.reshape(batch * seq, d)
    depth = ln_mix.shape[0]
    for layer in range(depth):
        j = layer // 2
        if layer % 2 == 0:
            x2 = _even_mixer(x2, ln_mix[layer], e_w_in[j], e_g_q[j], e_g_kv[j], e_w_uq[j], e_w_ukv[j],
                             e_w_out[j], batch=batch, seq=seq, blk=blk)
        else:
            x2 = _odd_mixer(x2, ln_mix[layer], o_w_in[j], o_w_out[j], batch=batch, seq=seq, blk=blk)
        x2 = _mlp_block(x2, ln_mlp[layer], mlp_w1[layer], mlp_w2[layer], tm=min(2 * blk, batch * seq))
    return _rmsnorm(x2, ln_final, x.dtype).reshape(batch, seq, d)
```

```python
import functools

import numpy as np
import jax
import jax.numpy as jnp
from jax import lax
from jax.experimental import pallas as pl
from jax.experimental.pallas import tpu as pltpu

HEAD_DIM = 128
ROPE_THETA = 10000.0
NORM_EPS = 1e-6
MLA_HEADS, MLA_Q_RANK, MLA_KV_RANK, MLA_NOPE, MLA_ROPE, MLA_V = 8, 512, 256, 128, 64, 128
DSA_HEADS, IDX_HEADS, IDX_DIM, DSA_TOPK_MAX = 8, 16, 64, 256
MOBA_HEADS, MOBA_BLOCK, MOBA_TOPK = 8, 256, 3
DIL_HEADS = 8
DIL_PATTERNS = ((128, 1), (512, 4), (2048, 16))

LANE = 128
SUBLANE = 8
VMEM_LIMIT_BYTES = 56 * 2**20

F32 = jnp.float32
_CD = jnp.bfloat16
_NEG = -1e30
_LOG2E = 1.4426950408889634
_INT_MIN = -(2**31)
_INT_MAX = 2**31 - 1
_KEY_NEG_INF = int(np.int32(np.uint32(0xFF800000) ^ np.uint32(0x7FFFFFFF)))
_HEADS_PER_STEP = 2
_INTERP_PROBES = 40


def _params(*sem):
    return pltpu.CompilerParams(dimension_semantics=sem, vmem_limit_bytes=VMEM_LIMIT_BYTES)


def _rmsnorm_body(x_ref, g_ref, o_ref):
    x = x_ref[...].astype(F32)
    y = x * lax.rsqrt(jnp.mean(x * x, axis=-1, keepdims=True) + NORM_EPS)
    o_ref[...] = (y * g_ref[...]).astype(o_ref.dtype)


def _rmsnorm(x, g, out_dtype, tm=512):
    m, d = x.shape
    return pl.pallas_call(
        _rmsnorm_body,
        grid=(m // tm,),
        in_specs=[pl.BlockSpec((tm, d), lambda i: (i, 0)), pl.BlockSpec((1, d), lambda i: (0, 0))],
        out_specs=pl.BlockSpec((tm, d), lambda i: (i, 0)),
        out_shape=jax.ShapeDtypeStruct((m, d), out_dtype),
        compiler_params=_params("parallel"),
        name="rmsnorm",
    )(x, g.reshape(1, d).astype(F32))


def _cast_body(x_ref, o_ref):
    o_ref[...] = x_ref[...].astype(o_ref.dtype)


def _to_mxu_dtype(w, layer=None, tm=256):
    m, n = w.shape[-2:]
    if layer is None:
        spec = pl.BlockSpec((tm, n), lambda i: (i, 0))
    else:
        spec = pl.BlockSpec((None, tm, n), lambda i: (layer, i, 0))
    return pl.pallas_call(
        _cast_body,
        grid=(m // tm,),
        in_specs=[spec],
        out_specs=pl.BlockSpec((tm, n), lambda i: (i, 0)),
        out_shape=jax.ShapeDtypeStruct((m, n), _CD),
        compiler_params=_params("parallel"),
        name="weight_cast",
    )(w)


def _rope_slab(y, cos, sin):
    return y * cos + pltpu.roll(y, LANE // 2, 1) * sin


def _matmul_body(*refs, n_extra, epilogue, out_dtype):
    a_ref, w_ref = refs[0], refs[1]
    extra = refs[2:2 + n_extra]
    o_ref, acc_ref = refs[2 + n_extra], refs[3 + n_extra]
    k = pl.program_id(2)

    @pl.when(k == 0)
    def _init():
        acc_ref[...] = jnp.zeros_like(acc_ref)

    acc_ref[...] += jnp.dot(a_ref[...], w_ref[...], preferred_element_type=F32)

    @pl.when(k == pl.num_programs(2) - 1)
    def _finish():
        o_ref[...] = epilogue(acc_ref[...], *extra).astype(out_dtype)


def _matmul(a, w, *, tm, tn, tk, out_dtype, epilogue, extra=(), extra_specs=(), name):
    m, kd = a.shape
    n = w.shape[1]
    assert m % tm == 0 and n % tn == 0 and kd % tk == 0, (a.shape, w.shape, tm, tn, tk)
    body = functools.partial(_matmul_body, n_extra=len(extra), epilogue=epilogue, out_dtype=out_dtype)
    return pl.pallas_call(
        body,
        grid=(m // tm, n // tn, kd // tk),
        in_specs=[pl.BlockSpec((tm, tk), lambda i, j, k: (i, k)),
                  pl.BlockSpec((tk, tn), lambda i, j, k: (k, j))] + list(extra_specs),
        out_specs=pl.BlockSpec((tm, tn), lambda i, j, k: (i, j)),
        out_shape=jax.ShapeDtypeStruct((m, n), out_dtype),
        scratch_shapes=[pltpu.VMEM((tm, tn), F32)],
        compiler_params=_params("parallel", "parallel", "arbitrary"),
        name=name,
    )(a, w, *extra)


def _norm_matmul_body(*refs, n_extra, epilogue, out_dtype):
    x_ref, g_ref, w_ref = refs[:3]
    extra = refs[3:3 + n_extra]
    o_ref, h_scr = refs[3 + n_extra], refs[4 + n_extra]

    @pl.when(pl.program_id(1) == 0)
    def _norm():
        x = x_ref[...]
        y = x * lax.rsqrt(jnp.mean(x * x, axis=-1, keepdims=True) + NORM_EPS)
        h_scr[...] = (y * g_ref[...]).astype(h_scr.dtype)

    y = jnp.dot(h_scr[...], w_ref[...], preferred_element_type=F32)
    o_ref[...] = epilogue(y, *extra).astype(out_dtype)


def _norm_matmul(x, g, w, *, tm, tn, out_dtype, epilogue, extra=(), extra_specs=(), name):
    m, d = x.shape
    n = w.shape[1]
    assert m % tm == 0 and n % tn == 0, (x.shape, w.shape, tm, tn)
    body = functools.partial(_norm_matmul_body, n_extra=len(extra), epilogue=epilogue, out_dtype=out_dtype)
    return pl.pallas_call(
        body,
        grid=(m // tm, n // tn),
        in_specs=[pl.BlockSpec((tm, d), lambda i, j: (i, 0)),
                  pl.BlockSpec((1, d), lambda i, j: (0, 0)),
                  pl.BlockSpec((d, tn), lambda i, j: (0, j))] + list(extra_specs),
        out_specs=pl.BlockSpec((tm, tn), lambda i, j: (i, j)),
        out_shape=jax.ShapeDtypeStruct((m, n), out_dtype),
        scratch_shapes=[pltpu.VMEM((tm, d), _CD)],
        compiler_params=_params("parallel", "arbitrary"),
        name=name,
    )(x, g.reshape(1, d).astype(F32), w, *extra)


def _ep_scale(scale):
    def ep(y):
        return y if scale == 1.0 else y * scale
    return ep


def _ep_relu2(y):
    return jnp.square(jnp.maximum(y, 0.0))


def _ep_residual(y, r_ref):
    return y + r_ref[...]


def _ep_rmsnorm(y, g_ref):
    return y * lax.rsqrt(jnp.mean(y * y, axis=-1, keepdims=True) + NORM_EPS) * g_ref[...]


def _ep_rope(pattern, plain_scale):
    def ep(y, cos_ref, sin_ref):
        cos, sin = cos_ref[0], sin_ref[0]
        out = []
        for c, rot in enumerate(pattern):
            slab = y[:, c * LANE:(c + 1) * LANE]
            out.append(_rope_slab(slab, cos, sin) if rot else slab * plain_scale)
        return jnp.concatenate(out, axis=1)
    return ep


def _rope_matmul(a, w, tabs, tab_of_tile, *, seq, tm, tn, pattern, plain_scale=1.0, norm_gain=None, name):
    nblk = seq // tm
    spec = pl.BlockSpec((1, tm, LANE), lambda i, j, *_: (tab_of_tile(j), i % nblk, 0))
    common = dict(tm=tm, tn=tn, out_dtype=_CD, epilogue=_ep_rope(pattern, plain_scale), extra=tabs,
                  extra_specs=(spec, spec), name=name)
    if norm_gain is None:
        return _matmul(a, w, tk=a.shape[1], **common)
    return _norm_matmul(a, norm_gain, w, **common)


def _attn_init(m_scr, l_scr, acc_scr):
    m_scr[...] = jnp.full(m_scr.shape, _NEG, F32)
    l_scr[...] = jnp.zeros(l_scr.shape, F32)
    acc_scr[...] = jnp.zeros(acc_scr.shape, F32)


def _tile_rows(j, blk):
    return pl.ds(pl.multiple_of(j * blk, blk), blk)


def _dot_tn(a, b):
    return lax.dot_general(a, b, (((0,), (0,)), ((), ())), preferred_element_type=F32)


def _key_major(q):
    return q.astype(F32).T.astype(q.dtype)


def _attn_scratch(blk, hg):
    per_head = [pltpu.VMEM((1, blk), F32), pltpu.VMEM((1, blk), F32), pltpu.VMEM((HEAD_DIM, blk), F32)]
    return [pltpu.VMEM((2, hg, blk, blk), F32)] + per_head * hg


def _head_state(scr, hg):
    return scr[0], [scr[1 + 3 * g:4 + 3 * g] for g in range(hg)]


def _attn_step(st, v, m_scr, l_scr, acc_scr):
    m_prev = m_scr[...]
    m_new = jnp.maximum(m_prev, jnp.max(st, axis=0, keepdims=True))
    alpha = jnp.exp2(m_prev - m_new)
    p = jnp.exp2(st - m_new)
    l_scr[...] = alpha * l_scr[...] + jnp.sum(p, axis=0, keepdims=True)
    acc_scr[...] = alpha * acc_scr[...] + _dot_tn(v, p.astype(v.dtype))
    m_scr[...] = m_new


def _attn_out(l_scr, acc_scr, dtype):
    return (acc_scr[...] / l_scr[...]).T.astype(dtype)


def _pipelined_key_tiles(last, scores, consume, first=0):
    n_past = last - first

    def pair(t, carry):
        j = first + 2 * t
        scores(j + 1, 1)
        consume(j, 0, False)
        scores(j + 2, 0)
        consume(j + 1, 1, False)
        return carry

    scores(first, 0)
    lax.fori_loop(0, n_past // 2, pair, 0)

    @pl.when(n_past % 2 == 1)
    def _odd():
        scores(last, 1)
        consume(last - 1, 0, False)
        consume(last, 1, True)

    @pl.when(n_past % 2 == 0)
    def _even():
        consume(last, 0, True)


def _causal(s):
    kpos = lax.broadcasted_iota(jnp.int32, s.shape, 0)
    qpos = lax.broadcasted_iota(jnp.int32, s.shape, 1)
    return jnp.where(kpos <= qpos, s, _NEG)


def _mla_body(q_ref, kn_ref, kp_ref, v_ref, o_ref, *scr, blk, hg):
    i = pl.program_id(2)
    s_scr, heads = _head_state(scr, hg)
    for st in heads:
        _attn_init(*st)
    qts = [_key_major(q_ref[0, :, g * 2 * LANE:(g + 1) * 2 * LANE]) for g in range(hg)]

    def scores(j, slot):
        rows = _tile_rows(j, blk)
        kp = kp_ref[0, rows, :]
        for g in range(hg):
            k = jnp.concatenate([kn_ref[0, rows, g * LANE:(g + 1) * LANE], kp], axis=1)
            s_scr[slot, g] = jnp.dot(k, qts[g], preferred_element_type=F32)

    def consume(j, slot, masked):
        rows = _tile_rows(j, blk)
        for g, st in enumerate(heads):
            s = s_scr[slot, g]
            _attn_step(_causal(s) if masked else s, v_ref[0, rows, g * LANE:(g + 1) * LANE], *st)

    _pipelined_key_tiles(i, scores, consume)
    for g, (_, l_scr, acc_scr) in enumerate(heads):
        o_ref[0, :, g * LANE:(g + 1) * LANE] = _attn_out(l_scr, acc_scr, o_ref.dtype)


def _mla_attention(q, kv, r64, *, blk, hg=_HEADS_PER_STEP):
    b, s, _ = q.shape
    h = MLA_HEADS
    return pl.pallas_call(
        functools.partial(_mla_body, blk=blk, hg=hg),
        grid=(b, h // hg, s // blk),
        in_specs=[pl.BlockSpec((1, blk, hg * 2 * LANE), lambda b_, h_, i: (b_, i, h_)),
                  pl.BlockSpec((1, s, hg * LANE), lambda b_, h_, i: (b_, 0, h_)),
                  pl.BlockSpec((1, s, LANE), lambda b_, h_, i: (b_, 0, _R64_KPE)),
                  pl.BlockSpec((1, s, hg * LANE), lambda b_, h_, i: (b_, 0, h // hg + h_))],
        out_specs=pl.BlockSpec((1, blk, hg * LANE), lambda b_, h_, i: (b_, i, h_)),
        out_shape=jax.ShapeDtypeStruct((b, s, h * MLA_V), _CD),
        scratch_shapes=_attn_scratch(blk, hg),
        compiler_params=_params("parallel", "parallel", "arbitrary"),
        name="mla_attention",
    )(q, kv, r64, kv)


def _order_key(x):
    bits = lax.bitcast_convert_type(x, jnp.int32)
    return bits ^ ((bits >> 31) & _INT_MAX)


def _key_score(key):
    return lax.bitcast_convert_type(key ^ ((key >> 31) & _INT_MAX), F32)


def _dsa_select(i, qi_ref, ka_ref, kb_ref, w_ref, sel_scr, jc_scr, *, blk, topk, seq):
    wt = w_ref[0].T
    qits = [_key_major(qi_ref[0, :, p * LANE:(p + 1) * LANE]) for p in range(IDX_HEADS // 2)]
    kloc = lax.broadcasted_iota(jnp.int32, (blk, blk), 0)
    qpos = i * blk + lax.broadcasted_iota(jnp.int32, (blk, blk), 1)

    def score_tile(c, carry):
        rows = _tile_rows(c, blk)
        ka, kb = ka_ref[0, rows, :], kb_ref[0, rows, :]
        acc = jnp.zeros((blk, blk), F32)
        for p in range(IDX_HEADS // 2):
            da = jnp.dot(ka, qits[p], preferred_element_type=F32)
            db = jnp.dot(kb, qits[p], preferred_element_type=F32)
            acc = acc + wt[2 * p:2 * p + 1, :] * jnp.maximum(da, 0.0) + wt[2 * p + 1:2 * p + 2, :] * jnp.maximum(db, 0.0)
        key = _order_key(acc)
        causal = c * blk + kloc <= qpos
        sel_scr[c] = jnp.where(causal, key, _KEY_NEG_INF)
        lo8, hi8 = carry
        fold = lambda x: x.reshape(blk // SUBLANE, SUBLANE, blk)
        lo8 = jnp.minimum(lo8, jnp.min(fold(jnp.where(causal, key, _INT_MAX)), axis=0))
        hi8 = jnp.maximum(hi8, jnp.max(fold(jnp.where(causal, key, _INT_MIN)), axis=0))
        return lo8, hi8

    lo8, hi8 = lax.fori_loop(0, i + 1, score_tile, (jnp.full((SUBLANE, blk), _INT_MAX, jnp.int32),
                                                    jnp.full((SUBLANE, blk), _INT_MIN, jnp.int32)))
    key_min, key_max = lo8[0:1], hi8[0:1]
    for r in range(1, SUBLANE):
        key_min = jnp.minimum(key_min, lo8[r:r + 1])
        key_max = jnp.maximum(key_max, hi8[r:r + 1])

    n_causal = i * blk + lax.broadcasted_iota(jnp.int32, (1, blk), 1) + 1
    kk = jnp.minimum(topk, n_causal).astype(F32)

    def count(pred):
        def body(c, cnt):
            hit = jnp.where(pred(sel_scr[c], c * blk + kloc), 1.0, 0.0)
            return cnt + jnp.sum(hit.reshape(blk // SUBLANE, SUBLANE, blk), axis=0)
        cnt = lax.fori_loop(0, i + 1, body, jnp.zeros((SUBLANE, blk), F32))
        return jnp.sum(cnt, axis=0, keepdims=True)

    def unfinished(lo, hi, n_lo):
        return (n_lo > kk) & (hi > lo + 1)

    def probe(state):
        lo, hi, n_lo, n_hi, w_lo, w_hi, last, it, _ = state
        lo_f, hi_f = _key_score(lo), _key_score(hi)
        a, b = (n_lo - kk + 0.5) * w_lo, (kk - 0.5 - n_hi) * w_hi
        interp = _order_key(lo_f + (hi_f - lo_f) * (a / (a + b)))
        mid = (lo >> 1) + (hi >> 1) + (lo & hi & 1)
        cand = jnp.clip(jnp.where(it < _INTERP_PROBES, interp, mid), lo + 1, hi - 1)
        n = count(lambda keys, kpos: keys >= cand)
        live = unfinished(lo, hi, n_lo)
        up = live & (n >= kk)
        down = live & (n < kk)
        lo, n_lo = jnp.where(up, cand, lo), jnp.where(up, n, n_lo)
        hi, n_hi = jnp.where(down, cand, hi), jnp.where(down, n, n_hi)
        w_hi = jnp.where(up, jnp.where(last > 0.5, 0.5 * w_hi, 1.0), jnp.where(down, 1.0, w_hi))
        w_lo = jnp.where(down, jnp.where(last < -0.5, 0.5 * w_lo, 1.0), jnp.where(up, 1.0, w_lo))
        last = jnp.where(up, 1.0, jnp.where(down, -1.0, last))
        more = jnp.max(jnp.where(unfinished(lo, hi, n_lo), 1.0, 0.0))
        return lo, hi, n_lo, n_hi, w_lo, w_hi, last, it + 1, more

    n_all = n_causal.astype(F32)
    one, zero = jnp.ones((1, blk), F32), jnp.zeros((1, blk), F32)
    start = (key_min, key_max + 1, n_all, zero, one, one, zero, jnp.int32(0),
             jnp.max(jnp.where(unfinished(key_min, key_max + 1, n_all), 1.0, 0.0)))
    t, _, n_ge = lax.while_loop(lambda s: s[8] > 0.5, probe, start)[:3]
    jc_scr[...] = jnp.full((1, blk), seq, jnp.int32)

    @pl.when(jnp.max(n_ge - kk) > 0.5)
    def _ties():
        need = kk - count(lambda keys, kpos: keys > t)
        nbits = (seq - 1).bit_length()

        def pos_step(b, jc):
            cand = jc + (jnp.int32(1) << (nbits - 1 - b))
            n_lt = count(lambda keys, kpos: (keys == t) & (kpos < cand))
            return jnp.where(n_lt < need, cand, jc)

        jc_scr[...] = lax.fori_loop(0, nbits, pos_step, jnp.zeros((1, blk), jnp.int32))

    jc = jc_scr[...]

    def to_bias(c, carry):
        keys = sel_scr[c]
        chosen = (keys > t) | ((keys == t) & (c * blk + kloc <= jc))
        sel_scr[c] = lax.bitcast_convert_type(jnp.where(chosen, 0.0, _NEG).astype(F32), jnp.int32)
        return carry

    lax.fori_loop(0, i + 1, to_bias, 0)


def _dsa_body(qi_ref, ka_ref, kb_ref, w_ref, q_ref, k_ref, v_ref, o_ref, sel_scr, jc_scr, *scr,
              blk, hg, topk, seq):
    i = pl.program_id(1)

    @pl.when(pl.program_id(2) == 0)
    def _select():
        _dsa_select(i, qi_ref, ka_ref, kb_ref, w_ref, sel_scr, jc_scr, blk=blk, topk=topk, seq=seq)

    s_scr, heads = _head_state(scr, hg)
    for st in heads:
        _attn_init(*st)
    qts = [_key_major(q_ref[0, :, g * LANE:(g + 1) * LANE]) for g in range(hg)]

    def scores(j, slot):
        rows = _tile_rows(j, blk)
        bias = lax.bitcast_convert_type(sel_scr[j], F32)
        for g in range(hg):
            s_scr[slot, g] = bias + jnp.dot(k_ref[0, rows, g * LANE:(g + 1) * LANE], qts[g],
                                            preferred_element_type=F32)

    def consume(j, slot, masked):
        rows = _tile_rows(j, blk)
        for g, st in enumerate(heads):
            _attn_step(s_scr[slot, g], v_ref[0, rows, g * LANE:(g + 1) * LANE], *st)

    _pipelined_key_tiles(i, scores, consume)
    for g, (_, l_scr, acc_scr) in enumerate(heads):
        o_ref[0, :, g * LANE:(g + 1) * LANE] = _attn_out(l_scr, acc_scr, o_ref.dtype)


def _dsa_attention(r64, w_idx, qkv, *, blk, hg=_HEADS_PER_STEP):
    b, s, _ = qkv.shape
    h = DSA_HEADS
    topk = min(DSA_TOPK_MAX, s // 4)
    return pl.pallas_call(
        functools.partial(_dsa_body, blk=blk, hg=hg, topk=topk, seq=s),
        grid=(b, s // blk, h // hg),
        in_specs=[pl.BlockSpec((1, blk, 8 * LANE), lambda b_, i, h_: (b_, i, 0)),
                  pl.BlockSpec((1, s, LANE), lambda b_, i, h_: (b_, 0, _R64_KIA)),
                  pl.BlockSpec((1, s, LANE), lambda b_, i, h_: (b_, 0, _R64_KIB)),
                  pl.BlockSpec((1, blk, LANE), lambda b_, i, h_: (b_, i, 0)),
                  pl.BlockSpec((1, blk, hg * LANE), lambda b_, i, h_: (b_, i, h_)),
                  pl.BlockSpec((1, s, hg * LANE), lambda b_, i, h_: (b_, 0, h // hg + h_)),
                  pl.BlockSpec((1, s, hg * LANE), lambda b_, i, h_: (b_, 0, 2 * (h // hg) + h_))],
        out_specs=pl.BlockSpec((1, blk, hg * LANE), lambda b_, i, h_: (b_, i, h_)),
        out_shape=jax.ShapeDtypeStruct((b, s, h * HEAD_DIM), _CD),
        scratch_shapes=[pltpu.VMEM((s // blk, blk, blk), jnp.int32),
                        pltpu.VMEM((1, blk), jnp.int32)] + _attn_scratch(blk, hg),
        compiler_params=_params("parallel", "arbitrary", "arbitrary"),
        name="dsa_attention",
    )(r64, r64, r64, w_idx, qkv, qkv, qkv)


def _moba_body(q_ref, k_ref, v_ref, o_ref, kmean_scr, pick_scr, *scr, blk, hg, seq, nbp):
    i = pl.program_id(2)
    nper = blk // MOBA_BLOCK
    shift = MOBA_BLOCK.bit_length() - 1
    s_scr, heads = _head_state(scr, hg)

    @pl.when(i == 0)
    def _block_means():
        r = lax.broadcasted_iota(jnp.int32, (nbp, seq), 0)
        c = lax.broadcasted_iota(jnp.int32, (nbp, seq), 1)
        avg = jnp.where((c >> shift) == r, 1.0 / MOBA_BLOCK, 0.0).astype(k_ref.dtype)
        for g in range(hg):
            kmean_scr[g] = jnp.dot(avg, k_ref[0, :, g * LANE:(g + 1) * LANE],
                                   preferred_element_type=F32).astype(kmean_scr.dtype)

    qts = [_key_major(q_ref[0, :, g * LANE:(g + 1) * LANE]) for g in range(hg)]
    kblk = lax.broadcasted_iota(jnp.int32, (nbp, blk), 0)
    own = (i * blk + lax.broadcasted_iota(jnp.int32, (nbp, blk), 1)) >> shift
    kblk_f = kblk.astype(F32)
    for g, st in enumerate(heads):
        _attn_init(*st)
        gate = jnp.where(kblk < own, jnp.dot(kmean_scr[g], qts[g], preferred_element_type=F32), _NEG)
        pick = jnp.zeros((nbp, blk), F32)
        for _ in range(MOBA_TOPK):
            best = jnp.max(gate, axis=0, keepdims=True)
            hit = (gate == best) & (best > 0.5 * _NEG)
            first = jnp.min(jnp.where(hit, kblk_f, float(nbp)), axis=0, keepdims=True)
            new = kblk_f == first
            pick = jnp.where(new, 1.0, pick)
            gate = jnp.where(new, _NEG, gate)
        pick_scr[g] = jnp.where((pick > 0.5) | (kblk == own), 0.0, _NEG)

    def scores(j, slot):
        for g in range(hg):
            for u in range(nper):
                rows = pl.ds(pl.multiple_of(j * blk + u * MOBA_BLOCK, MOBA_BLOCK), MOBA_BLOCK)
                s = jnp.dot(k_ref[0, rows, g * LANE:(g + 1) * LANE], qts[g], preferred_element_type=F32)
                s_scr[slot, g, u * MOBA_BLOCK:(u + 1) * MOBA_BLOCK, :] = s + pick_scr[g, pl.ds(j * nper + u, 1), :]

    def consume(j, slot, masked):
        rows = _tile_rows(j, blk)
        for g, st in enumerate(heads):
            s = s_scr[slot, g]
            _attn_step(_causal(s) if masked else s, v_ref[0, rows, g * LANE:(g + 1) * LANE], *st)

    _pipelined_key_tiles(i, scores, consume)
    for g, (_, l_scr, acc_scr) in enumerate(heads):
        o_ref[0, :, g * LANE:(g + 1) * LANE] = _attn_out(l_scr, acc_scr, o_ref.dtype)


def _moba_attention(qk, v, *, blk, q_off, k_off, v_off, hg=_HEADS_PER_STEP):
    b, s, _ = qk.shape
    h = MOBA_HEADS
    assert s % MOBA_BLOCK == 0 and blk % MOBA_BLOCK == 0
    nbp = -(-(s // MOBA_BLOCK) // SUBLANE) * SUBLANE
    return pl.pallas_call(
        functools.partial(_moba_body, blk=blk, hg=hg, seq=s, nbp=nbp),
        grid=(b, h // hg, s // blk),
        in_specs=[pl.BlockSpec((1, blk, hg * LANE), lambda b_, h_, i: (b_, i, q_off // hg + h_)),
                  pl.BlockSpec((1, s, hg * LANE), lambda b_, h_, i: (b_, 0, k_off // hg + h_)),
                  pl.BlockSpec((1, s, hg * LANE), lambda b_, h_, i: (b_, 0, v_off // hg + h_))],
        out_specs=pl.BlockSpec((1, blk, hg * LANE), lambda b_, h_, i: (b_, i, h_)),
        out_shape=jax.ShapeDtypeStruct((b, s, h * HEAD_DIM), _CD),
        scratch_shapes=[pltpu.VMEM((hg, nbp, HEAD_DIM), _CD),
                        pltpu.VMEM((hg, nbp, blk), F32)] + _attn_scratch(blk, hg),
        compiler_params=_params("parallel", "parallel", "arbitrary"),
        name="moba_attention",
    )(qk, qk, v)


def _dilated_body(q_ref, k_ref, v_ref, bias_ref, o_ref, *scr, blk, hg, nrel):
    i = pl.program_id(2)
    s_scr, heads = _head_state(scr, hg)
    for st in heads:
        _attn_init(*st)
    qts = [_key_major(q_ref[0, :, g * LANE:(g + 1) * LANE]) for g in range(hg)]

    def scores(j, slot):
        rows = _tile_rows(j, blk)
        bias = bias_ref[i - j]
        for g in range(hg):
            s_scr[slot, g] = bias + jnp.dot(k_ref[0, rows, g * LANE:(g + 1) * LANE], qts[g],
                                            preferred_element_type=F32)

    def consume(j, slot, masked):
        rows = _tile_rows(j, blk)
        for g, st in enumerate(heads):
            _attn_step(s_scr[slot, g], v_ref[0, rows, g * LANE:(g + 1) * LANE], *st)

    _pipelined_key_tiles(i, scores, consume, first=jnp.maximum(i - (nrel - 1), 0))
    for g, (_, l_scr, acc_scr) in enumerate(heads):
        o_ref[0, :, g * LANE:(g + 1) * LANE] = _attn_out(l_scr, acc_scr, o_ref.dtype)


def _dilated_bias(blk):
    reach = max(w for w, _ in DIL_PATTERNS)
    nrel = -(-reach // blk) + 1
    rel = jnp.arange(nrel, dtype=jnp.int32)[:, None, None]
    k = jnp.arange(blk, dtype=jnp.int32)[None, :, None]
    q = jnp.arange(blk, dtype=jnp.int32)[None, None, :]
    d = rel * blk + q - k
    mult = jnp.zeros(d.shape, F32)
    for window, dil in DIL_PATTERNS:
        mult = mult + ((d >= 0) & (d <= (window // dil) * dil) & (d % dil == 0)).astype(F32)
    return jnp.where(mult > 0, jnp.log2(jnp.maximum(mult, 1.0)), _NEG), nrel


def _dilated_attention(qk, v, *, blk, q_off, k_off, v_off, hg=_HEADS_PER_STEP):
    b, s, _ = qk.shape
    h = DIL_HEADS
    bias, nrel = _dilated_bias(blk)
    return pl.pallas_call(
        functools.partial(_dilated_body, blk=blk, hg=hg, nrel=nrel),
        grid=(b, h // hg, s // blk),
        in_specs=[pl.BlockSpec((1, blk, hg * LANE), lambda b_, h_, i: (b_, i, q_off // hg + h_)),
                  pl.BlockSpec((1, s, hg * LANE), lambda b_, h_, i: (b_, 0, k_off // hg + h_)),
                  pl.BlockSpec((1, s, hg * LANE), lambda b_, h_, i: (b_, 0, v_off // hg + h_)),
                  pl.BlockSpec((nrel, blk, blk), lambda b_, h_, i: (0, 0, 0))],
        out_specs=pl.BlockSpec((1, blk, hg * LANE), lambda b_, h_, i: (b_, i, h_)),
        out_shape=jax.ShapeDtypeStruct((b, s, h * HEAD_DIM), _CD),
        scratch_shapes=_attn_scratch(blk, hg),
        compiler_params=_params("parallel", "parallel", "arbitrary"),
        name="dilated_attention",
    )(qk, qk, v, bias)


_E_CQ, _E_CKV, _E_KPE, _E_Q, _E_K, _E_V, _E_QI, _E_KI, _E_WI = [
    int(o) for o in np.cumsum([0, MLA_Q_RANK, MLA_KV_RANK, MLA_ROPE, DSA_HEADS * HEAD_DIM, DSA_HEADS * HEAD_DIM,
                               DSA_HEADS * HEAD_DIM, IDX_HEADS * IDX_DIM, IDX_DIM])]
_R64_KIA, _R64_KIB, _R64_KPE, _R64_SLABS = 8, 9, 10, 12


def _take_cols(w, idx):
    idx = np.asarray(idx)
    cols = jnp.take(w, jnp.asarray(np.maximum(idx, 0)), axis=1)
    return jnp.where(jnp.asarray(idx >= 0)[None, :], cols, 0.0).astype(_CD)


def _r64_columns():
    half = IDX_DIM // 2
    a = np.arange(half)
    z = -np.ones(half, np.int64)
    cols = []
    for p in range(IDX_HEADS // 2):
        ha, hb = _E_QI + 2 * p * IDX_DIM, _E_QI + (2 * p + 1) * IDX_DIM
        cols += [ha + a, hb + a, ha + half + a, hb + half + a]
    cols += [_E_KI + a, z, _E_KI + half + a, z]
    cols += [z, _E_KI + a, z, _E_KI + half + a]
    cols += [_E_KPE + a, z, _E_KPE + half + a, z]
    cols += [z, z, z, z]
    return np.concatenate(cols)


def _uq_columns():
    half = MLA_ROPE // 2
    a = np.arange(half)
    z = -np.ones(half, np.int64)
    cols = []
    for h in range(MLA_HEADS):
        o = h * (MLA_NOPE + MLA_ROPE)
        cols += [o + np.arange(MLA_NOPE), o + MLA_NOPE + a, z, o + MLA_NOPE + half + a, z]
    return np.concatenate(cols)


def _ukv_columns():
    per = MLA_NOPE + MLA_V
    kn = [h * per + np.arange(MLA_NOPE) for h in range(MLA_HEADS)]
    vv = [h * per + MLA_NOPE + np.arange(MLA_V) for h in range(MLA_HEADS)]
    return np.concatenate(kn + vv)


def _rope_tables(seq, dim, scales, with_identity=False):
    inv = ROPE_THETA ** (-jnp.arange(0, dim, 2, dtype=F32) / dim)
    ang = jnp.arange(seq, dtype=F32)[:, None] * inv[None, :]
    reps = (LANE // 2) // (dim // 2)
    cos = jnp.tile(jnp.cos(ang), (1, 2 * reps))
    sin = jnp.tile(jnp.sin(ang), (1, reps))
    sin = jnp.concatenate([-sin, sin], axis=1)
    sc = jnp.asarray(scales, F32)[:, None, None]
    cos, sin = cos[None] * sc, sin[None] * sc
    if with_identity:
        cos = jnp.concatenate([cos, jnp.ones((1, seq, LANE), F32)], axis=0)
        sin = jnp.concatenate([sin, jnp.zeros((1, seq, LANE), F32)], axis=0)
    return cos, sin


def _mlp_block(x2, g, w1, w2, *, tm):
    up = _norm_matmul(x2, g, _to_mxu_dtype(*w1), tm=tm, tn=1024, out_dtype=_CD, epilogue=_ep_relu2, name="mlp_up")
    res = pl.BlockSpec((tm, 1024), lambda i, j, k: (i, j))
    return _matmul(up, _to_mxu_dtype(*w2), tm=tm, tn=1024, tk=2048, out_dtype=F32,
                   epilogue=_ep_residual, extra=(x2,), extra_specs=(res,), name="mlp_down")


def _out_proj_body(a_ref, b_ref, w_ref, r_ref, o_ref):
    half = a_ref.shape[1]
    y = jnp.dot(a_ref[...], w_ref[:half, :], preferred_element_type=F32)
    y = y + jnp.dot(b_ref[...], w_ref[half:, :], preferred_element_type=F32)
    o_ref[...] = y + r_ref[...]


def _out_proj(a, b, w_out, x2, *, tm, tn=1024):
    t, half = a.shape
    d = x2.shape[1]
    return pl.pallas_call(
        _out_proj_body,
        grid=(t // tm, d // tn),
        in_specs=[pl.BlockSpec((tm, half), lambda i, j: (i, 0)),
                  pl.BlockSpec((tm, half), lambda i, j: (i, 0)),
                  pl.BlockSpec((2 * half, tn), lambda i, j: (0, j)),
                  pl.BlockSpec((tm, tn), lambda i, j: (i, j))],
        out_specs=pl.BlockSpec((tm, tn), lambda i, j: (i, j)),
        out_shape=jax.ShapeDtypeStruct((t, d), F32),
        compiler_params=_params("parallel", "parallel"),
        name="out_proj",
    )(a, b, _to_mxu_dtype(*w_out), x2)


def _even_mixer(x2, g_mix, w_in, g_q, g_kv, w_uq, w_ukv, w_out, *, batch, seq, blk):
    t, d = x2.shape
    tm = min(2 * blk, seq)
    hm = _rmsnorm(x2, g_mix, _CD)
    mla_scale = _LOG2E * (MLA_NOPE + MLA_ROPE) ** -0.5
    rope64 = _rope_tables(seq, IDX_DIM, (1.0, mla_scale))
    rope128 = _rope_tables(seq, HEAD_DIM, (_LOG2E * HEAD_DIM ** -0.5, 1.0), with_identity=True)
    nh = DSA_HEADS * HEAD_DIM

    r64 = _rope_matmul(hm, _take_cols(w_in, _r64_columns()), rope64, lambda j: 0, seq=seq, tm=tm,
                       tn=_R64_SLABS // 2 * LANE, pattern=(True,) * (_R64_SLABS // 2), name="in_proj_rope64")
    qkv = _rope_matmul(hm, w_in[:, _E_Q:_E_QI].astype(_CD), rope128, lambda j: j, seq=seq, tm=tm,
                       tn=nh, pattern=(True,) * DSA_HEADS, name="in_proj_qkv")
    w_idx = _matmul(hm, _take_cols(w_in, np.concatenate([_E_WI + np.arange(IDX_HEADS),
                                                          -np.ones(LANE - IDX_HEADS, np.int64)])),
                    tm=tm, tn=LANE, tk=d, out_dtype=F32,
                    epilogue=_ep_scale(IDX_HEADS ** -0.5 * IDX_DIM ** -0.5), name="in_proj_widx")
    gspec = lambda n: pl.BlockSpec((1, n), lambda i, j, k: (0, 0))
    c_q = _matmul(hm, w_in[:, _E_CQ:_E_CKV].astype(_CD), tm=tm, tn=MLA_Q_RANK, tk=d, out_dtype=_CD,
                  epilogue=_ep_rmsnorm, extra=(g_q.reshape(1, -1),), extra_specs=(gspec(MLA_Q_RANK),),
                  name="in_proj_cq")
    c_kv = _matmul(hm, w_in[:, _E_CKV:_E_KPE].astype(_CD), tm=tm, tn=MLA_KV_RANK, tk=d, out_dtype=_CD,
                   epilogue=_ep_rmsnorm, extra=(g_kv.reshape(1, -1),), extra_specs=(gspec(MLA_KV_RANK),),
                   name="in_proj_ckv")

    q_mla = _rope_matmul(c_q, _take_cols(w_uq, _uq_columns()), rope64, lambda j: 1, seq=seq, tm=tm, tn=1024,
                         pattern=(False, True) * 4, plain_scale=mla_scale, name="mla_q_up")
    kv_mla = _matmul(c_kv, _take_cols(w_ukv, _ukv_columns()), tm=tm, tn=1024, tk=MLA_KV_RANK, out_dtype=_CD,
                     epilogue=_ep_scale(1.0), name="mla_kv_up")

    sh = lambda z: z.reshape(batch, seq, z.shape[-1])
    a = _mla_attention(sh(q_mla), sh(kv_mla), sh(r64), blk=blk)
    bsa = _dsa_attention(sh(r64), sh(w_idx), sh(qkv), blk=blk)
    return _out_proj(a.reshape(t, -1), bsa.reshape(t, -1), w_out, x2, tm=tm)


def _odd_mixer(x2, g_mix, w_in, w_out, *, batch, seq, blk):
    t, d = x2.shape
    tm = min(2 * blk, seq)
    nh = MOBA_HEADS * HEAD_DIM
    rope128 = _rope_tables(seq, HEAD_DIM, (_LOG2E * HEAD_DIM ** -0.5, 1.0), with_identity=True)
    qkv = _rope_matmul(x2, _to_mxu_dtype(*w_in), rope128, lambda j: j % 3, seq=seq, tm=tm, tn=nh,
                       pattern=(True,) * MOBA_HEADS, norm_gain=g_mix, name="in_proj_odd")
    qkv = qkv.reshape(batch, seq, qkv.shape[-1])
    hs = MOBA_HEADS
    c = _moba_attention(qkv, qkv, blk=blk, q_off=0, k_off=hs, v_off=2 * hs)
    dl = _dilated_attention(qkv, qkv, blk=blk, q_off=3 * hs, k_off=4 * hs, v_off=5 * hs)
    return _out_proj(c.reshape(t, -1), dl.reshape(t, -1), w_out, x2, tm=tm)


def kernel(x, ln_mix, ln_mlp, ln_final, e_w_in, e_g_q, e_g_kv, e_w_uq, e_w_ukv, e_w_out,
           o_w_in, o_w_out, mlp_w1, mlp_w2):
    batch, seq, d = x.shape
    blk = min(512, seq)
    assert seq % blk == 0 and blk % MOBA_BLOCK == 0
    x2 = x.reshape(batch * seq, d)
    depth = ln_mix.shape[0]
    for layer in range(depth):
        j = layer // 2
        if layer % 2 == 0:
            x2 = _even_mixer(x2, ln_mix[layer], e_w_in[j], e_g_q[j], e_g_kv[j], e_w_uq[j], e_w_ukv[j],
                             (e_w_out, j), batch=batch, seq=seq, blk=blk)
        else:
            x2 = _odd_mixer(x2, ln_mix[layer], (o_w_in, j), (o_w_out, j), batch=batch, seq=seq, blk=blk)
        x2 = _mlp_block(x2, ln_mlp[layer], (mlp_w1, layer), (mlp_w2, layer), tm=min(2 * blk, batch * seq))
    return _rmsnorm(x2, ln_final, x.dtype).reshape(batch, seq, d)
```

```python
import functools

import numpy as np
import jax
import jax.numpy as jnp
from jax import lax
from jax.experimental import pallas as pl
from jax.experimental.pallas import tpu as pltpu

HEAD_DIM = 128
ROPE_THETA = 10000.0
NORM_EPS = 1e-6
MLA_HEADS, MLA_Q_RANK, MLA_KV_RANK, MLA_NOPE, MLA_ROPE, MLA_V = 8, 512, 256, 128, 64, 128
DSA_HEADS, IDX_HEADS, IDX_DIM, DSA_TOPK_MAX = 8, 16, 64, 256
MOBA_HEADS, MOBA_BLOCK, MOBA_TOPK = 8, 256, 3
DIL_HEADS = 8
DIL_PATTERNS = ((128, 1), (512, 4), (2048, 16))

LANE = 128
SUBLANE = 8
VMEM_LIMIT_BYTES = 56 * 2**20

F32 = jnp.float32
_CD = jnp.bfloat16
_NEG = -1e30
_LOG2E = 1.4426950408889634
_INT_MAX = 2**31 - 1
_HEADS_PER_STEP = 2
_INTERP_PROBES = 40


def _params(*sem):
    return pltpu.CompilerParams(dimension_semantics=sem, vmem_limit_bytes=VMEM_LIMIT_BYTES)


def _rmsnorm_body(x_ref, g_ref, o_ref):
    x = x_ref[...].astype(F32)
    y = x * lax.rsqrt(jnp.mean(x * x, axis=-1, keepdims=True) + NORM_EPS)
    o_ref[...] = (y * g_ref[...]).astype(o_ref.dtype)


def _rmsnorm(x, g, out_dtype, tm=512):
    m, d = x.shape
    return pl.pallas_call(
        _rmsnorm_body,
        grid=(m // tm,),
        in_specs=[pl.BlockSpec((tm, d), lambda i: (i, 0)), pl.BlockSpec((1, d), lambda i: (0, 0))],
        out_specs=pl.BlockSpec((tm, d), lambda i: (i, 0)),
        out_shape=jax.ShapeDtypeStruct((m, d), out_dtype),
        compiler_params=_params("parallel"),
        name="rmsnorm",
    )(x, g.reshape(1, d).astype(F32))


def _cast_body(x_ref, o_ref):
    o_ref[...] = x_ref[...].astype(o_ref.dtype)


def _to_mxu_dtype(w, layer=None, tm=256):
    m, n = w.shape[-2:]
    if layer is None:
        spec = pl.BlockSpec((tm, n), lambda i: (i, 0))
    else:
        spec = pl.BlockSpec((None, tm, n), lambda i: (layer, i, 0))
    return pl.pallas_call(
        _cast_body,
        grid=(m // tm,),
        in_specs=[spec],
        out_specs=pl.BlockSpec((tm, n), lambda i: (i, 0)),
        out_shape=jax.ShapeDtypeStruct((m, n), _CD),
        compiler_params=_params("parallel"),
        name="weight_cast",
    )(w)


def _rope_slab(y, cos, sin):
    return y * cos + pltpu.roll(y, LANE // 2, 1) * sin


def _matmul_body(*refs, n_extra, epilogue, out_dtype):
    a_ref, w_ref = refs[0], refs[1]
    extra = refs[2:2 + n_extra]
    o_ref, acc_ref = refs[2 + n_extra], refs[3 + n_extra]
    k = pl.program_id(2)

    @pl.when(k == 0)
    def _init():
        acc_ref[...] = jnp.zeros_like(acc_ref)

    acc_ref[...] += jnp.dot(a_ref[...], w_ref[...], preferred_element_type=F32)

    @pl.when(k == pl.num_programs(2) - 1)
    def _finish():
        o_ref[...] = epilogue(acc_ref[...], *extra).astype(out_dtype)


def _matmul(a, w, *, tm, tn, tk, out_dtype, epilogue, extra=(), extra_specs=(), name):
    m, kd = a.shape
    n = w.shape[1]
    assert m % tm == 0 and n % tn == 0 and kd % tk == 0, (a.shape, w.shape, tm, tn, tk)
    body = functools.partial(_matmul_body, n_extra=len(extra), epilogue=epilogue, out_dtype=out_dtype)
    return pl.pallas_call(
        body,
        grid=(m // tm, n // tn, kd // tk),
        in_specs=[pl.BlockSpec((tm, tk), lambda i, j, k: (i, k)),
                  pl.BlockSpec((tk, tn), lambda i, j, k: (k, j))] + list(extra_specs),
        out_specs=pl.BlockSpec((tm, tn), lambda i, j, k: (i, j)),
        out_shape=jax.ShapeDtypeStruct((m, n), out_dtype),
        scratch_shapes=[pltpu.VMEM((tm, tn), F32)],
        compiler_params=_params("parallel", "parallel", "arbitrary"),
        name=name,
    )(a, w, *extra)


def _norm_matmul_body(*refs, n_extra, epilogue, out_dtype):
    x_ref, g_ref, w_ref = refs[:3]
    extra = refs[3:3 + n_extra]
    o_ref, h_scr = refs[3 + n_extra], refs[4 + n_extra]

    @pl.when(pl.program_id(1) == 0)
    def _norm():
        x = x_ref[...]
        y = x * lax.rsqrt(jnp.mean(x * x, axis=-1, keepdims=True) + NORM_EPS)
        h_scr[...] = (y * g_ref[...]).astype(h_scr.dtype)

    y = jnp.dot(h_scr[...], w_ref[...], preferred_element_type=F32)
    o_ref[...] = epilogue(y, *extra).astype(out_dtype)


def _norm_matmul(x, g, w, *, tm, tn, out_dtype, epilogue, extra=(), extra_specs=(), name):
    m, d = x.shape
    n = w.shape[1]
    assert m % tm == 0 and n % tn == 0, (x.shape, w.shape, tm, tn)
    body = functools.partial(_norm_matmul_body, n_extra=len(extra), epilogue=epilogue, out_dtype=out_dtype)
    return pl.pallas_call(
        body,
        grid=(m // tm, n // tn),
        in_specs=[pl.BlockSpec((tm, d), lambda i, j: (i, 0)),
                  pl.BlockSpec((1, d), lambda i, j: (0, 0)),
                  pl.BlockSpec((d, tn), lambda i, j: (0, j))] + list(extra_specs),
        out_specs=pl.BlockSpec((tm, tn), lambda i, j: (i, j)),
        out_shape=jax.ShapeDtypeStruct((m, n), out_dtype),
        scratch_shapes=[pltpu.VMEM((tm, d), _CD)],
        compiler_params=_params("parallel", "arbitrary"),
        name=name,
    )(x, g.reshape(1, d).astype(F32), w, *extra)


def _ep_scale(scale):
    def ep(y):
        return y if scale == 1.0 else y * scale
    return ep


def _ep_relu2(y):
    return jnp.square(jnp.maximum(y, 0.0))


def _ep_residual(y, r_ref):
    return y + r_ref[...]


def _ep_rmsnorm(y, g_ref):
    return y * lax.rsqrt(jnp.mean(y * y, axis=-1, keepdims=True) + NORM_EPS) * g_ref[...]


def _ep_rope(pattern, plain_scale):
    def ep(y, cos_ref, sin_ref):
        cos, sin = cos_ref[0], sin_ref[0]
        out = []
        for c, rot in enumerate(pattern):
            slab = y[:, c * LANE:(c + 1) * LANE]
            out.append(_rope_slab(slab, cos, sin) if rot else slab * plain_scale)
        return jnp.concatenate(out, axis=1)
    return ep


def _rope_matmul(a, w, tabs, tab_of_tile, *, seq, tm, tn, pattern, plain_scale=1.0, norm_gain=None, name):
    nblk = seq // tm
    spec = pl.BlockSpec((1, tm, LANE), lambda i, j, *_: (tab_of_tile(j), i % nblk, 0))
    common = dict(tm=tm, tn=tn, out_dtype=_CD, epilogue=_ep_rope(pattern, plain_scale), extra=tabs,
                  extra_specs=(spec, spec), name=name)
    if norm_gain is None:
        return _matmul(a, w, tk=a.shape[1], **common)
    return _norm_matmul(a, norm_gain, w, **common)


def _attn_init(m_scr, l_scr, acc_scr):
    m_scr[...] = jnp.full(m_scr.shape, _NEG, F32)
    l_scr[...] = jnp.zeros(l_scr.shape, F32)
    acc_scr[...] = jnp.zeros(acc_scr.shape, F32)


def _tile_rows(j, blk):
    return pl.ds(pl.multiple_of(j * blk, blk), blk)


def _dot_tn(a, b):
    return lax.dot_general(a, b, (((0,), (0,)), ((), ())), preferred_element_type=F32)


def _key_major(q):
    return q.astype(F32).T.astype(q.dtype)


def _attn_scratch(blk, hg):
    per_head = [pltpu.VMEM((1, blk), F32), pltpu.VMEM((1, blk), F32), pltpu.VMEM((HEAD_DIM, blk), F32)]
    return [pltpu.VMEM((2, hg, blk, blk), F32)] + per_head * hg


def _head_state(scr, hg):
    return scr[0], [scr[1 + 3 * g:4 + 3 * g] for g in range(hg)]


def _attn_step(st, v, m_scr, l_scr, acc_scr):
    m_prev = m_scr[...]
    m_new = jnp.maximum(m_prev, jnp.max(st, axis=0, keepdims=True))
    alpha = jnp.exp2(m_prev - m_new)
    p = jnp.exp2(st - m_new)
    l_scr[...] = alpha * l_scr[...] + jnp.sum(p, axis=0, keepdims=True)
    acc_scr[...] = alpha * acc_scr[...] + _dot_tn(v, p.astype(v.dtype))
    m_scr[...] = m_new


def _attn_out(l_scr, acc_scr, dtype):
    return (acc_scr[...] / l_scr[...]).T.astype(dtype)


def _pipelined_key_tiles(last, scores, consume, first=0):
    n_past = last - first

    def pair(t, carry):
        j = first + 2 * t
        scores(j + 1, 1)
        consume(j, 0, False)
        scores(j + 2, 0)
        consume(j + 1, 1, False)
        return carry

    scores(first, 0)
    lax.fori_loop(0, n_past // 2, pair, 0)

    @pl.when(n_past % 2 == 1)
    def _odd():
        scores(last, 1)
        consume(last - 1, 0, False)
        consume(last, 1, True)

    @pl.when(n_past % 2 == 0)
    def _even():
        consume(last, 0, True)


def _causal(s):
    kpos = lax.broadcasted_iota(jnp.int32, s.shape, 0)
    qpos = lax.broadcasted_iota(jnp.int32, s.shape, 1)
    return jnp.where(kpos <= qpos, s, _NEG)


def _mla_body(q_ref, kn_ref, kp_ref, v_ref, o_ref, *scr, blk, hg):
    i = pl.program_id(2)
    s_scr, heads = _head_state(scr, hg)
    for st in heads:
        _attn_init(*st)
    qts = [_key_major(q_ref[0, :, g * 2 * LANE:(g + 1) * 2 * LANE]) for g in range(hg)]

    def scores(j, slot):
        rows = _tile_rows(j, blk)
        kp = kp_ref[0, rows, :]
        for g in range(hg):
            k = jnp.concatenate([kn_ref[0, rows, g * LANE:(g + 1) * LANE], kp], axis=1)
            s_scr[slot, g] = jnp.dot(k, qts[g], preferred_element_type=F32)

    def consume(j, slot, masked):
        rows = _tile_rows(j, blk)
        for g, st in enumerate(heads):
            s = s_scr[slot, g]
            _attn_step(_causal(s) if masked else s, v_ref[0, rows, g * LANE:(g + 1) * LANE], *st)

    _pipelined_key_tiles(i, scores, consume)
    for g, (_, l_scr, acc_scr) in enumerate(heads):
        o_ref[0, :, g * LANE:(g + 1) * LANE] = _attn_out(l_scr, acc_scr, o_ref.dtype)


def _mla_attention(q, kv, r64, *, blk, hg=_HEADS_PER_STEP):
    b, s, _ = q.shape
    h = MLA_HEADS
    return pl.pallas_call(
        functools.partial(_mla_body, blk=blk, hg=hg),
        grid=(b, h // hg, s // blk),
        in_specs=[pl.BlockSpec((1, blk, hg * 2 * LANE), lambda b_, h_, i: (b_, i, h_)),
                  pl.BlockSpec((1, s, hg * LANE), lambda b_, h_, i: (b_, 0, h_)),
                  pl.BlockSpec((1, s, LANE), lambda b_, h_, i: (b_, 0, _R64_KPE)),
                  pl.BlockSpec((1, s, hg * LANE), lambda b_, h_, i: (b_, 0, h // hg + h_))],
        out_specs=pl.BlockSpec((1, blk, hg * LANE), lambda b_, h_, i: (b_, i, h_)),
        out_shape=jax.ShapeDtypeStruct((b, s, h * MLA_V), _CD),
        scratch_shapes=_attn_scratch(blk, hg),
        compiler_params=_params("parallel", "parallel", "arbitrary"),
        name="mla_attention",
    )(q, kv, r64, kv)


def _order_key(x):
    bits = lax.bitcast_convert_type(x, jnp.int32)
    return bits ^ ((bits >> 31) & _INT_MAX)


def _key_score(key):
    return lax.bitcast_convert_type(key ^ ((key >> 31) & _INT_MAX), F32)


def _dsa_select(i, qi_ref, ka_ref, kb_ref, w_ref, sel_scr, jc_scr, *, blk, topk, seq):
    wt = w_ref[0].T
    qits = [_key_major(qi_ref[0, :, p * LANE:(p + 1) * LANE]) for p in range(IDX_HEADS // 2)]
    kloc = lax.broadcasted_iota(jnp.int32, (blk, blk), 0)
    qpos = i * blk + lax.broadcasted_iota(jnp.int32, (blk, blk), 1)

    def score_tile(c, carry):
        rows = _tile_rows(c, blk)
        ka, kb = ka_ref[0, rows, :], kb_ref[0, rows, :]
        acc = jnp.zeros((blk, blk), F32)
        for p in range(IDX_HEADS // 2):
            da = jnp.dot(ka, qits[p], preferred_element_type=F32)
            db = jnp.dot(kb, qits[p], preferred_element_type=F32)
            acc = acc + wt[2 * p:2 * p + 1, :] * jnp.maximum(da, 0.0) + wt[2 * p + 1:2 * p + 2, :] * jnp.maximum(db, 0.0)
        causal = c * blk + kloc <= qpos
        sel_scr[c] = jnp.where(causal, acc, -jnp.inf)
        lo8, hi8 = carry
        fold = lambda x: x.reshape(blk // SUBLANE, SUBLANE, blk)
        lo8 = jnp.minimum(lo8, jnp.min(fold(jnp.where(causal, acc, jnp.inf)), axis=0))
        hi8 = jnp.maximum(hi8, jnp.max(fold(jnp.where(causal, acc, -jnp.inf)), axis=0))
        return lo8, hi8

    lo8, hi8 = lax.fori_loop(0, i + 1, score_tile, (jnp.full((SUBLANE, blk), jnp.inf, F32),
                                                    jnp.full((SUBLANE, blk), -jnp.inf, F32)))
    s_min, s_max = lo8[0:1], hi8[0:1]
    for r in range(1, SUBLANE):
        s_min = jnp.minimum(s_min, lo8[r:r + 1])
        s_max = jnp.maximum(s_max, hi8[r:r + 1])

    n_causal = i * blk + lax.broadcasted_iota(jnp.int32, (1, blk), 1) + 1
    kk = jnp.minimum(topk, n_causal).astype(F32)

    def count(pred):
        def body(c, cnt):
            hit = jnp.where(pred(sel_scr[c], c * blk + kloc), 1.0, 0.0)
            return cnt + jnp.sum(hit.reshape(blk // SUBLANE, SUBLANE, blk), axis=0)
        cnt = lax.fori_loop(0, i + 1, body, jnp.zeros((SUBLANE, blk), F32))
        return jnp.sum(cnt, axis=0, keepdims=True)

    def unfinished(lo, hi, n_lo):
        return (n_lo > kk) & (hi > lo + 1)

    def probe(state):
        lo, hi, n_lo, n_hi, w_lo, w_hi, last, it, _ = state
        lo_f, hi_f = _key_score(lo), _key_score(hi)
        a, b = (n_lo - kk + 0.5) * w_lo, (kk - 0.5 - n_hi) * w_hi
        interp = _order_key(lo_f + (hi_f - lo_f) * (a / (a + b)))
        mid = (lo >> 1) + (hi >> 1) + (lo & hi & 1)
        cand = jnp.clip(jnp.where(it < _INTERP_PROBES, interp, mid), lo + 1, hi - 1)
        cand_f = _key_score(cand)
        n = count(lambda s, kpos: s >= cand_f)
        live = unfinished(lo, hi, n_lo)
        up = live & (n >= kk)
        down = live & (n < kk)
        lo, n_lo = jnp.where(up, cand, lo), jnp.where(up, n, n_lo)
        hi, n_hi = jnp.where(down, cand, hi), jnp.where(down, n, n_hi)
        w_hi = jnp.where(up, jnp.where(last > 0.5, 0.5 * w_hi, 1.0), jnp.where(down, 1.0, w_hi))
        w_lo = jnp.where(down, jnp.where(last < -0.5, 0.5 * w_lo, 1.0), jnp.where(up, 1.0, w_lo))
        last = jnp.where(up, 1.0, jnp.where(down, -1.0, last))
        more = jnp.max(jnp.where(unfinished(lo, hi, n_lo), 1.0, 0.0))
        return lo, hi, n_lo, n_hi, w_lo, w_hi, last, it + 1, more

    n_all = n_causal.astype(F32)
    one, zero = jnp.ones((1, blk), F32), jnp.zeros((1, blk), F32)
    key_min, key_end = _order_key(s_min), _order_key(s_max) + 1
    start = (key_min, key_end, n_all, zero, one, one, zero, jnp.int32(0),
             jnp.max(jnp.where(unfinished(key_min, key_end, n_all), 1.0, 0.0)))
    t_key, _, n_ge = lax.while_loop(lambda s: s[8] > 0.5, probe, start)[:3]
    t = _key_score(t_key)
    jc_scr[...] = jnp.full((1, blk), seq, jnp.int32)

    @pl.when(jnp.max(n_ge - kk) > 0.5)
    def _ties():
        need = kk - count(lambda s, kpos: s > t)
        nbits = (seq - 1).bit_length()

        def pos_step(b, jc):
            cand = jc + (jnp.int32(1) << (nbits - 1 - b))
            n_lt = count(lambda s, kpos: (s == t) & (kpos < cand))
            return jnp.where(n_lt < need, cand, jc)

        jc_scr[...] = lax.fori_loop(0, nbits, pos_step, jnp.zeros((1, blk), jnp.int32))

    jc = jc_scr[...]

    def to_bias(c, carry):
        s = sel_scr[c]
        chosen = (s > t) | ((s == t) & (c * blk + kloc <= jc))
        sel_scr[c] = jnp.where(chosen, 0.0, _NEG).astype(F32)
        return carry

    lax.fori_loop(0, i + 1, to_bias, 0)


def _dsa_body(qi_ref, ka_ref, kb_ref, w_ref, q_ref, k_ref, v_ref, o_ref, sel_scr, jc_scr, *scr,
              blk, hg, topk, seq):
    i = pl.program_id(1)

    @pl.when(pl.program_id(2) == 0)
    def _select():
        _dsa_select(i, qi_ref, ka_ref, kb_ref, w_ref, sel_scr, jc_scr, blk=blk, topk=topk, seq=seq)

    s_scr, heads = _head_state(scr, hg)
    for st in heads:
        _attn_init(*st)
    qts = [_key_major(q_ref[0, :, g * LANE:(g + 1) * LANE]) for g in range(hg)]

    def scores(j, slot):
        rows = _tile_rows(j, blk)
        bias = sel_scr[j]
        for g in range(hg):
            s_scr[slot, g] = bias + jnp.dot(k_ref[0, rows, g * LANE:(g + 1) * LANE], qts[g],
                                            preferred_element_type=F32)

    def consume(j, slot, masked):
        rows = _tile_rows(j, blk)
        for g, st in enumerate(heads):
            _attn_step(s_scr[slot, g], v_ref[0, rows, g * LANE:(g + 1) * LANE], *st)

    _pipelined_key_tiles(i, scores, consume)
    for g, (_, l_scr, acc_scr) in enumerate(heads):
        o_ref[0, :, g * LANE:(g + 1) * LANE] = _attn_out(l_scr, acc_scr, o_ref.dtype)


def _dsa_attention(r64, w_idx, qkv, *, blk, hg=_HEADS_PER_STEP):
    b, s, _ = qkv.shape
    h = DSA_HEADS
    topk = min(DSA_TOPK_MAX, s // 4)
    return pl.pallas_call(
        functools.partial(_dsa_body, blk=blk, hg=hg, topk=topk, seq=s),
        grid=(b, s // blk, h // hg),
        in_specs=[pl.BlockSpec((1, blk, 8 * LANE), lambda b_, i, h_: (b_, i, 0)),
                  pl.BlockSpec((1, s, LANE), lambda b_, i, h_: (b_, 0, _R64_KIA)),
                  pl.BlockSpec((1, s, LANE), lambda b_, i, h_: (b_, 0, _R64_KIB)),
                  pl.BlockSpec((1, blk, LANE), lambda b_, i, h_: (b_, i, 0)),
                  pl.BlockSpec((1, blk, hg * LANE), lambda b_, i, h_: (b_, i, h_)),
                  pl.BlockSpec((1, s, hg * LANE), lambda b_, i, h_: (b_, 0, h // hg + h_)),
                  pl.BlockSpec((1, s, hg * LANE), lambda b_, i, h_: (b_, 0, 2 * (h // hg) + h_))],
        out_specs=pl.BlockSpec((1, blk, hg * LANE), lambda b_, i, h_: (b_, i, h_)),
        out_shape=jax.ShapeDtypeStruct((b, s, h * HEAD_DIM), _CD),
        scratch_shapes=[pltpu.VMEM((s // blk, blk, blk), F32),
                        pltpu.VMEM((1, blk), jnp.int32)] + _attn_scratch(blk, hg),
        compiler_params=_params("parallel", "arbitrary", "arbitrary"),
        name="dsa_attention",
    )(r64, r64, r64, w_idx, qkv, qkv, qkv)


def _moba_body(q_ref, k_ref, v_ref, o_ref, kmean_scr, pick_scr, *scr, blk, hg, seq, nbp):
    i = pl.program_id(2)
    nper = blk // MOBA_BLOCK
    shift = MOBA_BLOCK.bit_length() - 1
    s_scr, heads = _head_state(scr, hg)

    @pl.when(i == 0)
    def _block_means():
        r = lax.broadcasted_iota(jnp.int32, (nbp, seq), 0)
        c = lax.broadcasted_iota(jnp.int32, (nbp, seq), 1)
        avg = jnp.where((c >> shift) == r, 1.0 / MOBA_BLOCK, 0.0).astype(k_ref.dtype)
        for g in range(hg):
            kmean_scr[g] = jnp.dot(avg, k_ref[0, :, g * LANE:(g + 1) * LANE],
                                   preferred_element_type=F32).astype(kmean_scr.dtype)

    qts = [_key_major(q_ref[0, :, g * LANE:(g + 1) * LANE]) for g in range(hg)]
    kblk = lax.broadcasted_iota(jnp.int32, (nbp, blk), 0)
    own = (i * blk + lax.broadcasted_iota(jnp.int32, (nbp, blk), 1)) >> shift
    kblk_f = kblk.astype(F32)
    for g, st in enumerate(heads):
        _attn_init(*st)
        gate = jnp.where(kblk < own, jnp.dot(kmean_scr[g], qts[g], preferred_element_type=F32), _NEG)
        pick = jnp.zeros((nbp, blk), F32)
        for _ in range(MOBA_TOPK):
            best = jnp.max(gate, axis=0, keepdims=True)
            hit = (gate == best) & (best > 0.5 * _NEG)
            first = jnp.min(jnp.where(hit, kblk_f, float(nbp)), axis=0, keepdims=True)
            new = kblk_f == first
            pick = jnp.where(new, 1.0, pick)
            gate = jnp.where(new, _NEG, gate)
        pick_scr[g] = jnp.where((pick > 0.5) | (kblk == own), 0.0, _NEG)

    def scores(j, slot):
        for g in range(hg):
            for u in range(nper):
                rows = pl.ds(pl.multiple_of(j * blk + u * MOBA_BLOCK, MOBA_BLOCK), MOBA_BLOCK)
                s = jnp.dot(k_ref[0, rows, g * LANE:(g + 1) * LANE], qts[g], preferred_element_type=F32)
                s_scr[slot, g, u * MOBA_BLOCK:(u + 1) * MOBA_BLOCK, :] = s + pick_scr[g, pl.ds(j * nper + u, 1), :]

    def consume(j, slot, masked):
        rows = _tile_rows(j, blk)
        for g, st in enumerate(heads):
            s = s_scr[slot, g]
            _attn_step(_causal(s) if masked else s, v_ref[0, rows, g * LANE:(g + 1) * LANE], *st)

    _pipelined_key_tiles(i, scores, consume)
    for g, (_, l_scr, acc_scr) in enumerate(heads):
        o_ref[0, :, g * LANE:(g + 1) * LANE] = _attn_out(l_scr, acc_scr, o_ref.dtype)


def _moba_attention(qk, v, *, blk, q_off, k_off, v_off, hg=_HEADS_PER_STEP):
    b, s, _ = qk.shape
    h = MOBA_HEADS
    assert s % MOBA_BLOCK == 0 and blk % MOBA_BLOCK == 0
    nbp = -(-(s // MOBA_BLOCK) // SUBLANE) * SUBLANE
    return pl.pallas_call(
        functools.partial(_moba_body, blk=blk, hg=hg, seq=s, nbp=nbp),
        grid=(b, h // hg, s // blk),
        in_specs=[pl.BlockSpec((1, blk, hg * LANE), lambda b_, h_, i: (b_, i, q_off // hg + h_)),
                  pl.BlockSpec((1, s, hg * LANE), lambda b_, h_, i: (b_, 0, k_off // hg + h_)),
                  pl.BlockSpec((1, s, hg * LANE), lambda b_, h_, i: (b_, 0, v_off // hg + h_))],
        out_specs=pl.BlockSpec((1, blk, hg * LANE), lambda b_, h_, i: (b_, i, h_)),
        out_shape=jax.ShapeDtypeStruct((b, s, h * HEAD_DIM), _CD),
        scratch_shapes=[pltpu.VMEM((hg, nbp, HEAD_DIM), _CD),
                        pltpu.VMEM((hg, nbp, blk), F32)] + _attn_scratch(blk, hg),
        compiler_params=_params("parallel", "parallel", "arbitrary"),
        name="moba_attention",
    )(qk, qk, v)


def _dilated_body(q_ref, k_ref, v_ref, bias_ref, o_ref, *scr, blk, hg, nrel):
    i = pl.program_id(2)
    s_scr, heads = _head_state(scr, hg)
    for st in heads:
        _attn_init(*st)
    qts = [_key_major(q_ref[0, :, g * LANE:(g + 1) * LANE]) for g in range(hg)]

    def scores(j, slot):
        rows = _tile_rows(j, blk)
        bias = bias_ref[i - j]
        for g in range(hg):
            s_scr[slot, g] = bias + jnp.dot(k_ref[0, rows, g * LANE:(g + 1) * LANE], qts[g],
                                            preferred_element_type=F32)

    def consume(j, slot, masked):
        rows = _tile_rows(j, blk)
        for g, st in enumerate(heads):
            _attn_step(s_scr[slot, g], v_ref[0, rows, g * LANE:(g + 1) * LANE], *st)

    _pipelined_key_tiles(i, scores, consume, first=jnp.maximum(i - (nrel - 1), 0))
    for g, (_, l_scr, acc_scr) in enumerate(heads):
        o_ref[0, :, g * LANE:(g + 1) * LANE] = _attn_out(l_scr, acc_scr, o_ref.dtype)


def _dilated_bias(blk):
    reach = max(w for w, _ in DIL_PATTERNS)
    nrel = -(-reach // blk) + 1
    rel = jnp.arange(nrel, dtype=jnp.int32)[:, None, None]
    k = jnp.arange(blk, dtype=jnp.int32)[None, :, None]
    q = jnp.arange(blk, dtype=jnp.int32)[None, None, :]
    d = rel * blk + q - k
    mult = jnp.zeros(d.shape, F32)
    for window, dil in DIL_PATTERNS:
        mult = mult + ((d >= 0) & (d <= (window // dil) * dil) & (d % dil == 0)).astype(F32)
    return jnp.where(mult > 0, jnp.log2(jnp.maximum(mult, 1.0)), _NEG), nrel


def _dilated_attention(qk, v, *, blk, q_off, k_off, v_off, hg=_HEADS_PER_STEP):
    b, s, _ = qk.shape
    h = DIL_HEADS
    bias, nrel = _dilated_bias(blk)
    return pl.pallas_call(
        functools.partial(_dilated_body, blk=blk, hg=hg, nrel=nrel),
        grid=(b, h // hg, s // blk),
        in_specs=[pl.BlockSpec((1, blk, hg * LANE), lambda b_, h_, i: (b_, i, q_off // hg + h_)),
                  pl.BlockSpec((1, s, hg * LANE), lambda b_, h_, i: (b_, 0, k_off // hg + h_)),
                  pl.BlockSpec((1, s, hg * LANE), lambda b_, h_, i: (b_, 0, v_off // hg + h_)),
                  pl.BlockSpec((nrel, blk, blk), lambda b_, h_, i: (0, 0, 0))],
        out_specs=pl.BlockSpec((1, blk, hg * LANE), lambda b_, h_, i: (b_, i, h_)),
        out_shape=jax.ShapeDtypeStruct((b, s, h * HEAD_DIM), _CD),
        scratch_shapes=_attn_scratch(blk, hg),
        compiler_params=_params("parallel", "parallel", "arbitrary"),
        name="dilated_attention",
    )(qk, qk, v, bias)


_E_CQ, _E_CKV, _E_KPE, _E_Q, _E_K, _E_V, _E_QI, _E_KI, _E_WI = [
    int(o) for o in np.cumsum([0, MLA_Q_RANK, MLA_KV_RANK, MLA_ROPE, DSA_HEADS * HEAD_DIM, DSA_HEADS * HEAD_DIM,
                               DSA_HEADS * HEAD_DIM, IDX_HEADS * IDX_DIM, IDX_DIM])]
_R64_KIA, _R64_KIB, _R64_KPE, _R64_SLABS = 8, 9, 10, 12


def _take_cols(w, idx):
    idx = np.asarray(idx)
    cols = jnp.take(w, jnp.asarray(np.maximum(idx, 0)), axis=1)
    return jnp.where(jnp.asarray(idx >= 0)[None, :], cols, 0.0).astype(_CD)


def _r64_columns():
    half = IDX_DIM // 2
    a = np.arange(half)
    z = -np.ones(half, np.int64)
    cols = []
    for p in range(IDX_HEADS // 2):
        ha, hb = _E_QI + 2 * p * IDX_DIM, _E_QI + (2 * p + 1) * IDX_DIM
        cols += [ha + a, hb + a, ha + half + a, hb + half + a]
    cols += [_E_KI + a, z, _E_KI + half + a, z]
    cols += [z, _E_KI + a, z, _E_KI + half + a]
    cols += [_E_KPE + a, z, _E_KPE + half + a, z]
    cols += [z, z, z, z]
    return np.concatenate(cols)


def _uq_columns():
    half = MLA_ROPE // 2
    a = np.arange(half)
    z = -np.ones(half, np.int64)
    cols = []
    for h in range(MLA_HEADS):
        o = h * (MLA_NOPE + MLA_ROPE)
        cols += [o + np.arange(MLA_NOPE), o + MLA_NOPE + a, z, o + MLA_NOPE + half + a, z]
    return np.concatenate(cols)


def _ukv_columns():
    per = MLA_NOPE + MLA_V
    kn = [h * per + np.arange(MLA_NOPE) for h in range(MLA_HEADS)]
    vv = [h * per + MLA_NOPE + np.arange(MLA_V) for h in range(MLA_HEADS)]
    return np.concatenate(kn + vv)


def _rope_tables(seq, dim, scales, with_identity=False):
    inv = ROPE_THETA ** (-jnp.arange(0, dim, 2, dtype=F32) / dim)
    ang = jnp.arange(seq, dtype=F32)[:, None] * inv[None, :]
    reps = (LANE // 2) // (dim // 2)
    cos = jnp.tile(jnp.cos(ang), (1, 2 * reps))
    sin = jnp.tile(jnp.sin(ang), (1, reps))
    sin = jnp.concatenate([-sin, sin], axis=1)
    sc = jnp.asarray(scales, F32)[:, None, None]
    cos, sin = cos[None] * sc, sin[None] * sc
    if with_identity:
        cos = jnp.concatenate([cos, jnp.ones((1, seq, LANE), F32)], axis=0)
        sin = jnp.concatenate([sin, jnp.zeros((1, seq, LANE), F32)], axis=0)
    return cos, sin


def _mlp_block(x2, g, w1, w2, *, tm):
    up = _norm_matmul(x2, g, _to_mxu_dtype(*w1), tm=tm, tn=1024, out_dtype=_CD, epilogue=_ep_relu2, name="mlp_up")
    res = pl.BlockSpec((tm, 1024), lambda i, j, k: (i, j))
    return _matmul(up, _to_mxu_dtype(*w2), tm=tm, tn=1024, tk=2048, out_dtype=F32,
                   epilogue=_ep_residual, extra=(x2,), extra_specs=(res,), name="mlp_down")


def _out_proj_body(a_ref, b_ref, w_ref, r_ref, o_ref):
    half = a_ref.shape[1]
    y = jnp.dot(a_ref[...], w_ref[:half, :], preferred_element_type=F32)
    y = y + jnp.dot(b_ref[...], w_ref[half:, :], preferred_element_type=F32)
    o_ref[...] = y + r_ref[...]


def _out_proj(a, b, w_out, x2, *, tm, tn=1024):
    t, half = a.shape
    d = x2.shape[1]
    return pl.pallas_call(
        _out_proj_body,
        grid=(t // tm, d // tn),
        in_specs=[pl.BlockSpec((tm, half), lambda i, j: (i, 0)),
                  pl.BlockSpec((tm, half), lambda i, j: (i, 0)),
                  pl.BlockSpec((2 * half, tn), lambda i, j: (0, j)),
                  pl.BlockSpec((tm, tn), lambda i, j: (i, j))],
        out_specs=pl.BlockSpec((tm, tn), lambda i, j: (i, j)),
        out_shape=jax.ShapeDtypeStruct((t, d), F32),
        compiler_params=_params("parallel", "parallel"),
        name="out_proj",
    )(a, b, _to_mxu_dtype(*w_out), x2)


def _even_mixer(x2, g_mix, w_in, g_q, g_kv, w_uq, w_ukv, w_out, *, batch, seq, blk):
    t, d = x2.shape
    tm = min(2 * blk, seq)
    hm = _rmsnorm(x2, g_mix, _CD)
    mla_scale = _LOG2E * (MLA_NOPE + MLA_ROPE) ** -0.5
    rope64 = _rope_tables(seq, IDX_DIM, (1.0, mla_scale))
    rope128 = _rope_tables(seq, HEAD_DIM, (_LOG2E * HEAD_DIM ** -0.5, 1.0), with_identity=True)
    nh = DSA_HEADS * HEAD_DIM

    r64 = _rope_matmul(hm, _take_cols(w_in, _r64_columns()), rope64, lambda j: 0, seq=seq, tm=tm,
                       tn=_R64_SLABS // 2 * LANE, pattern=(True,) * (_R64_SLABS // 2), name="in_proj_rope64")
    qkv = _rope_matmul(hm, w_in[:, _E_Q:_E_QI].astype(_CD), rope128, lambda j: j, seq=seq, tm=tm,
                       tn=nh, pattern=(True,) * DSA_HEADS, name="in_proj_qkv")
    w_idx = _matmul(hm, _take_cols(w_in, np.concatenate([_E_WI + np.arange(IDX_HEADS),
                                                          -np.ones(LANE - IDX_HEADS, np.int64)])),
                    tm=tm, tn=LANE, tk=d, out_dtype=F32,
                    epilogue=_ep_scale(IDX_HEADS ** -0.5 * IDX_DIM ** -0.5), name="in_proj_widx")
    gspec = lambda n: pl.BlockSpec((1, n), lambda i, j, k: (0, 0))
    c_q = _matmul(hm, w_in[:, _E_CQ:_E_CKV].astype(_CD), tm=tm, tn=MLA_Q_RANK, tk=d, out_dtype=_CD,
                  epilogue=_ep_rmsnorm, extra=(g_q.reshape(1, -1),), extra_specs=(gspec(MLA_Q_RANK),),
                  name="in_proj_cq")
    c_kv = _matmul(hm, w_in[:, _E_CKV:_E_KPE].astype(_CD), tm=tm, tn=MLA_KV_RANK, tk=d, out_dtype=_CD,
                   epilogue=_ep_rmsnorm, extra=(g_kv.reshape(1, -1),), extra_specs=(gspec(MLA_KV_RANK),),
                   name="in_proj_ckv")

    q_mla = _rope_matmul(c_q, _take_cols(w_uq, _uq_columns()), rope64, lambda j: 1, seq=seq, tm=tm, tn=1024,
                         pattern=(False, True) * 4, plain_scale=mla_scale, name="mla_q_up")
    kv_mla = _matmul(c_kv, _take_cols(w_ukv, _ukv_columns()), tm=tm, tn=1024, tk=MLA_KV_RANK, out_dtype=_CD,
                     epilogue=_ep_scale(1.0), name="mla_kv_up")

    sh = lambda z: z.reshape(batch, seq, z.shape[-1])
    a = _mla_attention(sh(q_mla), sh(kv_mla), sh(r64), blk=blk)
    bsa = _dsa_attention(sh(r64), sh(w_idx), sh(qkv), blk=blk)
    return _out_proj(a.reshape(t, -1), bsa.reshape(t, -1), w_out, x2, tm=tm)


def _odd_mixer(x2, g_mix, w_in, w_out, *, batch, seq, blk):
    t, d = x2.shape
    tm = min(2 * blk, seq)
    nh = MOBA_HEADS * HEAD_DIM
    rope128 = _rope_tables(seq, HEAD_DIM, (_LOG2E * HEAD_DIM ** -0.5, 1.0), with_identity=True)
    qkv = _rope_matmul(x2, _to_mxu_dtype(*w_in), rope128, lambda j: j % 3, seq=seq, tm=tm, tn=nh,
                       pattern=(True,) * MOBA_HEADS, norm_gain=g_mix, name="in_proj_odd")
    qkv = qkv.reshape(batch, seq, qkv.shape[-1])
    hs = MOBA_HEADS
    c = _moba_attention(qkv, qkv, blk=blk, q_off=0, k_off=hs, v_off=2 * hs)
    dl = _dilated_attention(qkv, qkv, blk=blk, q_off=3 * hs, k_off=4 * hs, v_off=5 * hs)
    return _out_proj(c.reshape(t, -1), dl.reshape(t, -1), w_out, x2, tm=tm)


def kernel(x, ln_mix, ln_mlp, ln_final, e_w_in, e_g_q, e_g_kv, e_w_uq, e_w_ukv, e_w_out,
           o_w_in, o_w_out, mlp_w1, mlp_w2):
    batch, seq, d = x.shape
    blk = min(512, seq)
    assert seq % blk == 0 and blk % MOBA_BLOCK == 0
    x2 = x.reshape(batch * seq, d)
    depth = ln_mix.shape[0]
    for layer in range(depth):
        j = layer // 2
        if layer % 2 == 0:
            x2 = _even_mixer(x2, ln_mix[layer], e_w_in[j], e_g_q[j], e_g_kv[j], e_w_uq[j], e_w_ukv[j],
                             (e_w_out, j), batch=batch, seq=seq, blk=blk)
        else:
            x2 = _odd_mixer(x2, ln_mix[layer], (o_w_in, j), (o_w_out, j), batch=batch, seq=seq, blk=blk)
        x2 = _mlp_block(x2, ln_mlp[layer], (mlp_w1, layer), (mlp_w2, layer), tm=min(2 * blk, batch * seq))
    return _rmsnorm(x2, ln_final, x.dtype).reshape(batch, seq, d)
```

```python
import functools

import numpy as np
import jax
import jax.numpy as jnp
from jax import lax
from jax.experimental import pallas as pl
from jax.experimental.pallas import tpu as pltpu

HEAD_DIM = 128
ROPE_THETA = 10000.0
NORM_EPS = 1e-6
MLA_HEADS, MLA_Q_RANK, MLA_KV_RANK, MLA_NOPE, MLA_ROPE, MLA_V = 8, 512, 256, 128, 64, 128
DSA_HEADS, IDX_HEADS, IDX_DIM, DSA_TOPK_MAX = 8, 16, 64, 256
MOBA_HEADS, MOBA_BLOCK, MOBA_TOPK = 8, 256, 3
DIL_HEADS = 8
DIL_PATTERNS = ((128, 1), (512, 4), (2048, 16))

LANE = 128
SUBLANE = 8
VMEM_LIMIT_BYTES = 56 * 2**20

F32 = jnp.float32
_CD = jnp.bfloat16
_NEG = -1e30
_LOG2E = 1.4426950408889634
_INT_MAX = 2**31 - 1
_ATTN_TILE = 512
_PROJ_ROWS = 2 * _ATTN_TILE
_PROJ_COLS = 1024
_HEADS_PER_STEP = 2
_INTERP_PROBES = 40


def _params(*sem):
    return pltpu.CompilerParams(dimension_semantics=sem, vmem_limit_bytes=VMEM_LIMIT_BYTES)


def _rmsnorm_body(x_ref, g_ref, o_ref):
    x = x_ref[...].astype(F32)
    y = x * lax.rsqrt(jnp.mean(x * x, axis=-1, keepdims=True) + NORM_EPS)
    o_ref[...] = (y * g_ref[...]).astype(o_ref.dtype)


def _rmsnorm(x, g, out_dtype, tm=_ATTN_TILE):
    m, d = x.shape
    return pl.pallas_call(
        _rmsnorm_body,
        grid=(m // tm,),
        in_specs=[pl.BlockSpec((tm, d), lambda i: (i, 0)), pl.BlockSpec((1, d), lambda i: (0, 0))],
        out_specs=pl.BlockSpec((tm, d), lambda i: (i, 0)),
        out_shape=jax.ShapeDtypeStruct((m, d), out_dtype),
        compiler_params=_params("parallel"),
        name="rmsnorm",
    )(x, g.reshape(1, d).astype(F32))


def _cast_body(x_ref, o_ref):
    o_ref[...] = x_ref[...].astype(o_ref.dtype)


def _to_mxu_dtype(w, layer=None, tm=256):
    m, n = w.shape[-2:]
    if layer is None:
        spec = pl.BlockSpec((tm, n), lambda i: (i, 0))
    else:
        spec = pl.BlockSpec((None, tm, n), lambda i: (layer, i, 0))
    return pl.pallas_call(
        _cast_body,
        grid=(m // tm,),
        in_specs=[spec],
        out_specs=pl.BlockSpec((tm, n), lambda i: (i, 0)),
        out_shape=jax.ShapeDtypeStruct((m, n), _CD),
        compiler_params=_params("parallel"),
        name="weight_cast",
    )(w)


def _rope_slab(y, cos, sin):
    return y * cos + pltpu.roll(y, LANE // 2, 1) * sin


def _matmul_body(*refs, n_extra, epilogue, out_dtype):
    a_ref, w_ref = refs[0], refs[1]
    extra = refs[2:2 + n_extra]
    o_ref, acc_ref = refs[2 + n_extra], refs[3 + n_extra]
    k = pl.program_id(2)

    @pl.when(k == 0)
    def _init():
        acc_ref[...] = jnp.zeros_like(acc_ref)

    acc_ref[...] += jnp.dot(a_ref[...], w_ref[...], preferred_element_type=F32)

    @pl.when(k == pl.num_programs(2) - 1)
    def _finish():
        o_ref[...] = epilogue(acc_ref[...], *extra).astype(out_dtype)


def _matmul(a, w, *, tm, tn, tk, out_dtype, epilogue, extra=(), extra_specs=(), name):
    m, kd = a.shape
    n = w.shape[1]
    assert m % tm == 0 and n % tn == 0 and kd % tk == 0, (a.shape, w.shape, tm, tn, tk)
    body = functools.partial(_matmul_body, n_extra=len(extra), epilogue=epilogue, out_dtype=out_dtype)
    return pl.pallas_call(
        body,
        grid=(m // tm, n // tn, kd // tk),
        in_specs=[pl.BlockSpec((tm, tk), lambda i, j, k: (i, k)),
                  pl.BlockSpec((tk, tn), lambda i, j, k: (k, j))] + list(extra_specs),
        out_specs=pl.BlockSpec((tm, tn), lambda i, j, k: (i, j)),
        out_shape=jax.ShapeDtypeStruct((m, n), out_dtype),
        scratch_shapes=[pltpu.VMEM((tm, tn), F32)],
        compiler_params=_params("parallel", "parallel", "arbitrary"),
        name=name,
    )(a, w, *extra)


def _norm_matmul_body(*refs, n_extra, epilogue, out_dtype):
    x_ref, g_ref, w_ref = refs[:3]
    extra = refs[3:3 + n_extra]
    o_ref = refs[3 + n_extra]
    h_ref = refs[4 + n_extra]

    @pl.when(pl.program_id(1) == 0)
    def _norm():
        x = x_ref[...]
        y = x * lax.rsqrt(jnp.mean(x * x, axis=-1, keepdims=True) + NORM_EPS)
        h_ref[...] = (y * g_ref[...]).astype(h_ref.dtype)

    y = jnp.dot(h_ref[...], w_ref[...], preferred_element_type=F32)
    o_ref[...] = epilogue(y, *extra).astype(out_dtype)


def _norm_matmul(x, g, w, *, tm, tn, out_dtype, epilogue, extra=(), extra_specs=(), keep_h=False, name):
    m, d = x.shape
    n = w.shape[1]
    assert m % tm == 0 and n % tn == 0, (x.shape, w.shape, tm, tn)
    body = functools.partial(_norm_matmul_body, n_extra=len(extra), epilogue=epilogue, out_dtype=out_dtype)
    y_spec, y_shape = pl.BlockSpec((tm, tn), lambda i, j: (i, j)), jax.ShapeDtypeStruct((m, n), out_dtype)
    h_spec, h_shape = pl.BlockSpec((tm, d), lambda i, j: (i, 0)), jax.ShapeDtypeStruct((m, d), _CD)
    return pl.pallas_call(
        body,
        grid=(m // tm, n // tn),
        in_specs=[pl.BlockSpec((tm, d), lambda i, j: (i, 0)),
                  pl.BlockSpec((1, d), lambda i, j: (0, 0)),
                  pl.BlockSpec((d, tn), lambda i, j: (0, j))] + list(extra_specs),
        out_specs=(y_spec, h_spec) if keep_h else y_spec,
        out_shape=(y_shape, h_shape) if keep_h else y_shape,
        scratch_shapes=[] if keep_h else [pltpu.VMEM((tm, d), _CD)],
        compiler_params=_params("parallel", "arbitrary"),
        name=name,
    )(x, g.reshape(1, d).astype(F32), w, *extra)


def _ep_scale(scale):
    def ep(y):
        return y if scale == 1.0 else y * scale
    return ep


def _ep_relu2(y):
    return jnp.square(jnp.maximum(y, 0.0))


def _ep_residual(y, r_ref):
    return y + r_ref[...]


def _ep_rmsnorm_groups(widths):
    def ep(y, g_ref):
        out, lo = [], 0
        for w in widths:
            seg = y[:, lo:lo + w]
            out.append(seg * lax.rsqrt(jnp.mean(seg * seg, axis=-1, keepdims=True) + NORM_EPS) * g_ref[:, lo:lo + w])
            lo += w
        return jnp.concatenate(out, axis=1)
    return ep


def _ep_rope(pattern, plain_scale):
    def ep(y, cos_ref, sin_ref):
        cos, sin = cos_ref[0], sin_ref[0]
        out = []
        for c, rot in enumerate(pattern):
            slab = y[:, c * LANE:(c + 1) * LANE]
            out.append(_rope_slab(slab, cos, sin) if rot else slab * plain_scale)
        return jnp.concatenate(out, axis=1)
    return ep


def _rope_matmul(a, w, tabs, tab_of_tile, *, seq, tm, tn, pattern, plain_scale=1.0, norm_gain=None,
                 keep_h=False, name):
    nblk = seq // tm
    spec = pl.BlockSpec((1, tm, LANE), lambda i, j, *_: (tab_of_tile(j), i % nblk, 0))
    common = dict(tm=tm, tn=tn, out_dtype=_CD, epilogue=_ep_rope(pattern, plain_scale), extra=tabs,
                  extra_specs=(spec, spec), name=name)
    if norm_gain is None:
        return _matmul(a, w, tk=a.shape[1], **common)
    return _norm_matmul(a, norm_gain, w, keep_h=keep_h, **common)


def _attn_init(m_scr, l_scr, acc_scr):
    m_scr[...] = jnp.full(m_scr.shape, _NEG, F32)
    l_scr[...] = jnp.zeros(l_scr.shape, F32)
    acc_scr[...] = jnp.zeros(acc_scr.shape, F32)


def _tile_rows(j, blk):
    return pl.ds(pl.multiple_of(j * blk, blk), blk)


def _dot_tn(a, b):
    return lax.dot_general(a, b, (((0,), (0,)), ((), ())), preferred_element_type=F32)


def _key_major(q):
    return q.astype(F32).T.astype(q.dtype)


def _attn_scratch(blk, hg):
    per_head = [pltpu.VMEM((1, blk), F32), pltpu.VMEM((1, blk), F32), pltpu.VMEM((HEAD_DIM, blk), F32)]
    return [pltpu.VMEM((2, hg, blk, blk), F32)] + per_head * hg


def _head_state(scr, hg):
    return scr[0], [scr[1 + 3 * g:4 + 3 * g] for g in range(hg)]


def _attn_step(st, v, m_scr, l_scr, acc_scr):
    m_prev = m_scr[...]
    m_new = jnp.maximum(m_prev, jnp.max(st, axis=0, keepdims=True))
    alpha = jnp.exp2(m_prev - m_new)
    p = jnp.exp2(st - m_new)
    l_scr[...] = alpha * l_scr[...] + jnp.sum(p, axis=0, keepdims=True)
    acc_scr[...] = alpha * acc_scr[...] + _dot_tn(v, p.astype(v.dtype))
    m_scr[...] = m_new


def _attn_out(l_scr, acc_scr, dtype):
    return (acc_scr[...] / l_scr[...]).T.astype(dtype)


def _pipelined_key_tiles(last, scores, consume, first=0):
    n_past = last - first

    def pair(t, carry):
        j = first + 2 * t
        scores(j + 1, 1)
        consume(j, 0, False)
        scores(j + 2, 0)
        consume(j + 1, 1, False)
        return carry

    scores(first, 0)
    lax.fori_loop(0, n_past // 2, pair, 0)

    @pl.when(n_past % 2 == 1)
    def _odd():
        scores(last, 1)
        consume(last - 1, 0, False)
        consume(last, 1, True)

    @pl.when(n_past % 2 == 0)
    def _even():
        consume(last, 0, True)


def _causal(s):
    kpos = lax.broadcasted_iota(jnp.int32, s.shape, 0)
    qpos = lax.broadcasted_iota(jnp.int32, s.shape, 1)
    return jnp.where(kpos <= qpos, s, _NEG)


def _mla_body(q_ref, kn_ref, kp_ref, v_ref, o_ref, *scr, blk, hg):
    i = pl.program_id(2)
    s_scr, heads = _head_state(scr, hg)
    for st in heads:
        _attn_init(*st)
    qts = [_key_major(q_ref[0, :, g * 2 * LANE:(g + 1) * 2 * LANE]) for g in range(hg)]

    def scores(j, slot):
        rows = _tile_rows(j, blk)
        kp = kp_ref[0, rows, :]
        for g in range(hg):
            k = jnp.concatenate([kn_ref[0, rows, g * LANE:(g + 1) * LANE], kp], axis=1)
            s_scr[slot, g] = jnp.dot(k, qts[g], preferred_element_type=F32)

    def consume(j, slot, masked):
        rows = _tile_rows(j, blk)
        for g, st in enumerate(heads):
            s = s_scr[slot, g]
            _attn_step(_causal(s) if masked else s, v_ref[0, rows, g * LANE:(g + 1) * LANE], *st)

    _pipelined_key_tiles(i, scores, consume)
    for g, (_, l_scr, acc_scr) in enumerate(heads):
        o_ref[0, :, g * LANE:(g + 1) * LANE] = _attn_out(l_scr, acc_scr, o_ref.dtype)


def _mla_attention(q, kv, r64, *, blk, hg=_HEADS_PER_STEP):
    b, s, _ = q.shape
    h = MLA_HEADS
    return pl.pallas_call(
        functools.partial(_mla_body, blk=blk, hg=hg),
        grid=(b, h // hg, s // blk),
        in_specs=[pl.BlockSpec((1, blk, hg * 2 * LANE), lambda b_, h_, i: (b_, i, h_)),
                  pl.BlockSpec((1, s, hg * LANE), lambda b_, h_, i: (b_, 0, h_)),
                  pl.BlockSpec((1, s, LANE), lambda b_, h_, i: (b_, 0, _R64_KPE)),
                  pl.BlockSpec((1, s, hg * LANE), lambda b_, h_, i: (b_, 0, h // hg + h_))],
        out_specs=pl.BlockSpec((1, blk, hg * LANE), lambda b_, h_, i: (b_, i, h_)),
        out_shape=jax.ShapeDtypeStruct((b, s, h * MLA_V), _CD),
        scratch_shapes=_attn_scratch(blk, hg),
        compiler_params=_params("parallel", "parallel", "arbitrary"),
        name="mla_attention",
    )(q, kv, r64, kv)


def _order_key(x):
    bits = lax.bitcast_convert_type(x, jnp.int32)
    return bits ^ ((bits >> 31) & _INT_MAX)


def _key_score(key):
    return lax.bitcast_convert_type(key ^ ((key >> 31) & _INT_MAX), F32)


def _dsa_select(i, qi_ref, ka_ref, kb_ref, w_ref, sel_scr, jc_scr, *, blk, topk, seq):
    wt = w_ref[0].T
    qits = [_key_major(qi_ref[0, :, p * LANE:(p + 1) * LANE]) for p in range(IDX_HEADS // 2)]
    kloc = lax.broadcasted_iota(jnp.int32, (blk, blk), 0)
    qpos = i * blk + lax.broadcasted_iota(jnp.int32, (blk, blk), 1)

    def score_tile(c, carry):
        rows = _tile_rows(c, blk)
        ka, kb = ka_ref[0, rows, :], kb_ref[0, rows, :]
        acc = jnp.zeros((blk, blk), F32)
        for p in range(IDX_HEADS // 2):
            da = jnp.dot(ka, qits[p], preferred_element_type=F32)
            db = jnp.dot(kb, qits[p], preferred_element_type=F32)
            acc = acc + wt[2 * p:2 * p + 1, :] * jnp.maximum(da, 0.0) + wt[2 * p + 1:2 * p + 2, :] * jnp.maximum(db, 0.0)
        causal = c * blk + kloc <= qpos
        sel_scr[c] = jnp.where(causal, acc, -jnp.inf)
        lo8, hi8 = carry
        fold = lambda x: x.reshape(blk // SUBLANE, SUBLANE, blk)
        lo8 = jnp.minimum(lo8, jnp.min(fold(jnp.where(causal, acc, jnp.inf)), axis=0))
        hi8 = jnp.maximum(hi8, jnp.max(fold(jnp.where(causal, acc, -jnp.inf)), axis=0))
        return lo8, hi8

    lo8, hi8 = lax.fori_loop(0, i + 1, score_tile, (jnp.full((SUBLANE, blk), jnp.inf, F32),
                                                    jnp.full((SUBLANE, blk), -jnp.inf, F32)))
    s_min, s_max = lo8[0:1], hi8[0:1]
    for r in range(1, SUBLANE):
        s_min = jnp.minimum(s_min, lo8[r:r + 1])
        s_max = jnp.maximum(s_max, hi8[r:r + 1])

    n_causal = i * blk + lax.broadcasted_iota(jnp.int32, (1, blk), 1) + 1
    kk = jnp.minimum(topk, n_causal).astype(F32)

    def count(pred):
        def body(c, cnt):
            hit = jnp.where(pred(sel_scr[c], c * blk + kloc), 1.0, 0.0)
            return cnt + jnp.sum(hit.reshape(blk // SUBLANE, SUBLANE, blk), axis=0)
        cnt = lax.fori_loop(0, i + 1, body, jnp.zeros((SUBLANE, blk), F32))
        return jnp.sum(cnt, axis=0, keepdims=True)

    def unfinished(lo, hi, n_lo):
        return (n_lo > kk) & (hi > lo + 1)

    def probe(state):
        lo, hi, n_lo, n_hi, w_lo, w_hi, last, it, _ = state
        lo_f, hi_f = _key_score(lo), _key_score(hi)
        a, b = (n_lo - kk + 0.5) * w_lo, (kk - 0.5 - n_hi) * w_hi
        interp = _order_key(lo_f + (hi_f - lo_f) * (a / (a + b)))
        mid = (lo >> 1) + (hi >> 1) + (lo & hi & 1)
        cand = jnp.clip(jnp.where(it < _INTERP_PROBES, interp, mid), lo + 1, hi - 1)
        cand_f = _key_score(cand)
        n = count(lambda s, kpos: s >= cand_f)
        live = unfinished(lo, hi, n_lo)
        up = live & (n >= kk)
        down = live & (n < kk)
        lo, n_lo = jnp.where(up, cand, lo), jnp.where(up, n, n_lo)
        hi, n_hi = jnp.where(down, cand, hi), jnp.where(down, n, n_hi)
        w_hi = jnp.where(up, jnp.where(last > 0.5, 0.5 * w_hi, 1.0), jnp.where(down, 1.0, w_hi))
        w_lo = jnp.where(down, jnp.where(last < -0.5, 0.5 * w_lo, 1.0), jnp.where(up, 1.0, w_lo))
        last = jnp.where(up, 1.0, jnp.where(down, -1.0, last))
        more = jnp.max(jnp.where(unfinished(lo, hi, n_lo), 1.0, 0.0))
        return lo, hi, n_lo, n_hi, w_lo, w_hi, last, it + 1, more

    n_all = n_causal.astype(F32)
    one, zero = jnp.ones((1, blk), F32), jnp.zeros((1, blk), F32)
    key_min, key_end = _order_key(s_min), _order_key(s_max) + 1
    start = (key_min, key_end, n_all, zero, one, one, zero, jnp.int32(0),
             jnp.max(jnp.where(unfinished(key_min, key_end, n_all), 1.0, 0.0)))
    t_key, _, n_ge = lax.while_loop(lambda s: s[8] > 0.5, probe, start)[:3]
    t = _key_score(t_key)
    jc_scr[...] = jnp.full((1, blk), seq, jnp.int32)

    @pl.when(jnp.max(n_ge - kk) > 0.5)
    def _ties():
        need = kk - count(lambda s, kpos: s > t)
        nbits = (seq - 1).bit_length()

        def pos_step(b, jc):
            cand = jc + (jnp.int32(1) << (nbits - 1 - b))
            n_lt = count(lambda s, kpos: (s == t) & (kpos < cand))
            return jnp.where(n_lt < need, cand, jc)

        jc_scr[...] = lax.fori_loop(0, nbits, pos_step, jnp.zeros((1, blk), jnp.int32))

    jc = jc_scr[...]

    def to_bias(c, carry):
        s = sel_scr[c]
        chosen = (s > t) | ((s == t) & (c * blk + kloc <= jc))
        sel_scr[c] = jnp.where(chosen, 0.0, _NEG).astype(F32)
        return carry

    lax.fori_loop(0, i + 1, to_bias, 0)


def _dsa_body(qi_ref, ka_ref, kb_ref, w_ref, q_ref, k_ref, v_ref, o_ref, sel_scr, jc_scr, *scr,
              blk, hg, topk, seq):
    i = pl.program_id(1)

    @pl.when(pl.program_id(2) == 0)
    def _select():
        _dsa_select(i, qi_ref, ka_ref, kb_ref, w_ref, sel_scr, jc_scr, blk=blk, topk=topk, seq=seq)

    s_scr, heads = _head_state(scr, hg)
    for st in heads:
        _attn_init(*st)
    qts = [_key_major(q_ref[0, :, g * LANE:(g + 1) * LANE]) for g in range(hg)]

    def scores(j, slot):
        rows = _tile_rows(j, blk)
        bias = sel_scr[j]
        for g in range(hg):
            s_scr[slot, g] = bias + jnp.dot(k_ref[0, rows, g * LANE:(g + 1) * LANE], qts[g],
                                            preferred_element_type=F32)

    def consume(j, slot, masked):
        rows = _tile_rows(j, blk)
        for g, st in enumerate(heads):
            _attn_step(s_scr[slot, g], v_ref[0, rows, g * LANE:(g + 1) * LANE], *st)

    _pipelined_key_tiles(i, scores, consume)
    for g, (_, l_scr, acc_scr) in enumerate(heads):
        o_ref[0, :, g * LANE:(g + 1) * LANE] = _attn_out(l_scr, acc_scr, o_ref.dtype)


def _dsa_attention(r64, w_idx, qkv, *, blk, hg=_HEADS_PER_STEP):
    b, s, _ = qkv.shape
    h = DSA_HEADS
    topk = min(DSA_TOPK_MAX, s // 4)
    return pl.pallas_call(
        functools.partial(_dsa_body, blk=blk, hg=hg, topk=topk, seq=s),
        grid=(b, s // blk, h // hg),
        in_specs=[pl.BlockSpec((1, blk, 8 * LANE), lambda b_, i, h_: (b_, i, 0)),
                  pl.BlockSpec((1, s, LANE), lambda b_, i, h_: (b_, 0, _R64_KIA)),
                  pl.BlockSpec((1, s, LANE), lambda b_, i, h_: (b_, 0, _R64_KIB)),
                  pl.BlockSpec((1, blk, LANE), lambda b_, i, h_: (b_, i, 0)),
                  pl.BlockSpec((1, blk, hg * LANE), lambda b_, i, h_: (b_, i, h_)),
                  pl.BlockSpec((1, s, hg * LANE), lambda b_, i, h_: (b_, 0, h // hg + h_)),
                  pl.BlockSpec((1, s, hg * LANE), lambda b_, i, h_: (b_, 0, 2 * (h // hg) + h_))],
        out_specs=pl.BlockSpec((1, blk, hg * LANE), lambda b_, i, h_: (b_, i, h_)),
        out_shape=jax.ShapeDtypeStruct((b, s, h * HEAD_DIM), _CD),
        scratch_shapes=[pltpu.VMEM((s // blk, blk, blk), F32),
                        pltpu.VMEM((1, blk), jnp.int32)] + _attn_scratch(blk, hg),
        compiler_params=_params("parallel", "arbitrary", "arbitrary"),
        name="dsa_attention",
    )(r64, r64, r64, w_idx, qkv, qkv, qkv)


def _moba_body(q_ref, k_ref, v_ref, o_ref, kmean_scr, pick_scr, *scr, blk, hg, seq, nbp):
    i = pl.program_id(2)
    nper = blk // MOBA_BLOCK
    shift = MOBA_BLOCK.bit_length() - 1
    s_scr, heads = _head_state(scr, hg)

    @pl.when(i == 0)
    def _block_means():
        r = lax.broadcasted_iota(jnp.int32, (nbp, seq), 0)
        c = lax.broadcasted_iota(jnp.int32, (nbp, seq), 1)
        avg = jnp.where((c >> shift) == r, 1.0 / MOBA_BLOCK, 0.0).astype(k_ref.dtype)
        for g in range(hg):
            kmean_scr[g] = jnp.dot(avg, k_ref[0, :, g * LANE:(g + 1) * LANE],
                                   preferred_element_type=F32).astype(kmean_scr.dtype)

    qts = [_key_major(q_ref[0, :, g * LANE:(g + 1) * LANE]) for g in range(hg)]
    kblk = lax.broadcasted_iota(jnp.int32, (nbp, blk), 0)
    own = (i * blk + lax.broadcasted_iota(jnp.int32, (nbp, blk), 1)) >> shift
    kblk_f = kblk.astype(F32)
    for g, st in enumerate(heads):
        _attn_init(*st)
        gate = jnp.where(kblk < own, jnp.dot(kmean_scr[g], qts[g], preferred_element_type=F32), _NEG)
        pick = jnp.zeros((nbp, blk), F32)
        for _ in range(MOBA_TOPK):
            best = jnp.max(gate, axis=0, keepdims=True)
            hit = (gate == best) & (best > 0.5 * _NEG)
            first = jnp.min(jnp.where(hit, kblk_f, float(nbp)), axis=0, keepdims=True)
            new = kblk_f == first
            pick = jnp.where(new, 1.0, pick)
            gate = jnp.where(new, _NEG, gate)
        pick_scr[g] = jnp.where((pick > 0.5) | (kblk == own), 0.0, _NEG)

    def scores(j, slot):
        for g in range(hg):
            for u in range(nper):
                rows = pl.ds(pl.multiple_of(j * blk + u * MOBA_BLOCK, MOBA_BLOCK), MOBA_BLOCK)
                s = jnp.dot(k_ref[0, rows, g * LANE:(g + 1) * LANE], qts[g], preferred_element_type=F32)
                s_scr[slot, g, u * MOBA_BLOCK:(u + 1) * MOBA_BLOCK, :] = s + pick_scr[g, pl.ds(j * nper + u, 1), :]

    def consume(j, slot, masked):
        rows = _tile_rows(j, blk)
        for g, st in enumerate(heads):
            s = s_scr[slot, g]
            _attn_step(_causal(s) if masked else s, v_ref[0, rows, g * LANE:(g + 1) * LANE], *st)

    _pipelined_key_tiles(i, scores, consume)
    for g, (_, l_scr, acc_scr) in enumerate(heads):
        o_ref[0, :, g * LANE:(g + 1) * LANE] = _attn_out(l_scr, acc_scr, o_ref.dtype)


def _moba_attention(qk, v, *, blk, q_off, k_off, v_off, hg=_HEADS_PER_STEP):
    b, s, _ = qk.shape
    h = MOBA_HEADS
    assert s % MOBA_BLOCK == 0 and blk % MOBA_BLOCK == 0
    nbp = -(-(s // MOBA_BLOCK) // SUBLANE) * SUBLANE
    return pl.pallas_call(
        functools.partial(_moba_body, blk=blk, hg=hg, seq=s, nbp=nbp),
        grid=(b, h // hg, s // blk),
        in_specs=[pl.BlockSpec((1, blk, hg * LANE), lambda b_, h_, i: (b_, i, q_off // hg + h_)),
                  pl.BlockSpec((1, s, hg * LANE), lambda b_, h_, i: (b_, 0, k_off // hg + h_)),
                  pl.BlockSpec((1, s, hg * LANE), lambda b_, h_, i: (b_, 0, v_off // hg + h_))],
        out_specs=pl.BlockSpec((1, blk, hg * LANE), lambda b_, h_, i: (b_, i, h_)),
        out_shape=jax.ShapeDtypeStruct((b, s, h * HEAD_DIM), _CD),
        scratch_shapes=[pltpu.VMEM((hg, nbp, HEAD_DIM), _CD),
                        pltpu.VMEM((hg, nbp, blk), F32)] + _attn_scratch(blk, hg),
        compiler_params=_params("parallel", "parallel", "arbitrary"),
        name="moba_attention",
    )(qk, qk, v)


def _dilated_body(q_ref, k_ref, v_ref, bias_ref, o_ref, *scr, blk, hg, nrel):
    i = pl.program_id(2)
    s_scr, heads = _head_state(scr, hg)
    for st in heads:
        _attn_init(*st)
    qts = [_key_major(q_ref[0, :, g * LANE:(g + 1) * LANE]) for g in range(hg)]

    def scores(j, slot):
        rows = _tile_rows(j, blk)
        bias = bias_ref[i - j]
        for g in range(hg):
            s_scr[slot, g] = bias + jnp.dot(k_ref[0, rows, g * LANE:(g + 1) * LANE], qts[g],
                                            preferred_element_type=F32)

    def consume(j, slot, masked):
        rows = _tile_rows(j, blk)
        for g, st in enumerate(heads):
            _attn_step(s_scr[slot, g], v_ref[0, rows, g * LANE:(g + 1) * LANE], *st)

    _pipelined_key_tiles(i, scores, consume, first=jnp.maximum(i - (nrel - 1), 0))
    for g, (_, l_scr, acc_scr) in enumerate(heads):
        o_ref[0, :, g * LANE:(g + 1) * LANE] = _attn_out(l_scr, acc_scr, o_ref.dtype)


def _dilated_bias(blk):
    reach = max(w for w, _ in DIL_PATTERNS)
    nrel = -(-reach // blk) + 1
    rel = jnp.arange(nrel, dtype=jnp.int32)[:, None, None]
    k = jnp.arange(blk, dtype=jnp.int32)[None, :, None]
    q = jnp.arange(blk, dtype=jnp.int32)[None, None, :]
    d = rel * blk + q - k
    mult = jnp.zeros(d.shape, F32)
    for window, dil in DIL_PATTERNS:
        mult = mult + ((d >= 0) & (d <= (window // dil) * dil) & (d % dil == 0)).astype(F32)
    return jnp.where(mult > 0, jnp.log2(jnp.maximum(mult, 1.0)), _NEG), nrel


def _dilated_attention(qk, v, *, blk, q_off, k_off, v_off, hg=_HEADS_PER_STEP):
    b, s, _ = qk.shape
    h = DIL_HEADS
    bias, nrel = _dilated_bias(blk)
    return pl.pallas_call(
        functools.partial(_dilated_body, blk=blk, hg=hg, nrel=nrel),
        grid=(b, h // hg, s // blk),
        in_specs=[pl.BlockSpec((1, blk, hg * LANE), lambda b_, h_, i: (b_, i, q_off // hg + h_)),
                  pl.BlockSpec((1, s, hg * LANE), lambda b_, h_, i: (b_, 0, k_off // hg + h_)),
                  pl.BlockSpec((1, s, hg * LANE), lambda b_, h_, i: (b_, 0, v_off // hg + h_)),
                  pl.BlockSpec((nrel, blk, blk), lambda b_, h_, i: (0, 0, 0))],
        out_specs=pl.BlockSpec((1, blk, hg * LANE), lambda b_, h_, i: (b_, i, h_)),
        out_shape=jax.ShapeDtypeStruct((b, s, h * HEAD_DIM), _CD),
        scratch_shapes=_attn_scratch(blk, hg),
        compiler_params=_params("parallel", "parallel", "arbitrary"),
        name="dilated_attention",
    )(qk, qk, v, bias)


_E_CQ, _E_CKV, _E_KPE, _E_Q, _E_K, _E_V, _E_QI, _E_KI, _E_WI = [
    int(o) for o in np.cumsum([0, MLA_Q_RANK, MLA_KV_RANK, MLA_ROPE, DSA_HEADS * HEAD_DIM, DSA_HEADS * HEAD_DIM,
                               DSA_HEADS * HEAD_DIM, IDX_HEADS * IDX_DIM, IDX_DIM])]
_R64_KIA, _R64_KIB, _R64_KPE, _R64_SLABS = 8, 9, 10, 12


def _take_cols(w, idx):
    idx = np.asarray(idx)
    cols = jnp.take(w, jnp.asarray(np.maximum(idx, 0)), axis=1)
    return jnp.where(jnp.asarray(idx >= 0)[None, :], cols, 0.0).astype(_CD)


def _r64_columns():
    half = IDX_DIM // 2
    a = np.arange(half)
    z = -np.ones(half, np.int64)
    cols = []
    for p in range(IDX_HEADS // 2):
        ha, hb = _E_QI + 2 * p * IDX_DIM, _E_QI + (2 * p + 1) * IDX_DIM
        cols += [ha + a, hb + a, ha + half + a, hb + half + a]
    cols += [_E_KI + a, z, _E_KI + half + a, z]
    cols += [z, _E_KI + a, z, _E_KI + half + a]
    cols += [_E_KPE + a, z, _E_KPE + half + a, z]
    cols += [z, z, z, z]
    return np.concatenate(cols)


def _uq_columns():
    half = MLA_ROPE // 2
    a = np.arange(half)
    z = -np.ones(half, np.int64)
    cols = []
    for h in range(MLA_HEADS):
        o = h * (MLA_NOPE + MLA_ROPE)
        cols += [o + np.arange(MLA_NOPE), o + MLA_NOPE + a, z, o + MLA_NOPE + half + a, z]
    return np.concatenate(cols)


def _ukv_columns():
    per = MLA_NOPE + MLA_V
    kn = [h * per + np.arange(MLA_NOPE) for h in range(MLA_HEADS)]
    vv = [h * per + MLA_NOPE + np.arange(MLA_V) for h in range(MLA_HEADS)]
    return np.concatenate(kn + vv)


def _rope_tables(seq, dim, scales, with_identity=False):
    inv = ROPE_THETA ** (-jnp.arange(0, dim, 2, dtype=F32) / dim)
    ang = jnp.arange(seq, dtype=F32)[:, None] * inv[None, :]
    reps = (LANE // 2) // (dim // 2)
    cos = jnp.tile(jnp.cos(ang), (1, 2 * reps))
    sin = jnp.tile(jnp.sin(ang), (1, reps))
    sin = jnp.concatenate([-sin, sin], axis=1)
    sc = jnp.asarray(scales, F32)[:, None, None]
    cos, sin = cos[None] * sc, sin[None] * sc
    if with_identity:
        cos = jnp.concatenate([cos, jnp.ones((1, seq, LANE), F32)], axis=0)
        sin = jnp.concatenate([sin, jnp.zeros((1, seq, LANE), F32)], axis=0)
    return cos, sin


def _mlp_block(x2, g, w1, w2, *, tm):
    tn = _PROJ_COLS
    up = _norm_matmul(x2, g, _to_mxu_dtype(*w1), tm=tm, tn=tn, out_dtype=_CD, epilogue=_ep_relu2, name="mlp_up")
    res = pl.BlockSpec((tm, tn), lambda i, j, k: (i, j))
    return _matmul(up, _to_mxu_dtype(*w2), tm=tm, tn=tn, tk=2 * tn, out_dtype=F32,
                   epilogue=_ep_residual, extra=(x2,), extra_specs=(res,), name="mlp_down")


def _out_proj_body(a_ref, b_ref, w_ref, r_ref, o_ref):
    half = a_ref.shape[1]
    y = jnp.dot(a_ref[...], w_ref[:half, :], preferred_element_type=F32)
    y = y + jnp.dot(b_ref[...], w_ref[half:, :], preferred_element_type=F32)
    o_ref[...] = y + r_ref[...]


def _out_proj(a, b, w_out, x2, *, tm, tn=_PROJ_COLS):
    t, half = a.shape
    d = x2.shape[1]
    return pl.pallas_call(
        _out_proj_body,
        grid=(t // tm, d // tn),
        in_specs=[pl.BlockSpec((tm, half), lambda i, j: (i, 0)),
                  pl.BlockSpec((tm, half), lambda i, j: (i, 0)),
                  pl.BlockSpec((2 * half, tn), lambda i, j: (0, j)),
                  pl.BlockSpec((tm, tn), lambda i, j: (i, j))],
        out_specs=pl.BlockSpec((tm, tn), lambda i, j: (i, j)),
        out_shape=jax.ShapeDtypeStruct((t, d), F32),
        compiler_params=_params("parallel", "parallel"),
        name="out_proj",
    )(a, b, _to_mxu_dtype(*w_out), x2)


def _even_mixer(x2, g_mix, w_in, g_q, g_kv, w_uq, w_ukv, w_out, *, batch, seq, blk):
    t, d = x2.shape
    tm = min(_PROJ_ROWS, seq)
    mla_scale = _LOG2E * (MLA_NOPE + MLA_ROPE) ** -0.5
    rope64 = _rope_tables(seq, IDX_DIM, (1.0, mla_scale))
    rope128 = _rope_tables(seq, HEAD_DIM, (_LOG2E * HEAD_DIM ** -0.5, 1.0), with_identity=True)
    nh = DSA_HEADS * HEAD_DIM

    qkv, hm = _rope_matmul(x2, w_in[:, _E_Q:_E_QI].astype(_CD), rope128, lambda j: j, seq=seq, tm=tm, tn=nh,
                           pattern=(True,) * DSA_HEADS, norm_gain=g_mix, keep_h=True, name="in_proj_qkv")
    r64 = _rope_matmul(hm, _take_cols(w_in, _r64_columns()), rope64, lambda j: 0, seq=seq, tm=tm,
                       tn=_R64_SLABS // 2 * LANE, pattern=(True,) * (_R64_SLABS // 2), name="in_proj_rope64")
    w_idx = _matmul(hm, _take_cols(w_in, np.concatenate([_E_WI + np.arange(IDX_HEADS),
                                                          -np.ones(LANE - IDX_HEADS, np.int64)])),
                    tm=tm, tn=LANE, tk=d, out_dtype=F32,
                    epilogue=_ep_scale(IDX_HEADS ** -0.5 * IDX_DIM ** -0.5), name="in_proj_widx")
    ranks = (MLA_Q_RANK, MLA_KV_RANK)
    lat = _matmul(hm, w_in[:, _E_CQ:_E_KPE].astype(_CD), tm=tm, tn=sum(ranks), tk=d, out_dtype=_CD,
                  epilogue=_ep_rmsnorm_groups(ranks), extra=(jnp.concatenate([g_q, g_kv]).reshape(1, -1),),
                  extra_specs=(pl.BlockSpec((1, sum(ranks)), lambda i, j, k: (0, 0)),), name="in_proj_latents")

    q_mla = _rope_matmul(lat[:, :MLA_Q_RANK], _take_cols(w_uq, _uq_columns()), rope64, lambda j: 1, seq=seq,
                         tm=tm, tn=_PROJ_COLS, pattern=(False, True) * (_PROJ_COLS // (2 * LANE)),
                         plain_scale=mla_scale, name="mla_q_up")
    kv_mla = _matmul(lat[:, MLA_Q_RANK:], _take_cols(w_ukv, _ukv_columns()), tm=tm, tn=_PROJ_COLS, tk=MLA_KV_RANK,
                     out_dtype=_CD, epilogue=_ep_scale(1.0), name="mla_kv_up")

    sh = lambda z: z.reshape(batch, seq, z.shape[-1])
    a = _mla_attention(sh(q_mla), sh(kv_mla), sh(r64), blk=blk)
    bsa = _dsa_attention(sh(r64), sh(w_idx), sh(qkv), blk=blk)
    return _out_proj(a.reshape(t, -1), bsa.reshape(t, -1), w_out, x2, tm=tm)


def _odd_mixer(x2, g_mix, w_in, w_out, *, batch, seq, blk):
    t, d = x2.shape
    tm = min(_PROJ_ROWS, seq)
    nh = MOBA_HEADS * HEAD_DIM
    rope128 = _rope_tables(seq, HEAD_DIM, (_LOG2E * HEAD_DIM ** -0.5, 1.0), with_identity=True)
    qkv = _rope_matmul(x2, _to_mxu_dtype(*w_in), rope128, lambda j: j % 3, seq=seq, tm=tm, tn=nh,
                       pattern=(True,) * MOBA_HEADS, norm_gain=g_mix, name="in_proj_odd")
    qkv = qkv.reshape(batch, seq, qkv.shape[-1])
    hs = MOBA_HEADS
    c = _moba_attention(qkv, qkv, blk=blk, q_off=0, k_off=hs, v_off=2 * hs)
    dl = _dilated_attention(qkv, qkv, blk=blk, q_off=3 * hs, k_off=4 * hs, v_off=5 * hs)
    return _out_proj(c.reshape(t, -1), dl.reshape(t, -1), w_out, x2, tm=tm)


def kernel(x, ln_mix, ln_mlp, ln_final, e_w_in, e_g_q, e_g_kv, e_w_uq, e_w_ukv, e_w_out,
           o_w_in, o_w_out, mlp_w1, mlp_w2):
    batch, seq, d = x.shape
    blk = min(_ATTN_TILE, seq)
    assert seq % blk == 0 and blk % MOBA_BLOCK == 0
    x2 = x.reshape(batch * seq, d)
    depth = ln_mix.shape[0]
    for layer in range(depth):
        j = layer // 2
        if layer % 2 == 0:
            x2 = _even_mixer(x2, ln_mix[layer], e_w_in[j], e_g_q[j], e_g_kv[j], e_w_uq[j], e_w_ukv[j],
                             (e_w_out, j), batch=batch, seq=seq, blk=blk)
        else:
            x2 = _odd_mixer(x2, ln_mix[layer], (o_w_in, j), (o_w_out, j), batch=batch, seq=seq, blk=blk)
        x2 = _mlp_block(x2, ln_mlp[layer], (mlp_w1, layer), (mlp_w2, layer), tm=min(_PROJ_ROWS, batch * seq))
    return _rmsnorm(x2, ln_final, x.dtype).reshape(batch, seq, d)
```

```python
import functools

import numpy as np
import jax
import jax.numpy as jnp
from jax import lax
from jax.experimental import pallas as pl
from jax.experimental.pallas import tpu as pltpu

HEAD_DIM = 128
ROPE_THETA = 10000.0
NORM_EPS = 1e-6
MLA_HEADS, MLA_Q_RANK, MLA_KV_RANK, MLA_NOPE, MLA_ROPE, MLA_V = 8, 512, 256, 128, 64, 128
DSA_HEADS, IDX_HEADS, IDX_DIM, DSA_TOPK_MAX = 8, 16, 64, 256
MOBA_HEADS, MOBA_BLOCK, MOBA_TOPK = 8, 256, 3
DIL_HEADS = 8
DIL_PATTERNS = ((128, 1), (512, 4), (2048, 16))

LANE = 128
SUBLANE = 8
VMEM_LIMIT_BYTES = 56 * 2**20

F32 = jnp.float32
_CD = jnp.bfloat16
_NEG = -1e30
_LOG2E = 1.4426950408889634
_INT_MAX = 2**31 - 1
_ATTN_TILE = 512
_PROJ_ROWS = 2 * _ATTN_TILE
_PROJ_COLS = 1024
_HEADS_PER_STEP = 2
_INTERP_PROBES = 40


def _params(*sem):
    return pltpu.CompilerParams(dimension_semantics=sem, vmem_limit_bytes=VMEM_LIMIT_BYTES)


def _rmsnorm_body(x_ref, g_ref, o_ref):
    x = x_ref[...].astype(F32)
    y = x * lax.rsqrt(jnp.mean(x * x, axis=-1, keepdims=True) + NORM_EPS)
    o_ref[...] = (y * g_ref[...]).astype(o_ref.dtype)


def _rmsnorm(x, g, out_dtype, tm=_ATTN_TILE):
    m, d = x.shape
    return pl.pallas_call(
        _rmsnorm_body,
        grid=(m // tm,),
        in_specs=[pl.BlockSpec((tm, d), lambda i: (i, 0)), pl.BlockSpec((1, d), lambda i: (0, 0))],
        out_specs=pl.BlockSpec((tm, d), lambda i: (i, 0)),
        out_shape=jax.ShapeDtypeStruct((m, d), out_dtype),
        compiler_params=_params("parallel"),
        name="rmsnorm",
    )(x, g.reshape(1, d).astype(F32))


def _cast_body(x_ref, o_ref):
    o_ref[...] = x_ref[...].astype(o_ref.dtype)


def _to_mxu_dtype(w, layer=None, tm=256):
    m, n = w.shape[-2:]
    if layer is None:
        spec = pl.BlockSpec((tm, n), lambda i: (i, 0))
    else:
        spec = pl.BlockSpec((None, tm, n), lambda i: (layer, i, 0))
    return pl.pallas_call(
        _cast_body,
        grid=(m // tm,),
        in_specs=[spec],
        out_specs=pl.BlockSpec((tm, n), lambda i: (i, 0)),
        out_shape=jax.ShapeDtypeStruct((m, n), _CD),
        compiler_params=_params("parallel"),
        name="weight_cast",
    )(w)


def _rope_slab(y, cos, sin):
    return y * cos + pltpu.roll(y, LANE // 2, 1) * sin


def _matmul_body(*refs, n_extra, epilogue, out_dtype):
    a_ref, w_ref = refs[0], refs[1]
    extra = refs[2:2 + n_extra]
    o_ref, acc_ref = refs[2 + n_extra], refs[3 + n_extra]
    k = pl.program_id(2)

    @pl.when(k == 0)
    def _init():
        acc_ref[...] = jnp.zeros_like(acc_ref)

    acc_ref[...] += jnp.dot(a_ref[...], w_ref[...], preferred_element_type=F32)

    @pl.when(k == pl.num_programs(2) - 1)
    def _finish():
        o_ref[...] = epilogue(acc_ref[...], *extra).astype(out_dtype)


def _matmul(a, w, *, tm, tn, tk, out_dtype, epilogue, extra=(), extra_specs=(), name):
    m, kd = a.shape
    n = w.shape[1]
    assert m % tm == 0 and n % tn == 0 and kd % tk == 0, (a.shape, w.shape, tm, tn, tk)
    body = functools.partial(_matmul_body, n_extra=len(extra), epilogue=epilogue, out_dtype=out_dtype)
    return pl.pallas_call(
        body,
        grid=(m // tm, n // tn, kd // tk),
        in_specs=[pl.BlockSpec((tm, tk), lambda i, j, k: (i, k)),
                  pl.BlockSpec((tk, tn), lambda i, j, k: (k, j))] + list(extra_specs),
        out_specs=pl.BlockSpec((tm, tn), lambda i, j, k: (i, j)),
        out_shape=jax.ShapeDtypeStruct((m, n), out_dtype),
        scratch_shapes=[pltpu.VMEM((tm, tn), F32)],
        compiler_params=_params("parallel", "parallel", "arbitrary"),
        name=name,
    )(a, w, *extra)


def _norm_matmul_body(*refs, n_extra, epilogue, out_dtype):
    x_ref, g_ref, w_ref = refs[:3]
    extra = refs[3:3 + n_extra]
    o_ref = refs[3 + n_extra]
    h_ref = refs[4 + n_extra]

    @pl.when(pl.program_id(1) == 0)
    def _norm():
        x = x_ref[...]
        y = x * lax.rsqrt(jnp.mean(x * x, axis=-1, keepdims=True) + NORM_EPS)
        h_ref[...] = (y * g_ref[...]).astype(h_ref.dtype)

    y = jnp.dot(h_ref[...], w_ref[...], preferred_element_type=F32)
    o_ref[...] = epilogue(y, *extra).astype(out_dtype)


def _norm_matmul(x, g, w, *, tm, tn, out_dtype, epilogue, extra=(), extra_specs=(), keep_h=False, name):
    m, d = x.shape
    n = w.shape[1]
    assert m % tm == 0 and n % tn == 0, (x.shape, w.shape, tm, tn)
    body = functools.partial(_norm_matmul_body, n_extra=len(extra), epilogue=epilogue, out_dtype=out_dtype)
    y_spec, y_shape = pl.BlockSpec((tm, tn), lambda i, j: (i, j)), jax.ShapeDtypeStruct((m, n), out_dtype)
    h_spec, h_shape = pl.BlockSpec((tm, d), lambda i, j: (i, 0)), jax.ShapeDtypeStruct((m, d), _CD)
    return pl.pallas_call(
        body,
        grid=(m // tm, n // tn),
        in_specs=[pl.BlockSpec((tm, d), lambda i, j: (i, 0)),
                  pl.BlockSpec((1, d), lambda i, j: (0, 0)),
                  pl.BlockSpec((d, tn), lambda i, j: (0, j))] + list(extra_specs),
        out_specs=(y_spec, h_spec) if keep_h else y_spec,
        out_shape=(y_shape, h_shape) if keep_h else y_shape,
        scratch_shapes=[] if keep_h else [pltpu.VMEM((tm, d), _CD)],
        compiler_params=_params("parallel", "arbitrary"),
        name=name,
    )(x, g.reshape(1, d).astype(F32), w, *extra)


def _ep_scale(scale):
    def ep(y):
        return y if scale == 1.0 else y * scale
    return ep


def _ep_relu2(y):
    return jnp.square(jnp.maximum(y, 0.0))


def _ep_residual(y, r_ref):
    return y + r_ref[...]


def _ep_rmsnorm_groups(widths):
    def ep(y, g_ref):
        out, lo = [], 0
        for w in widths:
            seg = y[:, lo:lo + w]
            out.append(seg * lax.rsqrt(jnp.mean(seg * seg, axis=-1, keepdims=True) + NORM_EPS) * g_ref[:, lo:lo + w])
            lo += w
        return jnp.concatenate(out, axis=1)
    return ep


def _ep_rope(pattern, plain_scale):
    def ep(y, cos_ref, sin_ref):
        cos, sin = cos_ref[0], sin_ref[0]
        out = []
        for c, rot in enumerate(pattern):
            slab = y[:, c * LANE:(c + 1) * LANE]
            out.append(_rope_slab(slab, cos, sin) if rot else slab * plain_scale)
        return jnp.concatenate(out, axis=1)
    return ep


def _rope_matmul(a, w, tabs, tab_of_tile, *, seq, tm, tn, pattern, plain_scale=1.0, norm_gain=None,
                 keep_h=False, name):
    nblk = seq // tm
    spec = pl.BlockSpec((1, tm, LANE), lambda i, j, *_: (tab_of_tile(j), i % nblk, 0))
    common = dict(tm=tm, tn=tn, out_dtype=_CD, epilogue=_ep_rope(pattern, plain_scale), extra=tabs,
                  extra_specs=(spec, spec), name=name)
    if norm_gain is None:
        return _matmul(a, w, tk=a.shape[1], **common)
    return _norm_matmul(a, norm_gain, w, keep_h=keep_h, **common)


def _attn_init(m_scr, l_scr, acc_scr):
    m_scr[...] = jnp.full(m_scr.shape, _NEG, F32)
    l_scr[...] = jnp.zeros(l_scr.shape, F32)
    acc_scr[...] = jnp.zeros(acc_scr.shape, F32)


def _tile_rows(j, blk):
    return pl.ds(pl.multiple_of(j * blk, blk), blk)


def _dot_tn(a, b):
    return lax.dot_general(a, b, (((0,), (0,)), ((), ())), preferred_element_type=F32)


def _key_major(q):
    return q.astype(F32).T.astype(q.dtype)


def _attn_scratch(blk, hg):
    per_head = [pltpu.VMEM((1, blk), F32), pltpu.VMEM((1, blk), F32), pltpu.VMEM((HEAD_DIM, blk), F32)]
    return [pltpu.VMEM((2, hg, blk, blk), F32)] + per_head * hg


def _head_state(scr, hg):
    return scr[0], [scr[1 + 3 * g:4 + 3 * g] for g in range(hg)]


def _attn_step(st, v, m_scr, l_scr, acc_scr):
    m_prev = m_scr[...]
    m_new = jnp.maximum(m_prev, jnp.max(st, axis=0, keepdims=True))
    alpha = jnp.exp2(m_prev - m_new)
    p = jnp.exp2(st - m_new)
    l_scr[...] = alpha * l_scr[...] + jnp.sum(p, axis=0, keepdims=True)
    acc_scr[...] = alpha * acc_scr[...] + _dot_tn(v, p.astype(v.dtype))
    m_scr[...] = m_new


def _attn_out(l_scr, acc_scr, dtype):
    return (acc_scr[...] / l_scr[...]).T.astype(dtype)


def _pipelined_key_tiles(last, scores, consume, first=0):
    n_past = last - first

    def pair(t, carry):
        j = first + 2 * t
        scores(j + 1, 1)
        consume(j, 0, False)
        scores(j + 2, 0)
        consume(j + 1, 1, False)
        return carry

    scores(first, 0)
    lax.fori_loop(0, n_past // 2, pair, 0)

    @pl.when(n_past % 2 == 1)
    def _odd():
        scores(last, 1)
        consume(last - 1, 0, False)
        consume(last, 1, True)

    @pl.when(n_past % 2 == 0)
    def _even():
        consume(last, 0, True)


def _causal(s):
    kpos = lax.broadcasted_iota(jnp.int32, s.shape, 0)
    qpos = lax.broadcasted_iota(jnp.int32, s.shape, 1)
    return jnp.where(kpos <= qpos, s, _NEG)


def _mla_body(q_ref, kn_ref, kp_ref, v_ref, o_ref, *scr, blk, hg):
    i = pl.program_id(2)
    s_scr, heads = _head_state(scr, hg)
    for st in heads:
        _attn_init(*st)
    qts = [_key_major(q_ref[0, :, g * 2 * LANE:(g + 1) * 2 * LANE]) for g in range(hg)]

    def scores(j, slot):
        rows = _tile_rows(j, blk)
        kp = kp_ref[0, rows, :]
        for g in range(hg):
            k = jnp.concatenate([kn_ref[0, rows, g * LANE:(g + 1) * LANE], kp], axis=1)
            s_scr[slot, g] = jnp.dot(k, qts[g], preferred_element_type=F32)

    def consume(j, slot, masked):
        rows = _tile_rows(j, blk)
        for g, st in enumerate(heads):
            s = s_scr[slot, g]
            _attn_step(_causal(s) if masked else s, v_ref[0, rows, g * LANE:(g + 1) * LANE], *st)

    _pipelined_key_tiles(i, scores, consume)
    for g, (_, l_scr, acc_scr) in enumerate(heads):
        o_ref[0, :, g * LANE:(g + 1) * LANE] = _attn_out(l_scr, acc_scr, o_ref.dtype)


def _mla_attention(q, kv, r64, *, blk, hg=_HEADS_PER_STEP):
    b, s, _ = q.shape
    h = MLA_HEADS
    return pl.pallas_call(
        functools.partial(_mla_body, blk=blk, hg=hg),
        grid=(b, h // hg, s // blk),
        in_specs=[pl.BlockSpec((1, blk, hg * 2 * LANE), lambda b_, h_, i: (b_, i, h_)),
                  pl.BlockSpec((1, s, hg * LANE), lambda b_, h_, i: (b_, 0, h_)),
                  pl.BlockSpec((1, s, LANE), lambda b_, h_, i: (b_, 0, _R64_KPE)),
                  pl.BlockSpec((1, s, hg * LANE), lambda b_, h_, i: (b_, 0, h // hg + h_))],
        out_specs=pl.BlockSpec((1, blk, hg * LANE), lambda b_, h_, i: (b_, i, h_)),
        out_shape=jax.ShapeDtypeStruct((b, s, h * MLA_V), _CD),
        scratch_shapes=_attn_scratch(blk, hg),
        compiler_params=_params("parallel", "parallel", "arbitrary"),
        name="mla_attention",
    )(q, kv, r64, kv)


def _order_key(x):
    bits = lax.bitcast_convert_type(x, jnp.int32)
    return bits ^ ((bits >> 31) & _INT_MAX)


def _key_score(key):
    return lax.bitcast_convert_type(key ^ ((key >> 31) & _INT_MAX), F32)


_LOW16 = 2**16 - 1
_PACKED_ROWS = 2 * SUBLANE


def _coarse_floor(x):
    bits = lax.bitcast_convert_type(x, jnp.int32)
    cut = bits & ~_LOW16
    cut = cut + jnp.where((bits < 0) & (bits != cut), _LOW16 + 1, 0)
    return lax.bitcast_convert_type(cut, F32)


def _grid_index(v):
    return _order_key(v) >> 16


def _grid_value(g):
    return _key_score((g << 16) + jnp.where(g < 0, _LOW16, 0))


def _fold_rows(x, rows):
    parts = [x[r:r + rows] for r in range(0, x.shape[0], rows)]
    while len(parts) > 1:
        parts = [a + b for a, b in zip(parts[::2], parts[1::2])] + parts[len(parts) // 2 * 2:]
    return parts[0]


def _dsa_select(i, qi_ref, ka_ref, kb_ref, w_ref, sel_scr, coarse_scr, jc_scr, *, blk, topk, seq):
    wt = w_ref[0].T
    qits = [_key_major(qi_ref[0, :, p * LANE:(p + 1) * LANE]) for p in range(IDX_HEADS // 2)]
    kloc = lax.broadcasted_iota(jnp.int32, (blk, blk), 0)
    qpos = i * blk + lax.broadcasted_iota(jnp.int32, (blk, blk), 1)

    def score_tile(c, carry):
        rows = _tile_rows(c, blk)
        ka, kb = ka_ref[0, rows, :], kb_ref[0, rows, :]
        acc = jnp.zeros((blk, blk), F32)
        for p in range(IDX_HEADS // 2):
            da = jnp.dot(ka, qits[p], preferred_element_type=F32)
            db = jnp.dot(kb, qits[p], preferred_element_type=F32)
            acc = acc + wt[2 * p:2 * p + 1, :] * jnp.maximum(da, 0.0) + wt[2 * p + 1:2 * p + 2, :] * jnp.maximum(db, 0.0)
        causal = c * blk + kloc <= qpos
        sel_scr[c] = jnp.where(causal, acc, -jnp.inf)
        coarse_scr[c] = jnp.where(causal, _coarse_floor(acc), -jnp.inf).astype(coarse_scr.dtype)
        lo8, hi8 = carry
        fold = lambda x: x.reshape(blk // SUBLANE, SUBLANE, blk)
        lo8 = jnp.minimum(lo8, jnp.min(fold(jnp.where(causal, acc, jnp.inf)), axis=0))
        hi8 = jnp.maximum(hi8, jnp.max(fold(jnp.where(causal, acc, -jnp.inf)), axis=0))
        return lo8, hi8

    lo8, hi8 = lax.fori_loop(0, i + 1, score_tile, (jnp.full((SUBLANE, blk), jnp.inf, F32),
                                                    jnp.full((SUBLANE, blk), -jnp.inf, F32)))
    s_min, s_max = lo8[0:1], hi8[0:1]
    for r in range(1, SUBLANE):
        s_min = jnp.minimum(s_min, lo8[r:r + 1])
        s_max = jnp.maximum(s_max, hi8[r:r + 1])

    n_causal = i * blk + lax.broadcasted_iota(jnp.int32, (1, blk), 1) + 1
    kk = jnp.minimum(topk, n_causal).astype(F32)

    def count(pred):
        def body(c, cnt):
            hit = jnp.where(pred(sel_scr[c], c * blk + kloc), 1.0, 0.0)
            return cnt + jnp.sum(hit.reshape(blk // SUBLANE, SUBLANE, blk), axis=0)
        cnt = lax.fori_loop(0, i + 1, body, jnp.zeros((SUBLANE, blk), F32))
        return jnp.sum(cnt, axis=0, keepdims=True)

    def count_coarse(cand):
        cand = cand.astype(coarse_scr.dtype)
        one_c, zero_c = jnp.ones((), coarse_scr.dtype), jnp.zeros((), coarse_scr.dtype)

        def body(c, cnt):
            hit = jnp.where(coarse_scr[c] >= cand, one_c, zero_c)
            return cnt + _fold_rows(hit, _PACKED_ROWS).astype(F32)
        cnt = lax.fori_loop(0, i + 1, body, jnp.zeros((_PACKED_ROWS, blk), F32))
        return jnp.sum(cnt, axis=0, keepdims=True)

    def bracket_search(count_ge, value_of, coord_of, lo, hi, n_lo, n_hi):
        def unfinished(lo, hi, n_lo):
            return (n_lo > kk) & (hi > lo + 1)

        def probe(state):
            lo, hi, n_lo, n_hi, w_lo, w_hi, last, it, _ = state
            lo_f, hi_f = value_of(lo), value_of(hi)
            a, b = (n_lo - kk + 0.5) * w_lo, (kk - 0.5 - n_hi) * w_hi
            interp = coord_of(lo_f + (hi_f - lo_f) * (a / (a + b)))
            mid = (lo >> 1) + (hi >> 1) + (lo & hi & 1)
            cand = jnp.clip(jnp.where(it < _INTERP_PROBES, interp, mid), lo + 1, hi - 1)
            n = count_ge(value_of(cand))
            live = unfinished(lo, hi, n_lo)
            up = live & (n >= kk)
            down = live & (n < kk)
            lo, n_lo = jnp.where(up, cand, lo), jnp.where(up, n, n_lo)
            hi, n_hi = jnp.where(down, cand, hi), jnp.where(down, n, n_hi)
            w_hi = jnp.where(up, jnp.where(last > 0.5, 0.5 * w_hi, 1.0), jnp.where(down, 1.0, w_hi))
            w_lo = jnp.where(down, jnp.where(last < -0.5, 0.5 * w_lo, 1.0), jnp.where(up, 1.0, w_lo))
            last = jnp.where(up, 1.0, jnp.where(down, -1.0, last))
            more = jnp.max(jnp.where(unfinished(lo, hi, n_lo), 1.0, 0.0))
            return lo, hi, n_lo, n_hi, w_lo, w_hi, last, it + 1, more

        one, zero = jnp.ones((1, blk), F32), jnp.zeros((1, blk), F32)
        start = (lo, hi, n_lo, n_hi, one, one, zero, jnp.int32(0),
                 jnp.max(jnp.where(unfinished(lo, hi, n_lo), 1.0, 0.0)))
        return lax.while_loop(lambda s: s[8] > 0.5, probe, start)[:4]

    g_lo, g_hi, n_lo, n_hi = bracket_search(
        count_coarse, _grid_value, _grid_index, _grid_index(_coarse_floor(s_min)),
        _grid_index(_coarse_floor(s_max)) + 1, n_causal.astype(F32), jnp.zeros((1, blk), F32))
    t_key, _, n_ge, _ = bracket_search(
        lambda cand: count(lambda s, kpos: s >= cand), _key_score, _order_key,
        _order_key(_grid_value(g_lo)), _order_key(_grid_value(g_hi)), n_lo, n_hi)
    t = _key_score(t_key)
    jc_scr[...] = jnp.full((1, blk), seq, jnp.int32)

    @pl.when(jnp.max(n_ge - kk) > 0.5)
    def _ties():
        need = kk - count(lambda s, kpos: s > t)
        nbits = (seq - 1).bit_length()

        def pos_step(b, jc):
            cand = jc + (jnp.int32(1) << (nbits - 1 - b))
            n_lt = count(lambda s, kpos: (s == t) & (kpos < cand))
            return jnp.where(n_lt < need, cand, jc)

        jc_scr[...] = lax.fori_loop(0, nbits, pos_step, jnp.zeros((1, blk), jnp.int32))

    jc = jc_scr[...]

    def to_bias(c, carry):
        s = sel_scr[c]
        chosen = (s > t) | ((s == t) & (c * blk + kloc <= jc))
        sel_scr[c] = jnp.where(chosen, 0.0, _NEG).astype(F32)
        return carry

    lax.fori_loop(0, i + 1, to_bias, 0)


def _dsa_body(qi_ref, ka_ref, kb_ref, w_ref, q_ref, k_ref, v_ref, o_ref, sel_scr, coarse_scr, jc_scr, *scr,
              blk, hg, topk, seq):
    i = pl.program_id(1)

    @pl.when(pl.program_id(2) == 0)
    def _select():
        _dsa_select(i, qi_ref, ka_ref, kb_ref, w_ref, sel_scr, coarse_scr, jc_scr, blk=blk, topk=topk, seq=seq)

    s_scr, heads = _head_state(scr, hg)
    for st in heads:
        _attn_init(*st)
    qts = [_key_major(q_ref[0, :, g * LANE:(g + 1) * LANE]) for g in range(hg)]

    def scores(j, slot):
        rows = _tile_rows(j, blk)
        bias = sel_scr[j]
        for g in range(hg):
            s_scr[slot, g] = bias + jnp.dot(k_ref[0, rows, g * LANE:(g + 1) * LANE], qts[g],
                                            preferred_element_type=F32)

    def consume(j, slot, masked):
        rows = _tile_rows(j, blk)
        for g, st in enumerate(heads):
            _attn_step(s_scr[slot, g], v_ref[0, rows, g * LANE:(g + 1) * LANE], *st)

    _pipelined_key_tiles(i, scores, consume)
    for g, (_, l_scr, acc_scr) in enumerate(heads):
        o_ref[0, :, g * LANE:(g + 1) * LANE] = _attn_out(l_scr, acc_scr, o_ref.dtype)


def _dsa_attention(r64, w_idx, qkv, *, blk, hg=_HEADS_PER_STEP):
    b, s, _ = qkv.shape
    h = DSA_HEADS
    topk = min(DSA_TOPK_MAX, s // 4)
    return pl.pallas_call(
        functools.partial(_dsa_body, blk=blk, hg=hg, topk=topk, seq=s),
        grid=(b, s // blk, h // hg),
        in_specs=[pl.BlockSpec((1, blk, 8 * LANE), lambda b_, i, h_: (b_, i, 0)),
                  pl.BlockSpec((1, s, LANE), lambda b_, i, h_: (b_, 0, _R64_KIA), pipeline_mode=pl.Buffered(1)),
                  pl.BlockSpec((1, s, LANE), lambda b_, i, h_: (b_, 0, _R64_KIB), pipeline_mode=pl.Buffered(1)),
                  pl.BlockSpec((1, blk, LANE), lambda b_, i, h_: (b_, i, 0)),
                  pl.BlockSpec((1, blk, hg * LANE), lambda b_, i, h_: (b_, i, h_)),
                  pl.BlockSpec((1, s, hg * LANE), lambda b_, i, h_: (b_, 0, h // hg + h_)),
                  pl.BlockSpec((1, s, hg * LANE), lambda b_, i, h_: (b_, 0, 2 * (h // hg) + h_))],
        out_specs=pl.BlockSpec((1, blk, hg * LANE), lambda b_, i, h_: (b_, i, h_)),
        out_shape=jax.ShapeDtypeStruct((b, s, h * HEAD_DIM), _CD),
        scratch_shapes=[pltpu.VMEM((s // blk, blk, blk), F32),
                        pltpu.VMEM((s // blk, blk, blk), jnp.bfloat16),
                        pltpu.VMEM((1, blk), jnp.int32)] + _attn_scratch(blk, hg),
        compiler_params=_params("parallel", "arbitrary", "arbitrary"),
        name="dsa_attention",
    )(r64, r64, r64, w_idx, qkv, qkv, qkv)


def _moba_body(q_ref, k_ref, v_ref, o_ref, kmean_scr, pick_scr, *scr, blk, hg, seq, nbp):
    i = pl.program_id(2)
    nper = blk // MOBA_BLOCK
    shift = MOBA_BLOCK.bit_length() - 1
    s_scr, heads = _head_state(scr, hg)

    @pl.when(i == 0)
    def _block_means():
        r = lax.broadcasted_iota(jnp.int32, (nbp, seq), 0)
        c = lax.broadcasted_iota(jnp.int32, (nbp, seq), 1)
        avg = jnp.where((c >> shift) == r, 1.0 / MOBA_BLOCK, 0.0).astype(k_ref.dtype)
        for g in range(hg):
            kmean_scr[g] = jnp.dot(avg, k_ref[0, :, g * LANE:(g + 1) * LANE],
                                   preferred_element_type=F32).astype(kmean_scr.dtype)

    qts = [_key_major(q_ref[0, :, g * LANE:(g + 1) * LANE]) for g in range(hg)]
    kblk = lax.broadcasted_iota(jnp.int32, (nbp, blk), 0)
    own = (i * blk + lax.broadcasted_iota(jnp.int32, (nbp, blk), 1)) >> shift
    kblk_f = kblk.astype(F32)
    for g, st in enumerate(heads):
        _attn_init(*st)
        gate = jnp.where(kblk < own, jnp.dot(kmean_scr[g], qts[g], preferred_element_type=F32), _NEG)
        pick = jnp.zeros((nbp, blk), F32)
        for _ in range(MOBA_TOPK):
            best = jnp.max(gate, axis=0, keepdims=True)
            hit = (gate == best) & (best > 0.5 * _NEG)
            first = jnp.min(jnp.where(hit, kblk_f, float(nbp)), axis=0, keepdims=True)
            new = kblk_f == first
            pick = jnp.where(new, 1.0, pick)
            gate = jnp.where(new, _NEG, gate)
        pick_scr[g] = jnp.where((pick > 0.5) | (kblk == own), 0.0, _NEG)

    def scores(j, slot):
        for g in range(hg):
            for u in range(nper):
                rows = pl.ds(pl.multiple_of(j * blk + u * MOBA_BLOCK, MOBA_BLOCK), MOBA_BLOCK)
                s = jnp.dot(k_ref[0, rows, g * LANE:(g + 1) * LANE], qts[g], preferred_element_type=F32)
                s_scr[slot, g, u * MOBA_BLOCK:(u + 1) * MOBA_BLOCK, :] = s + pick_scr[g, pl.ds(j * nper + u, 1), :]

    def consume(j, slot, masked):
        rows = _tile_rows(j, blk)
        for g, st in enumerate(heads):
            s = s_scr[slot, g]
            _attn_step(_causal(s) if masked else s, v_ref[0, rows, g * LANE:(g + 1) * LANE], *st)

    _pipelined_key_tiles(i, scores, consume)
    for g, (_, l_scr, acc_scr) in enumerate(heads):
        o_ref[0, :, g * LANE:(g + 1) * LANE] = _attn_out(l_scr, acc_scr, o_ref.dtype)


def _moba_attention(qk, v, *, blk, q_off, k_off, v_off, hg=_HEADS_PER_STEP):
    b, s, _ = qk.shape
    h = MOBA_HEADS
    assert s % MOBA_BLOCK == 0 and blk % MOBA_BLOCK == 0
    nbp = -(-(s // MOBA_BLOCK) // SUBLANE) * SUBLANE
    return pl.pallas_call(
        functools.partial(_moba_body, blk=blk, hg=hg, seq=s, nbp=nbp),
        grid=(b, h // hg, s // blk),
        in_specs=[pl.BlockSpec((1, blk, hg * LANE), lambda b_, h_, i: (b_, i, q_off // hg + h_)),
                  pl.BlockSpec((1, s, hg * LANE), lambda b_, h_, i: (b_, 0, k_off // hg + h_)),
                  pl.BlockSpec((1, s, hg * LANE), lambda b_, h_, i: (b_, 0, v_off // hg + h_))],
        out_specs=pl.BlockSpec((1, blk, hg * LANE), lambda b_, h_, i: (b_, i, h_)),
        out_shape=jax.ShapeDtypeStruct((b, s, h * HEAD_DIM), _CD),
        scratch_shapes=[pltpu.VMEM((hg, nbp, HEAD_DIM), _CD),
                        pltpu.VMEM((hg, nbp, blk), F32)] + _attn_scratch(blk, hg),
        compiler_params=_params("parallel", "parallel", "arbitrary"),
        name="moba_attention",
    )(qk, qk, v)


def _dilated_body(q_ref, k_ref, v_ref, bias_ref, o_ref, *scr, blk, hg, nrel):
    i = pl.program_id(2)
    s_scr, heads = _head_state(scr, hg)
    for st in heads:
        _attn_init(*st)
    qts = [_key_major(q_ref[0, :, g * LANE:(g + 1) * LANE]) for g in range(hg)]

    def scores(j, slot):
        rows = _tile_rows(j, blk)
        bias = bias_ref[i - j]
        for g in range(hg):
            s_scr[slot, g] = bias + jnp.dot(k_ref[0, rows, g * LANE:(g + 1) * LANE], qts[g],
                                            preferred_element_type=F32)

    def consume(j, slot, masked):
        rows = _tile_rows(j, blk)
        for g, st in enumerate(heads):
            _attn_step(s_scr[slot, g], v_ref[0, rows, g * LANE:(g + 1) * LANE], *st)

    _pipelined_key_tiles(i, scores, consume, first=jnp.maximum(i - (nrel - 1), 0))
    for g, (_, l_scr, acc_scr) in enumerate(heads):
        o_ref[0, :, g * LANE:(g + 1) * LANE] = _attn_out(l_scr, acc_scr, o_ref.dtype)


def _dilated_bias(blk):
    reach = max(w for w, _ in DIL_PATTERNS)
    nrel = -(-reach // blk) + 1
    rel = jnp.arange(nrel, dtype=jnp.int32)[:, None, None]
    k = jnp.arange(blk, dtype=jnp.int32)[None, :, None]
    q = jnp.arange(blk, dtype=jnp.int32)[None, None, :]
    d = rel * blk + q - k
    mult = jnp.zeros(d.shape, F32)
    for window, dil in DIL_PATTERNS:
        mult = mult + ((d >= 0) & (d <= (window // dil) * dil) & (d % dil == 0)).astype(F32)
    return jnp.where(mult > 0, jnp.log2(jnp.maximum(mult, 1.0)), _NEG), nrel


def _dilated_attention(qk, v, *, blk, q_off, k_off, v_off, hg=_HEADS_PER_STEP):
    b, s, _ = qk.shape
    h = DIL_HEADS
    bias, nrel = _dilated_bias(blk)
    return pl.pallas_call(
        functools.partial(_dilated_body, blk=blk, hg=hg, nrel=nrel),
        grid=(b, h // hg, s // blk),
        in_specs=[pl.BlockSpec((1, blk, hg * LANE), lambda b_, h_, i: (b_, i, q_off // hg + h_)),
                  pl.BlockSpec((1, s, hg * LANE), lambda b_, h_, i: (b_, 0, k_off // hg + h_)),
                  pl.BlockSpec((1, s, hg * LANE), lambda b_, h_, i: (b_, 0, v_off // hg + h_)),
                  pl.BlockSpec((nrel, blk, blk), lambda b_, h_, i: (0, 0, 0))],
        out_specs=pl.BlockSpec((1, blk, hg * LANE), lambda b_, h_, i: (b_, i, h_)),
        out_shape=jax.ShapeDtypeStruct((b, s, h * HEAD_DIM), _CD),
        scratch_shapes=_attn_scratch(blk, hg),
        compiler_params=_params("parallel", "parallel", "arbitrary"),
        name="dilated_attention",
    )(qk, qk, v, bias)


_E_CQ, _E_CKV, _E_KPE, _E_Q, _E_K, _E_V, _E_QI, _E_KI, _E_WI = [
    int(o) for o in np.cumsum([0, MLA_Q_RANK, MLA_KV_RANK, MLA_ROPE, DSA_HEADS * HEAD_DIM, DSA_HEADS * HEAD_DIM,
                               DSA_HEADS * HEAD_DIM, IDX_HEADS * IDX_DIM, IDX_DIM])]
_R64_KIA, _R64_KIB, _R64_KPE, _R64_SLABS = 8, 9, 10, 12


def _take_cols(w, idx):
    idx = np.asarray(idx)
    cols = jnp.take(w, jnp.asarray(np.maximum(idx, 0)), axis=1)
    return jnp.where(jnp.asarray(idx >= 0)[None, :], cols, 0.0).astype(_CD)


def _r64_columns():
    half = IDX_DIM // 2
    a = np.arange(half)
    z = -np.ones(half, np.int64)
    cols = []
    for p in range(IDX_HEADS // 2):
        ha, hb = _E_QI + 2 * p * IDX_DIM, _E_QI + (2 * p + 1) * IDX_DIM
        cols += [ha + a, hb + a, ha + half + a, hb + half + a]
    cols += [_E_KI + a, z, _E_KI + half + a, z]
    cols += [z, _E_KI + a, z, _E_KI + half + a]
    cols += [_E_KPE + a, z, _E_KPE + half + a, z]
    cols += [z, z, z, z]
    return np.concatenate(cols)


def _uq_columns():
    half = MLA_ROPE // 2
    a = np.arange(half)
    z = -np.ones(half, np.int64)
    cols = []
    for h in range(MLA_HEADS):
        o = h * (MLA_NOPE + MLA_ROPE)
        cols += [o + np.arange(MLA_NOPE), o + MLA_NOPE + a, z, o + MLA_NOPE + half + a, z]
    return np.concatenate(cols)


def _ukv_columns():
    per = MLA_NOPE + MLA_V
    kn = [h * per + np.arange(MLA_NOPE) for h in range(MLA_HEADS)]
    vv = [h * per + MLA_NOPE + np.arange(MLA_V) for h in range(MLA_HEADS)]
    return np.concatenate(kn + vv)


def _rope_tables(seq, dim, scales, with_identity=False):
    inv = ROPE_THETA ** (-jnp.arange(0, dim, 2, dtype=F32) / dim)
    ang = jnp.arange(seq, dtype=F32)[:, None] * inv[None, :]
    reps = (LANE // 2) // (dim // 2)
    cos = jnp.tile(jnp.cos(ang), (1, 2 * reps))
    sin = jnp.tile(jnp.sin(ang), (1, reps))
    sin = jnp.concatenate([-sin, sin], axis=1)
    sc = jnp.asarray(scales, F32)[:, None, None]
    cos, sin = cos[None] * sc, sin[None] * sc
    if with_identity:
        cos = jnp.concatenate([cos, jnp.ones((1, seq, LANE), F32)], axis=0)
        sin = jnp.concatenate([sin, jnp.zeros((1, seq, LANE), F32)], axis=0)
    return cos, sin


def _mlp_block(x2, g, w1, w2, *, tm):
    tn = _PROJ_COLS
    up = _norm_matmul(x2, g, _to_mxu_dtype(*w1), tm=tm, tn=tn, out_dtype=_CD, epilogue=_ep_relu2, name="mlp_up")
    res = pl.BlockSpec((tm, tn), lambda i, j, k: (i, j))
    return _matmul(up, _to_mxu_dtype(*w2), tm=tm, tn=tn, tk=2 * tn, out_dtype=F32,
                   epilogue=_ep_residual, extra=(x2,), extra_specs=(res,), name="mlp_down")


def _out_proj_body(a_ref, b_ref, w_ref, r_ref, o_ref):
    half = a_ref.shape[1]
    y = jnp.dot(a_ref[...], w_ref[:half, :], preferred_element_type=F32)
    y = y + jnp.dot(b_ref[...], w_ref[half:, :], preferred_element_type=F32)
    o_ref[...] = y + r_ref[...]


def _out_proj(a, b, w_out, x2, *, tm, tn=_PROJ_COLS):
    t, half = a.shape
    d = x2.shape[1]
    return pl.pallas_call(
        _out_proj_body,
        grid=(t // tm, d // tn),
        in_specs=[pl.BlockSpec((tm, half), lambda i, j: (i, 0)),
                  pl.BlockSpec((tm, half), lambda i, j: (i, 0)),
                  pl.BlockSpec((2 * half, tn), lambda i, j: (0, j)),
                  pl.BlockSpec((tm, tn), lambda i, j: (i, j))],
        out_specs=pl.BlockSpec((tm, tn), lambda i, j: (i, j)),
        out_shape=jax.ShapeDtypeStruct((t, d), F32),
        compiler_params=_params("parallel", "parallel"),
        name="out_proj",
    )(a, b, _to_mxu_dtype(*w_out), x2)


def _even_mixer(x2, g_mix, w_in, g_q, g_kv, w_uq, w_ukv, w_out, *, batch, seq, blk):
    t, d = x2.shape
    tm = min(_PROJ_ROWS, seq)
    mla_scale = _LOG2E * (MLA_NOPE + MLA_ROPE) ** -0.5
    rope64 = _rope_tables(seq, IDX_DIM, (1.0, mla_scale))
    rope128 = _rope_tables(seq, HEAD_DIM, (_LOG2E * HEAD_DIM ** -0.5, 1.0), with_identity=True)
    nh = DSA_HEADS * HEAD_DIM

    qkv, hm = _rope_matmul(x2, w_in[:, _E_Q:_E_QI].astype(_CD), rope128, lambda j: j, seq=seq, tm=tm, tn=nh,
                           pattern=(True,) * DSA_HEADS, norm_gain=g_mix, keep_h=True, name="in_proj_qkv")
    r64 = _rope_matmul(hm, _take_cols(w_in, _r64_columns()), rope64, lambda j: 0, seq=seq, tm=tm,
                       tn=_R64_SLABS // 2 * LANE, pattern=(True,) * (_R64_SLABS // 2), name="in_proj_rope64")
    w_idx = _matmul(hm, _take_cols(w_in, np.concatenate([_E_WI + np.arange(IDX_HEADS),
                                                          -np.ones(LANE - IDX_HEADS, np.int64)])),
                    tm=tm, tn=LANE, tk=d, out_dtype=F32,
                    epilogue=_ep_scale(IDX_HEADS ** -0.5 * IDX_DIM ** -0.5), name="in_proj_widx")
    ranks = (MLA_Q_RANK, MLA_KV_RANK)
    lat = _matmul(hm, w_in[:, _E_CQ:_E_KPE].astype(_CD), tm=tm, tn=sum(ranks), tk=d, out_dtype=_CD,
                  epilogue=_ep_rmsnorm_groups(ranks), extra=(jnp.concatenate([g_q, g_kv]).reshape(1, -1),),
                  extra_specs=(pl.BlockSpec((1, sum(ranks)), lambda i, j, k: (0, 0)),), name="in_proj_latents")

    q_mla = _rope_matmul(lat[:, :MLA_Q_RANK], _take_cols(w_uq, _uq_columns()), rope64, lambda j: 1, seq=seq,
                         tm=tm, tn=_PROJ_COLS, pattern=(False, True) * (_PROJ_COLS // (2 * LANE)),
                         plain_scale=mla_scale, name="mla_q_up")
    kv_mla = _matmul(lat[:, MLA_Q_RANK:], _take_cols(w_ukv, _ukv_columns()), tm=tm, tn=_PROJ_COLS, tk=MLA_KV_RANK,
                     out_dtype=_CD, epilogue=_ep_scale(1.0), name="mla_kv_up")

    sh = lambda z: z.reshape(batch, seq, z.shape[-1])
    a = _mla_attention(sh(q_mla), sh(kv_mla), sh(r64), blk=blk)
    bsa = _dsa_attention(sh(r64), sh(w_idx), sh(qkv), blk=blk)
    return _out_proj(a.reshape(t, -1), bsa.reshape(t, -1), w_out, x2, tm=tm)


def _odd_mixer(x2, g_mix, w_in, w_out, *, batch, seq, blk):
    t, d = x2.shape
    tm = min(_PROJ_ROWS, seq)
    nh = MOBA_HEADS * HEAD_DIM
    rope128 = _rope_tables(seq, HEAD_DIM, (_LOG2E * HEAD_DIM ** -0.5, 1.0), with_identity=True)
    qkv = _rope_matmul(x2, _to_mxu_dtype(*w_in), rope128, lambda j: j % 3, seq=seq, tm=tm, tn=nh,
                       pattern=(True,) * MOBA_HEADS, norm_gain=g_mix, name="in_proj_odd")
    qkv = qkv.reshape(batch, seq, qkv.shape[-1])
    hs = MOBA_HEADS
    c = _moba_attention(qkv, qkv, blk=blk, q_off=0, k_off=hs, v_off=2 * hs)
    dl = _dilated_attention(qkv, qkv, blk=blk, q_off=3 * hs, k_off=4 * hs, v_off=5 * hs)
    return _out_proj(c.reshape(t, -1), dl.reshape(t, -1), w_out, x2, tm=tm)


def kernel(x, ln_mix, ln_mlp, ln_final, e_w_in, e_g_q, e_g_kv, e_w_uq, e_w_ukv, e_w_out,
           o_w_in, o_w_out, mlp_w1, mlp_w2):
    batch, seq, d = x.shape
    blk = min(_ATTN_TILE, seq)
    assert seq % blk == 0 and blk % MOBA_BLOCK == 0
    x2 = x.reshape(batch * seq, d)
    depth = ln_mix.shape[0]
    for layer in range(depth):
        j = layer // 2
        if layer % 2 == 0:
            x2 = _even_mixer(x2, ln_mix[layer], e_w_in[j], e_g_q[j], e_g_kv[j], e_w_uq[j], e_w_ukv[j],
                             (e_w_out, j), batch=batch, seq=seq, blk=blk)
        else:
            x2 = _odd_mixer(x2, ln_mix[layer], (o_w_in, j), (o_w_out, j), batch=batch, seq=seq, blk=blk)
        x2 = _mlp_block(x2, ln_mlp[layer], (mlp_w1, layer), (mlp_w2, layer), tm=min(_PROJ_ROWS, batch * seq))
    return _rmsnorm(x2, ln_final, x.dtype).reshape(batch, seq, d)
```

```python
import functools

import numpy as np
import jax
import jax.numpy as jnp
from jax import lax
from jax.experimental import pallas as pl
from jax.experimental.pallas import tpu as pltpu

HEAD_DIM = 128
ROPE_THETA = 10000.0
NORM_EPS = 1e-6
MLA_HEADS, MLA_Q_RANK, MLA_KV_RANK, MLA_NOPE, MLA_ROPE, MLA_V = 8, 512, 256, 128, 64, 128
DSA_HEADS, IDX_HEADS, IDX_DIM, DSA_TOPK_MAX = 8, 16, 64, 256
MOBA_HEADS, MOBA_BLOCK, MOBA_TOPK = 8, 256, 3
DIL_HEADS = 8
DIL_PATTERNS = ((128, 1), (512, 4), (2048, 16))

LANE = 128
SUBLANE = 8
VMEM_LIMIT_BYTES = 56 * 2**20

F32 = jnp.float32
_CD = jnp.bfloat16
_NEG = -1e30
_LOG2E = 1.4426950408889634
_INT_MAX = 2**31 - 1
_ATTN_TILE = 512
_PROJ_ROWS = 2 * _ATTN_TILE
_PROJ_COLS = 1024
_HEADS_PER_STEP = 2
_INTERP_PROBES = 40


def _params(*sem):
    return pltpu.CompilerParams(dimension_semantics=sem, vmem_limit_bytes=VMEM_LIMIT_BYTES)


def _rmsnorm_body(x_ref, g_ref, o_ref):
    x = x_ref[...].astype(F32)
    y = x * lax.rsqrt(jnp.mean(x * x, axis=-1, keepdims=True) + NORM_EPS)
    o_ref[...] = (y * g_ref[...]).astype(o_ref.dtype)


def _rmsnorm(x, g, out_dtype, tm=_ATTN_TILE):
    m, d = x.shape
    return pl.pallas_call(
        _rmsnorm_body,
        grid=(m // tm,),
        in_specs=[pl.BlockSpec((tm, d), lambda i: (i, 0)), pl.BlockSpec((1, d), lambda i: (0, 0))],
        out_specs=pl.BlockSpec((tm, d), lambda i: (i, 0)),
        out_shape=jax.ShapeDtypeStruct((m, d), out_dtype),
        compiler_params=_params("parallel"),
        name="rmsnorm",
    )(x, g.reshape(1, d).astype(F32))


def _cast_body(x_ref, o_ref):
    o_ref[...] = x_ref[...].astype(o_ref.dtype)


def _to_mxu_dtype(w, layer=None, tm=256):
    m, n = w.shape[-2:]
    if layer is None:
        spec = pl.BlockSpec((tm, n), lambda i: (i, 0))
    else:
        spec = pl.BlockSpec((None, tm, n), lambda i: (layer, i, 0))
    return pl.pallas_call(
        _cast_body,
        grid=(m // tm,),
        in_specs=[spec],
        out_specs=pl.BlockSpec((tm, n), lambda i: (i, 0)),
        out_shape=jax.ShapeDtypeStruct((m, n), _CD),
        compiler_params=_params("parallel"),
        name="weight_cast",
    )(w)


def _rope_slab(y, cos, sin):
    return y * cos + pltpu.roll(y, LANE // 2, 1) * sin


def _matmul_body(*refs, n_extra, epilogue, out_dtype):
    a_ref, w_ref = refs[0], refs[1]
    extra = refs[2:2 + n_extra]
    o_ref, acc_ref = refs[2 + n_extra], refs[3 + n_extra]
    k = pl.program_id(2)

    @pl.when(k == 0)
    def _init():
        acc_ref[...] = jnp.zeros_like(acc_ref)

    acc_ref[...] += jnp.dot(a_ref[...], w_ref[...], preferred_element_type=F32)

    @pl.when(k == pl.num_programs(2) - 1)
    def _finish():
        o_ref[...] = epilogue(acc_ref[...], *extra).astype(out_dtype)


def _matmul(a, w, *, tm, tn, tk, out_dtype, epilogue, extra=(), extra_specs=(), name):
    m, kd = a.shape
    n = w.shape[1]
    assert m % tm == 0 and n % tn == 0 and kd % tk == 0, (a.shape, w.shape, tm, tn, tk)
    body = functools.partial(_matmul_body, n_extra=len(extra), epilogue=epilogue, out_dtype=out_dtype)
    return pl.pallas_call(
        body,
        grid=(m // tm, n // tn, kd // tk),
        in_specs=[pl.BlockSpec((tm, tk), lambda i, j, k: (i, k)),
                  pl.BlockSpec((tk, tn), lambda i, j, k: (k, j))] + list(extra_specs),
        out_specs=pl.BlockSpec((tm, tn), lambda i, j, k: (i, j)),
        out_shape=jax.ShapeDtypeStruct((m, n), out_dtype),
        scratch_shapes=[pltpu.VMEM((tm, tn), F32)],
        compiler_params=_params("parallel", "parallel", "arbitrary"),
        name=name,
    )(a, w, *extra)


def _norm_matmul_body(*refs, n_extra, epilogue, out_dtype):
    x_ref, g_ref, w_ref = refs[:3]
    extra = refs[3:3 + n_extra]
    o_ref = refs[3 + n_extra]
    h_ref = refs[4 + n_extra]

    @pl.when(pl.program_id(1) == 0)
    def _norm():
        x = x_ref[...]
        y = x * lax.rsqrt(jnp.mean(x * x, axis=-1, keepdims=True) + NORM_EPS)
        h_ref[...] = (y * g_ref[...]).astype(h_ref.dtype)

    y = jnp.dot(h_ref[...], w_ref[...], preferred_element_type=F32)
    o_ref[...] = epilogue(y, *extra).astype(out_dtype)


def _norm_matmul(x, g, w, *, tm, tn, out_dtype, epilogue, extra=(), extra_specs=(), keep_h=False, name):
    m, d = x.shape
    n = w.shape[1]
    assert m % tm == 0 and n % tn == 0, (x.shape, w.shape, tm, tn)
    body = functools.partial(_norm_matmul_body, n_extra=len(extra), epilogue=epilogue, out_dtype=out_dtype)
    y_spec, y_shape = pl.BlockSpec((tm, tn), lambda i, j: (i, j)), jax.ShapeDtypeStruct((m, n), out_dtype)
    h_spec, h_shape = pl.BlockSpec((tm, d), lambda i, j: (i, 0)), jax.ShapeDtypeStruct((m, d), _CD)
    return pl.pallas_call(
        body,
        grid=(m // tm, n // tn),
        in_specs=[pl.BlockSpec((tm, d), lambda i, j: (i, 0)),
                  pl.BlockSpec((1, d), lambda i, j: (0, 0)),
                  pl.BlockSpec((d, tn), lambda i, j: (0, j))] + list(extra_specs),
        out_specs=(y_spec, h_spec) if keep_h else y_spec,
        out_shape=(y_shape, h_shape) if keep_h else y_shape,
        scratch_shapes=[] if keep_h else [pltpu.VMEM((tm, d), _CD)],
        compiler_params=_params("parallel", "arbitrary"),
        name=name,
    )(x, g.reshape(1, d).astype(F32), w, *extra)


def _ep_scale(scale):
    def ep(y):
        return y if scale == 1.0 else y * scale
    return ep


def _ep_relu2(y):
    return jnp.square(jnp.maximum(y, 0.0))


def _ep_residual(y, r_ref):
    return y + r_ref[...]


def _ep_rmsnorm_groups(widths):
    def ep(y, g_ref):
        out, lo = [], 0
        for w in widths:
            seg = y[:, lo:lo + w]
            out.append(seg * lax.rsqrt(jnp.mean(seg * seg, axis=-1, keepdims=True) + NORM_EPS) * g_ref[:, lo:lo + w])
            lo += w
        return jnp.concatenate(out, axis=1)
    return ep


def _ep_rope(pattern, plain_scale):
    def ep(y, cos_ref, sin_ref):
        cos, sin = cos_ref[0], sin_ref[0]
        out = []
        for c, rot in enumerate(pattern):
            slab = y[:, c * LANE:(c + 1) * LANE]
            out.append(_rope_slab(slab, cos, sin) if rot else slab * plain_scale)
        return jnp.concatenate(out, axis=1)
    return ep


def _rope_matmul(a, w, tabs, tab_of_tile, *, seq, tm, tn, pattern, plain_scale=1.0, norm_gain=None,
                 keep_h=False, name):
    nblk = seq // tm
    spec = pl.BlockSpec((1, tm, LANE), lambda i, j, *_: (tab_of_tile(j), i % nblk, 0))
    common = dict(tm=tm, tn=tn, out_dtype=_CD, epilogue=_ep_rope(pattern, plain_scale), extra=tabs,
                  extra_specs=(spec, spec), name=name)
    if norm_gain is None:
        return _matmul(a, w, tk=a.shape[1], **common)
    return _norm_matmul(a, norm_gain, w, keep_h=keep_h, **common)


def _attn_init(m_scr, l_scr, acc_scr):
    m_scr[...] = jnp.full(m_scr.shape, _NEG, F32)
    l_scr[...] = jnp.zeros(l_scr.shape, F32)
    acc_scr[...] = jnp.zeros(acc_scr.shape, F32)


def _tile_rows(j, blk):
    return pl.ds(pl.multiple_of(j * blk, blk), blk)


def _dot_tn(a, b):
    return lax.dot_general(a, b, (((0,), (0,)), ((), ())), preferred_element_type=F32)


def _key_major(q):
    return q.astype(F32).T.astype(q.dtype)


def _attn_scratch(blk, hg):
    per_head = [pltpu.VMEM((1, blk), F32), pltpu.VMEM((1, blk), F32), pltpu.VMEM((HEAD_DIM, blk), F32)]
    return [pltpu.VMEM((2, hg, blk, blk), F32)] + per_head * hg


def _head_state(scr, hg):
    return scr[0], [scr[1 + 3 * g:4 + 3 * g] for g in range(hg)]


def _attn_step(st, v, m_scr, l_scr, acc_scr):
    m_prev = m_scr[...]
    m_new = jnp.maximum(m_prev, jnp.max(st, axis=0, keepdims=True))
    alpha = jnp.exp2(m_prev - m_new)
    p = jnp.exp2(st - m_new)
    l_scr[...] = alpha * l_scr[...] + jnp.sum(p, axis=0, keepdims=True)
    acc_scr[...] = alpha * acc_scr[...] + _dot_tn(v, p.astype(v.dtype))
    m_scr[...] = m_new


def _attn_out(l_scr, acc_scr, dtype):
    return (acc_scr[...] / l_scr[...]).T.astype(dtype)


def _pipelined_key_tiles(last, scores, consume, first=0):
    n_past = last - first

    def pair(t, carry):
        j = first + 2 * t
        scores(j + 1, 1)
        consume(j, 0, False)
        scores(j + 2, 0)
        consume(j + 1, 1, False)
        return carry

    scores(first, 0)
    lax.fori_loop(0, n_past // 2, pair, 0)

    @pl.when(n_past % 2 == 1)
    def _odd():
        scores(last, 1)
        consume(last - 1, 0, False)
        consume(last, 1, True)

    @pl.when(n_past % 2 == 0)
    def _even():
        consume(last, 0, True)


def _causal(s):
    kpos = lax.broadcasted_iota(jnp.int32, s.shape, 0)
    qpos = lax.broadcasted_iota(jnp.int32, s.shape, 1)
    return jnp.where(kpos <= qpos, s, _NEG)


def _mla_body(q_ref, kn_ref, kp_ref, v_ref, o_ref, *scr, blk, hg):
    i = pl.program_id(2)
    s_scr, heads = _head_state(scr, hg)
    for st in heads:
        _attn_init(*st)
    qts = [_key_major(q_ref[0, :, g * 2 * LANE:(g + 1) * 2 * LANE]) for g in range(hg)]

    def scores(j, slot):
        rows = _tile_rows(j, blk)
        kp = kp_ref[0, rows, :]
        for g in range(hg):
            k = jnp.concatenate([kn_ref[0, rows, g * LANE:(g + 1) * LANE], kp], axis=1)
            s_scr[slot, g] = jnp.dot(k, qts[g], preferred_element_type=F32)

    def consume(j, slot, masked):
        rows = _tile_rows(j, blk)
        for g, st in enumerate(heads):
            s = s_scr[slot, g]
            _attn_step(_causal(s) if masked else s, v_ref[0, rows, g * LANE:(g + 1) * LANE], *st)

    _pipelined_key_tiles(i, scores, consume)
    for g, (_, l_scr, acc_scr) in enumerate(heads):
        o_ref[0, :, g * LANE:(g + 1) * LANE] = _attn_out(l_scr, acc_scr, o_ref.dtype)


def _mla_attention(q, kv, r64, *, blk, hg=_HEADS_PER_STEP):
    b, s, _ = q.shape
    h = MLA_HEADS
    return pl.pallas_call(
        functools.partial(_mla_body, blk=blk, hg=hg),
        grid=(b, h // hg, s // blk),
        in_specs=[pl.BlockSpec((1, blk, hg * 2 * LANE), lambda b_, h_, i: (b_, i, h_)),
                  pl.BlockSpec((1, s, hg * LANE), lambda b_, h_, i: (b_, 0, h_)),
                  pl.BlockSpec((1, s, LANE), lambda b_, h_, i: (b_, 0, _R64_KPE)),
                  pl.BlockSpec((1, s, hg * LANE), lambda b_, h_, i: (b_, 0, h // hg + h_))],
        out_specs=pl.BlockSpec((1, blk, hg * LANE), lambda b_, h_, i: (b_, i, h_)),
        out_shape=jax.ShapeDtypeStruct((b, s, h * MLA_V), _CD),
        scratch_shapes=_attn_scratch(blk, hg),
        compiler_params=_params("parallel", "parallel", "arbitrary"),
        name="mla_attention",
    )(q, kv, r64, kv)


def _order_key(x):
    bits = lax.bitcast_convert_type(x, jnp.int32)
    return bits ^ ((bits >> 31) & _INT_MAX)


def _key_score(key):
    return lax.bitcast_convert_type(key ^ ((key >> 31) & _INT_MAX), F32)


def _dsa_select(i, qi_ref, ka_ref, kb_ref, w_ref, sel_scr, jc_scr, *, blk, topk, seq):
    wt = w_ref[0].T
    qits = [_key_major(qi_ref[0, :, p * LANE:(p + 1) * LANE]) for p in range(IDX_HEADS // 2)]
    kloc = lax.broadcasted_iota(jnp.int32, (blk, blk), 0)
    qpos = i * blk + lax.broadcasted_iota(jnp.int32, (blk, blk), 1)

    def score_tile(c, carry):
        rows = _tile_rows(c, blk)
        ka, kb = ka_ref[0, rows, :], kb_ref[0, rows, :]
        acc = jnp.zeros((blk, blk), F32)
        for p in range(IDX_HEADS // 2):
            da = jnp.dot(ka, qits[p], preferred_element_type=F32)
            db = jnp.dot(kb, qits[p], preferred_element_type=F32)
            acc = acc + wt[2 * p:2 * p + 1, :] * jnp.maximum(da, 0.0) + wt[2 * p + 1:2 * p + 2, :] * jnp.maximum(db, 0.0)
        causal = c * blk + kloc <= qpos
        sel_scr[c] = jnp.where(causal, acc, -jnp.inf)
        lo8, hi8 = carry
        fold = lambda x: x.reshape(blk // SUBLANE, SUBLANE, blk)
        lo8 = jnp.minimum(lo8, jnp.min(fold(jnp.where(causal, acc, jnp.inf)), axis=0))
        hi8 = jnp.maximum(hi8, jnp.max(fold(jnp.where(causal, acc, -jnp.inf)), axis=0))
        return lo8, hi8

    lo8, hi8 = lax.fori_loop(0, i + 1, score_tile, (jnp.full((SUBLANE, blk), jnp.inf, F32),
                                                    jnp.full((SUBLANE, blk), -jnp.inf, F32)))
    s_min, s_max = lo8[0:1], hi8[0:1]
    for r in range(1, SUBLANE):
        s_min = jnp.minimum(s_min, lo8[r:r + 1])
        s_max = jnp.maximum(s_max, hi8[r:r + 1])

    n_causal = i * blk + lax.broadcasted_iota(jnp.int32, (1, blk), 1) + 1
    kk = jnp.minimum(topk, n_causal).astype(F32)

    def count(pred):
        def body(c, cnt):
            hit = jnp.where(pred(sel_scr[c], c * blk + kloc), 1.0, 0.0)
            return cnt + jnp.sum(hit.reshape(blk // SUBLANE, SUBLANE, blk), axis=0)
        cnt = lax.fori_loop(0, i + 1, body, jnp.zeros((SUBLANE, blk), F32))
        return jnp.sum(cnt, axis=0, keepdims=True)

    def unfinished(lo, hi, n_lo):
        return (n_lo > kk) & (hi > lo + 1)

    def probe(state):
        lo, hi, n_lo, n_hi, w_lo, w_hi, last, it, _ = state
        lo_f, hi_f = _key_score(lo), _key_score(hi)
        a, b = (n_lo - kk + 0.5) * w_lo, (kk - 0.5 - n_hi) * w_hi
        interp = _order_key(lo_f + (hi_f - lo_f) * (a / (a + b)))
        mid = (lo >> 1) + (hi >> 1) + (lo & hi & 1)
        cand = jnp.clip(jnp.where(it < _INTERP_PROBES, interp, mid), lo + 1, hi - 1)
        cand_f = _key_score(cand)
        n = count(lambda s, kpos: s >= cand_f)
        live = unfinished(lo, hi, n_lo)
        up = live & (n >= kk)
        down = live & (n < kk)
        lo, n_lo = jnp.where(up, cand, lo), jnp.where(up, n, n_lo)
        hi, n_hi = jnp.where(down, cand, hi), jnp.where(down, n, n_hi)
        w_hi = jnp.where(up, jnp.where(last > 0.5, 0.5 * w_hi, 1.0), jnp.where(down, 1.0, w_hi))
        w_lo = jnp.where(down, jnp.where(last < -0.5, 0.5 * w_lo, 1.0), jnp.where(up, 1.0, w_lo))
        last = jnp.where(up, 1.0, jnp.where(down, -1.0, last))
        more = jnp.max(jnp.where(unfinished(lo, hi, n_lo), 1.0, 0.0))
        return lo, hi, n_lo, n_hi, w_lo, w_hi, last, it + 1, more

    n_all = n_causal.astype(F32)
    one, zero = jnp.ones((1, blk), F32), jnp.zeros((1, blk), F32)
    margin = 0.125 * (jnp.abs(s_min) + jnp.abs(s_max))
    key_min, key_end = _order_key(s_min - margin) - 1, _order_key(s_max) + 1
    start = (key_min, key_end, n_all, zero, one, one, zero, jnp.int32(0),
             jnp.max(jnp.where(unfinished(key_min, key_end, n_all), 1.0, 0.0)))
    t_key, _, n_ge = lax.while_loop(lambda s: s[8] > 0.5, probe, start)[:3]
    t = _key_score(t_key)
    jc_scr[...] = jnp.full((1, blk), seq, jnp.int32)

    @pl.when(jnp.max(n_ge - kk) > 0.5)
    def _ties():
        need = kk - count(lambda s, kpos: s > t)
        nbits = (seq - 1).bit_length()

        def pos_step(b, jc):
            cand = jc + (jnp.int32(1) << (nbits - 1 - b))
            n_lt = count(lambda s, kpos: (s == t) & (kpos < cand))
            return jnp.where(n_lt < need, cand, jc)

        jc_scr[...] = lax.fori_loop(0, nbits, pos_step, jnp.zeros((1, blk), jnp.int32))

    jc = jc_scr[...]

    def to_bias(c, carry):
        s = sel_scr[c]
        chosen = (s > t) | ((s == t) & (c * blk + kloc <= jc))
        sel_scr[c] = jnp.where(chosen, 0.0, _NEG).astype(F32)
        return carry

    lax.fori_loop(0, i + 1, to_bias, 0)


def _dsa_body(qi_ref, ka_ref, kb_ref, w_ref, q_ref, k_ref, v_ref, o_ref, sel_scr, jc_scr, *scr,
              blk, hg, topk, seq):
    i = pl.program_id(1)

    @pl.when(pl.program_id(2) == 0)
    def _select():
        _dsa_select(i, qi_ref, ka_ref, kb_ref, w_ref, sel_scr, jc_scr, blk=blk, topk=topk, seq=seq)

    s_scr, heads = _head_state(scr, hg)
    for st in heads:
        _attn_init(*st)
    qts = [_key_major(q_ref[0, :, g * LANE:(g + 1) * LANE]) for g in range(hg)]

    def scores(j, slot):
        rows = _tile_rows(j, blk)
        bias = sel_scr[j]
        for g in range(hg):
            s_scr[slot, g] = bias + jnp.dot(k_ref[0, rows, g * LANE:(g + 1) * LANE], qts[g],
                                            preferred_element_type=F32)

    def consume(j, slot, masked):
        rows = _tile_rows(j, blk)
        for g, st in enumerate(heads):
            _attn_step(s_scr[slot, g], v_ref[0, rows, g * LANE:(g + 1) * LANE], *st)

    _pipelined_key_tiles(i, scores, consume)
    for g, (_, l_scr, acc_scr) in enumerate(heads):
        o_ref[0, :, g * LANE:(g + 1) * LANE] = _attn_out(l_scr, acc_scr, o_ref.dtype)


def _dsa_attention(r64, w_idx, qkv, *, blk, hg=_HEADS_PER_STEP):
    b, s, _ = qkv.shape
    h = DSA_HEADS
    topk = min(DSA_TOPK_MAX, s // 4)
    return pl.pallas_call(
        functools.partial(_dsa_body, blk=blk, hg=hg, topk=topk, seq=s),
        grid=(b, s // blk, h // hg),
        in_specs=[pl.BlockSpec((1, blk, 8 * LANE), lambda b_, i, h_: (b_, i, 0)),
                  pl.BlockSpec((1, s, LANE), lambda b_, i, h_: (b_, 0, _R64_KIA)),
                  pl.BlockSpec((1, s, LANE), lambda b_, i, h_: (b_, 0, _R64_KIB)),
                  pl.BlockSpec((1, blk, LANE), lambda b_, i, h_: (b_, i, 0)),
                  pl.BlockSpec((1, blk, hg * LANE), lambda b_, i, h_: (b_, i, h_)),
                  pl.BlockSpec((1, s, hg * LANE), lambda b_, i, h_: (b_, 0, h // hg + h_)),
                  pl.BlockSpec((1, s, hg * LANE), lambda b_, i, h_: (b_, 0, 2 * (h // hg) + h_))],
        out_specs=pl.BlockSpec((1, blk, hg * LANE), lambda b_, i, h_: (b_, i, h_)),
        out_shape=jax.ShapeDtypeStruct((b, s, h * HEAD_DIM), _CD),
        scratch_shapes=[pltpu.VMEM((s // blk, blk, blk), F32),
                        pltpu.VMEM((1, blk), jnp.int32)] + _attn_scratch(blk, hg),
        compiler_params=_params("parallel", "arbitrary", "arbitrary"),
        name="dsa_attention",
    )(r64, r64, r64, w_idx, qkv, qkv, qkv)


def _moba_body(q_ref, k_ref, v_ref, o_ref, kmean_scr, pick_scr, *scr, blk, hg, seq, nbp):
    i = pl.program_id(2)
    nper = blk // MOBA_BLOCK
    shift = MOBA_BLOCK.bit_length() - 1
    s_scr, heads = _head_state(scr, hg)

    @pl.when(i == 0)
    def _block_means():
        r = lax.broadcasted_iota(jnp.int32, (nbp, seq), 0)
        c = lax.broadcasted_iota(jnp.int32, (nbp, seq), 1)
        avg = jnp.where((c >> shift) == r, 1.0 / MOBA_BLOCK, 0.0).astype(k_ref.dtype)
        for g in range(hg):
            kmean_scr[g] = jnp.dot(avg, k_ref[0, :, g * LANE:(g + 1) * LANE],
                                   preferred_element_type=F32).astype(kmean_scr.dtype)

    qts = [_key_major(q_ref[0, :, g * LANE:(g + 1) * LANE]) for g in range(hg)]
    kblk = lax.broadcasted_iota(jnp.int32, (nbp, blk), 0)
    own = (i * blk + lax.broadcasted_iota(jnp.int32, (nbp, blk), 1)) >> shift
    kblk_f = kblk.astype(F32)
    for g, st in enumerate(heads):
        _attn_init(*st)
        gate = jnp.where(kblk < own, jnp.dot(kmean_scr[g], qts[g], preferred_element_type=F32), _NEG)
        pick = jnp.zeros((nbp, blk), F32)
        for _ in range(MOBA_TOPK):
            best = jnp.max(gate, axis=0, keepdims=True)
            hit = (gate == best) & (best > 0.5 * _NEG)
            first = jnp.min(jnp.where(hit, kblk_f, float(nbp)), axis=0, keepdims=True)
            new = kblk_f == first
            pick = jnp.where(new, 1.0, pick)
            gate = jnp.where(new, _NEG, gate)
        pick_scr[g] = jnp.where((pick > 0.5) | (kblk == own), 0.0, _NEG)

    def scores(j, slot):
        for g in range(hg):
            for u in range(nper):
                rows = pl.ds(pl.multiple_of(j * blk + u * MOBA_BLOCK, MOBA_BLOCK), MOBA_BLOCK)
                s = jnp.dot(k_ref[0, rows, g * LANE:(g + 1) * LANE], qts[g], preferred_element_type=F32)
                s_scr[slot, g, u * MOBA_BLOCK:(u + 1) * MOBA_BLOCK, :] = s + pick_scr[g, pl.ds(j * nper + u, 1), :]

    def consume(j, slot, masked):
        rows = _tile_rows(j, blk)
        for g, st in enumerate(heads):
            s = s_scr[slot, g]
            _attn_step(_causal(s) if masked else s, v_ref[0, rows, g * LANE:(g + 1) * LANE], *st)

    _pipelined_key_tiles(i, scores, consume)
    for g, (_, l_scr, acc_scr) in enumerate(heads):
        o_ref[0, :, g * LANE:(g + 1) * LANE] = _attn_out(l_scr, acc_scr, o_ref.dtype)


def _moba_attention(qk, v, *, blk, q_off, k_off, v_off, hg=_HEADS_PER_STEP):
    b, s, _ = qk.shape
    h = MOBA_HEADS
    assert s % MOBA_BLOCK == 0 and blk % MOBA_BLOCK == 0
    nbp = -(-(s // MOBA_BLOCK) // SUBLANE) * SUBLANE
    return pl.pallas_call(
        functools.partial(_moba_body, blk=blk, hg=hg, seq=s, nbp=nbp),
        grid=(b, h // hg, s // blk),
        in_specs=[pl.BlockSpec((1, blk, hg * LANE), lambda b_, h_, i: (b_, i, q_off // hg + h_)),
                  pl.BlockSpec((1, s, hg * LANE), lambda b_, h_, i: (b_, 0, k_off // hg + h_)),
                  pl.BlockSpec((1, s, hg * LANE), lambda b_, h_, i: (b_, 0, v_off // hg + h_))],
        out_specs=pl.BlockSpec((1, blk, hg * LANE), lambda b_, h_, i: (b_, i, h_)),
        out_shape=jax.ShapeDtypeStruct((b, s, h * HEAD_DIM), _CD),
        scratch_shapes=[pltpu.VMEM((hg, nbp, HEAD_DIM), _CD),
                        pltpu.VMEM((hg, nbp, blk), F32)] + _attn_scratch(blk, hg),
        compiler_params=_params("parallel", "parallel", "arbitrary"),
        name="moba_attention",
    )(qk, qk, v)


def _dilated_body(q_ref, k_ref, v_ref, bias_ref, o_ref, *scr, blk, hg, nrel):
    i = pl.program_id(2)
    s_scr, heads = _head_state(scr, hg)
    for st in heads:
        _attn_init(*st)
    qts = [_key_major(q_ref[0, :, g * LANE:(g + 1) * LANE]) for g in range(hg)]

    def scores(j, slot):
        rows = _tile_rows(j, blk)
        bias = bias_ref[i - j]
        for g in range(hg):
            s_scr[slot, g] = bias + jnp.dot(k_ref[0, rows, g * LANE:(g + 1) * LANE], qts[g],
                                            preferred_element_type=F32)

    def consume(j, slot, masked):
        rows = _tile_rows(j, blk)
        for g, st in enumerate(heads):
            _attn_step(s_scr[slot, g], v_ref[0, rows, g * LANE:(g + 1) * LANE], *st)

    _pipelined_key_tiles(i, scores, consume, first=jnp.maximum(i - (nrel - 1), 0))
    for g, (_, l_scr, acc_scr) in enumerate(heads):
        o_ref[0, :, g * LANE:(g + 1) * LANE] = _attn_out(l_scr, acc_scr, o_ref.dtype)


def _dilated_bias(blk):
    reach = max(w for w, _ in DIL_PATTERNS)
    nrel = -(-reach // blk) + 1
    rel = jnp.arange(nrel, dtype=jnp.int32)[:, None, None]
    k = jnp.arange(blk, dtype=jnp.int32)[None, :, None]
    q = jnp.arange(blk, dtype=jnp.int32)[None, None, :]
    d = rel * blk + q - k
    mult = jnp.zeros(d.shape, F32)
    for window, dil in DIL_PATTERNS:
        mult = mult + ((d >= 0) & (d <= (window // dil) * dil) & (d % dil == 0)).astype(F32)
    return jnp.where(mult > 0, jnp.log2(jnp.maximum(mult, 1.0)), _NEG), nrel


def _dilated_attention(qk, v, *, blk, q_off, k_off, v_off, hg=_HEADS_PER_STEP):
    b, s, _ = qk.shape
    h = DIL_HEADS
    bias, nrel = _dilated_bias(blk)
    return pl.pallas_call(
        functools.partial(_dilated_body, blk=blk, hg=hg, nrel=nrel),
        grid=(b, h // hg, s // blk),
        in_specs=[pl.BlockSpec((1, blk, hg * LANE), lambda b_, h_, i: (b_, i, q_off // hg + h_)),
                  pl.BlockSpec((1, s, hg * LANE), lambda b_, h_, i: (b_, 0, k_off // hg + h_)),
                  pl.BlockSpec((1, s, hg * LANE), lambda b_, h_, i: (b_, 0, v_off // hg + h_)),
                  pl.BlockSpec((nrel, blk, blk), lambda b_, h_, i: (0, 0, 0))],
        out_specs=pl.BlockSpec((1, blk, hg * LANE), lambda b_, h_, i: (b_, i, h_)),
        out_shape=jax.ShapeDtypeStruct((b, s, h * HEAD_DIM), _CD),
        scratch_shapes=_attn_scratch(blk, hg),
        compiler_params=_params("parallel", "parallel", "arbitrary"),
        name="dilated_attention",
    )(qk, qk, v, bias)


_E_CQ, _E_CKV, _E_KPE, _E_Q, _E_K, _E_V, _E_QI, _E_KI, _E_WI = [
    int(o) for o in np.cumsum([0, MLA_Q_RANK, MLA_KV_RANK, MLA_ROPE, DSA_HEADS * HEAD_DIM, DSA_HEADS * HEAD_DIM,
                               DSA_HEADS * HEAD_DIM, IDX_HEADS * IDX_DIM, IDX_DIM])]
_R64_KIA, _R64_KIB, _R64_KPE, _R64_SLABS = 8, 9, 10, 12


def _take_cols(w, idx):
    idx = np.asarray(idx)
    cols = jnp.take(w, jnp.asarray(np.maximum(idx, 0)), axis=1)
    return jnp.where(jnp.asarray(idx >= 0)[None, :], cols, 0.0).astype(_CD)


def _r64_columns():
    half = IDX_DIM // 2
    a = np.arange(half)
    z = -np.ones(half, np.int64)
    cols = []
    for p in range(IDX_HEADS // 2):
        ha, hb = _E_QI + 2 * p * IDX_DIM, _E_QI + (2 * p + 1) * IDX_DIM
        cols += [ha + a, hb + a, ha + half + a, hb + half + a]
    cols += [_E_KI + a, z, _E_KI + half + a, z]
    cols += [z, _E_KI + a, z, _E_KI + half + a]
    cols += [_E_KPE + a, z, _E_KPE + half + a, z]
    cols += [z, z, z, z]
    return np.concatenate(cols)


def _uq_columns():
    half = MLA_ROPE // 2
    a = np.arange(half)
    z = -np.ones(half, np.int64)
    cols = []
    for h in range(MLA_HEADS):
        o = h * (MLA_NOPE + MLA_ROPE)
        cols += [o + np.arange(MLA_NOPE), o + MLA_NOPE + a, z, o + MLA_NOPE + half + a, z]
    return np.concatenate(cols)


def _ukv_columns():
    per = MLA_NOPE + MLA_V
    kn = [h * per + np.arange(MLA_NOPE) for h in range(MLA_HEADS)]
    vv = [h * per + MLA_NOPE + np.arange(MLA_V) for h in range(MLA_HEADS)]
    return np.concatenate(kn + vv)


def _rope_tables(seq, dim, scales, with_identity=False):
    inv = ROPE_THETA ** (-jnp.arange(0, dim, 2, dtype=F32) / dim)
    ang = jnp.arange(seq, dtype=F32)[:, None] * inv[None, :]
    reps = (LANE // 2) // (dim // 2)
    cos = jnp.tile(jnp.cos(ang), (1, 2 * reps))
    sin = jnp.tile(jnp.sin(ang), (1, reps))
    sin = jnp.concatenate([-sin, sin], axis=1)
    sc = jnp.asarray(scales, F32)[:, None, None]
    cos, sin = cos[None] * sc, sin[None] * sc
    if with_identity:
        cos = jnp.concatenate([cos, jnp.ones((1, seq, LANE), F32)], axis=0)
        sin = jnp.concatenate([sin, jnp.zeros((1, seq, LANE), F32)], axis=0)
    return cos, sin


def _mlp_block(x2, g, w1, w2, *, tm):
    tn = _PROJ_COLS
    up = _norm_matmul(x2, g, _to_mxu_dtype(*w1), tm=tm, tn=tn, out_dtype=_CD, epilogue=_ep_relu2, name="mlp_up")
    res = pl.BlockSpec((tm, tn), lambda i, j, k: (i, j))
    return _matmul(up, _to_mxu_dtype(*w2), tm=tm, tn=tn, tk=2 * tn, out_dtype=F32,
                   epilogue=_ep_residual, extra=(x2,), extra_specs=(res,), name="mlp_down")


def _out_proj_body(a_ref, b_ref, w_ref, r_ref, o_ref):
    half = a_ref.shape[1]
    y = jnp.dot(a_ref[...], w_ref[:half, :], preferred_element_type=F32)
    y = y + jnp.dot(b_ref[...], w_ref[half:, :], preferred_element_type=F32)
    o_ref[...] = y + r_ref[...]


def _out_proj(a, b, w_out, x2, *, tm, tn=_PROJ_COLS):
    t, half = a.shape
    d = x2.shape[1]
    return pl.pallas_call(
        _out_proj_body,
        grid=(t // tm, d // tn),
        in_specs=[pl.BlockSpec((tm, half), lambda i, j: (i, 0)),
                  pl.BlockSpec((tm, half), lambda i, j: (i, 0)),
                  pl.BlockSpec((2 * half, tn), lambda i, j: (0, j)),
                  pl.BlockSpec((tm, tn), lambda i, j: (i, j))],
        out_specs=pl.BlockSpec((tm, tn), lambda i, j: (i, j)),
        out_shape=jax.ShapeDtypeStruct((t, d), F32),
        compiler_params=_params("parallel", "parallel"),
        name="out_proj",
    )(a, b, _to_mxu_dtype(*w_out), x2)


def _even_mixer(x2, g_mix, w_in, g_q, g_kv, w_uq, w_ukv, w_out, *, batch, seq, blk):
    t, d = x2.shape
    tm = min(_PROJ_ROWS, seq)
    mla_scale = _LOG2E * (MLA_NOPE + MLA_ROPE) ** -0.5
    rope64 = _rope_tables(seq, IDX_DIM, (1.0, mla_scale))
    rope128 = _rope_tables(seq, HEAD_DIM, (_LOG2E * HEAD_DIM ** -0.5, 1.0), with_identity=True)
    nh = DSA_HEADS * HEAD_DIM

    qkv, hm = _rope_matmul(x2, w_in[:, _E_Q:_E_QI].astype(_CD), rope128, lambda j: j, seq=seq, tm=tm, tn=nh,
                           pattern=(True,) * DSA_HEADS, norm_gain=g_mix, keep_h=True, name="in_proj_qkv")
    r64 = _rope_matmul(hm, _take_cols(w_in, _r64_columns()), rope64, lambda j: 0, seq=seq, tm=tm,
                       tn=_R64_SLABS // 2 * LANE, pattern=(True,) * (_R64_SLABS // 2), name="in_proj_rope64")
    w_idx = _matmul(hm, _take_cols(w_in, np.concatenate([_E_WI + np.arange(IDX_HEADS),
                                                          -np.ones(LANE - IDX_HEADS, np.int64)])),
                    tm=tm, tn=LANE, tk=d, out_dtype=F32,
                    epilogue=_ep_scale(IDX_HEADS ** -0.5 * IDX_DIM ** -0.5), name="in_proj_widx")
    ranks = (MLA_Q_RANK, MLA_KV_RANK)
    lat = _matmul(hm, w_in[:, _E_CQ:_E_KPE].astype(_CD), tm=tm, tn=sum(ranks), tk=d, out_dtype=_CD,
                  epilogue=_ep_rmsnorm_groups(ranks), extra=(jnp.concatenate([g_q, g_kv]).reshape(1, -1),),
                  extra_specs=(pl.BlockSpec((1, sum(ranks)), lambda i, j, k: (0, 0)),), name="in_proj_latents")

    q_mla = _rope_matmul(lat[:, :MLA_Q_RANK], _take_cols(w_uq, _uq_columns()), rope64, lambda j: 1, seq=seq,
                         tm=tm, tn=_PROJ_COLS, pattern=(False, True) * (_PROJ_COLS // (2 * LANE)),
                         plain_scale=mla_scale, name="mla_q_up")
    kv_mla = _matmul(lat[:, MLA_Q_RANK:], _take_cols(w_ukv, _ukv_columns()), tm=tm, tn=_PROJ_COLS, tk=MLA_KV_RANK,
                     out_dtype=_CD, epilogue=_ep_scale(1.0), name="mla_kv_up")

    sh = lambda z: z.reshape(batch, seq, z.shape[-1])
    a = _mla_attention(sh(q_mla), sh(kv_mla), sh(r64), blk=blk)
    bsa = _dsa_attention(sh(r64), sh(w_idx), sh(qkv), blk=blk)
    return _out_proj(a.reshape(t, -1), bsa.reshape(t, -1), w_out, x2, tm=tm)


def _odd_mixer(x2, g_mix, w_in, w_out, *, batch, seq, blk):
    t, d = x2.shape
    tm = min(_PROJ_ROWS, seq)
    nh = MOBA_HEADS * HEAD_DIM
    rope128 = _rope_tables(seq, HEAD_DIM, (_LOG2E * HEAD_DIM ** -0.5, 1.0), with_identity=True)
    qkv = _rope_matmul(x2, _to_mxu_dtype(*w_in), rope128, lambda j: j % 3, seq=seq, tm=tm, tn=nh,
                       pattern=(True,) * MOBA_HEADS, norm_gain=g_mix, name="in_proj_odd")
    qkv = qkv.reshape(batch, seq, qkv.shape[-1])
    hs = MOBA_HEADS
    c = _moba_attention(qkv, qkv, blk=blk, q_off=0, k_off=hs, v_off=2 * hs)
    dl = _dilated_attention(qkv, qkv, blk=blk, q_off=3 * hs, k_off=4 * hs, v_off=5 * hs)
    return _out_proj(c.reshape(t, -1), dl.reshape(t, -1), w_out, x2, tm=tm)


def kernel(x, ln_mix, ln_mlp, ln_final, e_w_in, e_g_q, e_g_kv, e_w_uq, e_w_ukv, e_w_out,
           o_w_in, o_w_out, mlp_w1, mlp_w2):
    batch, seq, d = x.shape
    blk = min(_ATTN_TILE, seq)
    assert seq % blk == 0 and blk % MOBA_BLOCK == 0
    x2 = x.reshape(batch * seq, d)
    depth = ln_mix.shape[0]
    for layer in range(depth):
        j = layer // 2
        if layer % 2 == 0:
            x2 = _even_mixer(x2, ln_mix[layer], e_w_in[j], e_g_q[j], e_g_kv[j], e_w_uq[j], e_w_ukv[j],
                             (e_w_out, j), batch=batch, seq=seq, blk=blk)
        else:
            x2 = _odd_mixer(x2, ln_mix[layer], (o_w_in, j), (o_w_out, j), batch=batch, seq=seq, blk=blk)
        x2 = _mlp_block(x2, ln_mlp[layer], (mlp_w1, layer), (mlp_w2, layer), tm=min(_PROJ_ROWS, batch * seq))
    return _rmsnorm(x2, ln_final, x.dtype).reshape(batch, seq, d)
```

```python
import functools

import numpy as np
import jax
import jax.numpy as jnp
from jax import lax
from jax.experimental import pallas as pl
from jax.experimental.pallas import tpu as pltpu

HEAD_DIM = 128
ROPE_THETA = 10000.0
NORM_EPS = 1e-6
MLA_HEADS, MLA_Q_RANK, MLA_KV_RANK, MLA_NOPE, MLA_ROPE, MLA_V = 8, 512, 256, 128, 64, 128
DSA_HEADS, IDX_HEADS, IDX_DIM, DSA_TOPK_MAX = 8, 16, 64, 256
MOBA_HEADS, MOBA_BLOCK, MOBA_TOPK = 8, 256, 3
DIL_HEADS = 8
DIL_PATTERNS = ((128, 1), (512, 4), (2048, 16))

LANE = 128
SUBLANE = 8
VMEM_LIMIT_BYTES = 56 * 2**20

F32 = jnp.float32
_CD = jnp.bfloat16
_NEG = -1e30
_LOG2E = 1.4426950408889634
_INT_MAX = 2**31 - 1
_ATTN_TILE = 512
_PROJ_ROWS = 2 * _ATTN_TILE
_PROJ_COLS = 1024
_HEADS_PER_STEP = 2
_INTERP_PROBES = 40


def _params(*sem):
    return pltpu.CompilerParams(dimension_semantics=sem, vmem_limit_bytes=VMEM_LIMIT_BYTES)


def _rmsnorm_body(x_ref, g_ref, o_ref):
    x = x_ref[...].astype(F32)
    y = x * lax.rsqrt(jnp.mean(x * x, axis=-1, keepdims=True) + NORM_EPS)
    o_ref[...] = (y * g_ref[...]).astype(o_ref.dtype)


def _rmsnorm(x, g, out_dtype, tm=_ATTN_TILE):
    m, d = x.shape
    return pl.pallas_call(
        _rmsnorm_body,
        grid=(m // tm,),
        in_specs=[pl.BlockSpec((tm, d), lambda i: (i, 0)), pl.BlockSpec((1, d), lambda i: (0, 0))],
        out_specs=pl.BlockSpec((tm, d), lambda i: (i, 0)),
        out_shape=jax.ShapeDtypeStruct((m, d), out_dtype),
        compiler_params=_params("parallel"),
        name="rmsnorm",
    )(x, g.reshape(1, d).astype(F32))


def _cast_body(x_ref, o_ref):
    o_ref[...] = x_ref[...].astype(o_ref.dtype)


def _to_mxu_dtype(w, layer=None, tm=256):
    m, n = w.shape[-2:]
    if layer is None:
        spec = pl.BlockSpec((tm, n), lambda i: (i, 0))
    else:
        spec = pl.BlockSpec((None, tm, n), lambda i: (layer, i, 0))
    return pl.pallas_call(
        _cast_body,
        grid=(m // tm,),
        in_specs=[spec],
        out_specs=pl.BlockSpec((tm, n), lambda i: (i, 0)),
        out_shape=jax.ShapeDtypeStruct((m, n), _CD),
        compiler_params=_params("parallel"),
        name="weight_cast",
    )(w)


def _rope_slab(y, cos, sin):
    return y * cos + pltpu.roll(y, LANE // 2, 1) * sin


def _matmul_body(*refs, n_extra, epilogue, out_dtype):
    a_ref, w_ref = refs[0], refs[1]
    extra = refs[2:2 + n_extra]
    o_ref, acc_ref = refs[2 + n_extra], refs[3 + n_extra]
    k = pl.program_id(2)

    @pl.when(k == 0)
    def _init():
        acc_ref[...] = jnp.zeros_like(acc_ref)

    acc_ref[...] += jnp.dot(a_ref[...], w_ref[...], preferred_element_type=F32)

    @pl.when(k == pl.num_programs(2) - 1)
    def _finish():
        o_ref[...] = epilogue(acc_ref[...], *extra).astype(out_dtype)


def _matmul(a, w, *, tm, tn, tk, out_dtype, epilogue, extra=(), extra_specs=(), a_col=0, name):
    m = a.shape[0]
    kd, n = w.shape
    assert m % tm == 0 and n % tn == 0 and kd % tk == 0 and a_col % tk == 0, (a.shape, w.shape, tm, tn, tk)
    body = functools.partial(_matmul_body, n_extra=len(extra), epilogue=epilogue, out_dtype=out_dtype)
    k0 = a_col // tk
    return pl.pallas_call(
        body,
        grid=(m // tm, n // tn, kd // tk),
        in_specs=[pl.BlockSpec((tm, tk), lambda i, j, k: (i, k0 + k)),
                  pl.BlockSpec((tk, tn), lambda i, j, k: (k, j))] + list(extra_specs),
        out_specs=pl.BlockSpec((tm, tn), lambda i, j, k: (i, j)),
        out_shape=jax.ShapeDtypeStruct((m, n), out_dtype),
        scratch_shapes=[pltpu.VMEM((tm, tn), F32)],
        compiler_params=_params("parallel", "parallel", "arbitrary"),
        name=name,
    )(a, w, *extra)


def _norm_matmul_body(*refs, n_extra, epilogue, out_dtype):
    x_ref, g_ref, w_ref = refs[:3]
    extra = refs[3:3 + n_extra]
    o_ref = refs[3 + n_extra]
    h_ref = refs[4 + n_extra]

    @pl.when(pl.program_id(1) == 0)
    def _norm():
        x = x_ref[...]
        y = x * lax.rsqrt(jnp.mean(x * x, axis=-1, keepdims=True) + NORM_EPS)
        h_ref[...] = (y * g_ref[...]).astype(h_ref.dtype)

    y = jnp.dot(h_ref[...], w_ref[...], preferred_element_type=F32)
    o_ref[...] = epilogue(y, *extra).astype(out_dtype)


def _norm_matmul(x, g, w, *, tm, tn, out_dtype, epilogue, extra=(), extra_specs=(), keep_h=False, name):
    m, d = x.shape
    n = w.shape[1]
    assert m % tm == 0 and n % tn == 0, (x.shape, w.shape, tm, tn)
    body = functools.partial(_norm_matmul_body, n_extra=len(extra), epilogue=epilogue, out_dtype=out_dtype)
    y_spec, y_shape = pl.BlockSpec((tm, tn), lambda i, j: (i, j)), jax.ShapeDtypeStruct((m, n), out_dtype)
    h_spec, h_shape = pl.BlockSpec((tm, d), lambda i, j: (i, 0)), jax.ShapeDtypeStruct((m, d), _CD)
    return pl.pallas_call(
        body,
        grid=(m // tm, n // tn),
        in_specs=[pl.BlockSpec((tm, d), lambda i, j: (i, 0)),
                  pl.BlockSpec((1, d), lambda i, j: (0, 0)),
                  pl.BlockSpec((d, tn), lambda i, j: (0, j))] + list(extra_specs),
        out_specs=(y_spec, h_spec) if keep_h else y_spec,
        out_shape=(y_shape, h_shape) if keep_h else y_shape,
        scratch_shapes=[] if keep_h else [pltpu.VMEM((tm, d), _CD)],
        compiler_params=_params("parallel", "arbitrary"),
        name=name,
    )(x, g.reshape(1, d).astype(F32), w, *extra)


def _ep_scale(scale):
    def ep(y):
        return y if scale == 1.0 else y * scale
    return ep


def _ep_relu2(y):
    return jnp.square(jnp.maximum(y, 0.0))


def _ep_residual(y, r_ref):
    return y + r_ref[...]


def _ep_rmsnorm_groups(widths):
    def ep(y, g_ref):
        out, lo = [], 0
        for w in widths:
            seg = y[:, lo:lo + w]
            out.append(seg * lax.rsqrt(jnp.mean(seg * seg, axis=-1, keepdims=True) + NORM_EPS) * g_ref[:, lo:lo + w])
            lo += w
        return jnp.concatenate(out, axis=1)
    return ep


def _ep_rope(pattern, plain_scale):
    def ep(y, cos_ref, sin_ref):
        cos, sin = cos_ref[0], sin_ref[0]
        out = []
        for c, rot in enumerate(pattern):
            slab = y[:, c * LANE:(c + 1) * LANE]
            out.append(_rope_slab(slab, cos, sin) if rot else slab * plain_scale)
        return jnp.concatenate(out, axis=1)
    return ep


def _rope_matmul(a, w, tabs, tab_of_tile, *, seq, tm, tn, pattern, plain_scale=1.0, norm_gain=None,
                 keep_h=False, name):
    nblk = seq // tm
    spec = pl.BlockSpec((1, tm, LANE), lambda i, j, *_: (tab_of_tile(j), i % nblk, 0))
    common = dict(tm=tm, tn=tn, out_dtype=_CD, epilogue=_ep_rope(pattern, plain_scale), extra=tabs,
                  extra_specs=(spec, spec), name=name)
    if norm_gain is None:
        return _matmul(a, w, tk=w.shape[0], **common)
    return _norm_matmul(a, norm_gain, w, keep_h=keep_h, **common)


def _attn_init(m_scr, l_scr, acc_scr):
    m_scr[...] = jnp.full(m_scr.shape, _NEG, F32)
    l_scr[...] = jnp.zeros(l_scr.shape, F32)
    acc_scr[...] = jnp.zeros(acc_scr.shape, F32)


def _tile_rows(j, blk):
    return pl.ds(pl.multiple_of(j * blk, blk), blk)


def _dot_tn(a, b):
    return lax.dot_general(a, b, (((0,), (0,)), ((), ())), preferred_element_type=F32)


def _key_major(q):
    return q.astype(F32).T.astype(q.dtype)


def _attn_scratch(blk, hg):
    per_head = [pltpu.VMEM((1, blk), F32), pltpu.VMEM((1, blk), F32), pltpu.VMEM((HEAD_DIM, blk), F32)]
    return [pltpu.VMEM((2, hg, blk, blk), F32)] + per_head * hg


def _head_state(scr, hg):
    return scr[0], [scr[1 + 3 * g:4 + 3 * g] for g in range(hg)]


def _attn_step(st, v, m_scr, l_scr, acc_scr):
    m_prev = m_scr[...]
    m_new = jnp.maximum(m_prev, jnp.max(st, axis=0, keepdims=True))
    alpha = jnp.exp2(m_prev - m_new)
    p = jnp.exp2(st - m_new)
    l_scr[...] = alpha * l_scr[...] + jnp.sum(p, axis=0, keepdims=True)
    acc_scr[...] = alpha * acc_scr[...] + _dot_tn(v, p.astype(v.dtype))
    m_scr[...] = m_new


def _attn_out(l_scr, acc_scr, dtype):
    return (acc_scr[...] / l_scr[...]).T.astype(dtype)


def _pipelined_key_tiles(last, scores, consume, first=0):
    n_past = last - first

    def pair(t, carry):
        j = first + 2 * t
        scores(j + 1, 1)
        consume(j, 0, False)
        scores(j + 2, 0)
        consume(j + 1, 1, False)
        return carry

    scores(first, 0)
    lax.fori_loop(0, n_past // 2, pair, 0)

    @pl.when(n_past % 2 == 1)
    def _odd():
        scores(last, 1)
        consume(last - 1, 0, False)
        consume(last, 1, True)

    @pl.when(n_past % 2 == 0)
    def _even():
        consume(last, 0, True)


def _causal(s):
    kpos = lax.broadcasted_iota(jnp.int32, s.shape, 0)
    qpos = lax.broadcasted_iota(jnp.int32, s.shape, 1)
    return jnp.where(kpos <= qpos, s, _NEG)


def _mla_body(q_ref, kn_ref, kp_ref, v_ref, o_ref, *scr, blk, hg):
    i = pl.program_id(2)
    s_scr, heads = _head_state(scr, hg)
    for st in heads:
        _attn_init(*st)
    qts = [_key_major(q_ref[0, :, g * 2 * LANE:(g + 1) * 2 * LANE]) for g in range(hg)]

    def scores(j, slot):
        rows = _tile_rows(j, blk)
        kp = kp_ref[0, rows, :]
        for g in range(hg):
            k = jnp.concatenate([kn_ref[0, rows, g * LANE:(g + 1) * LANE], kp], axis=1)
            s_scr[slot, g] = jnp.dot(k, qts[g], preferred_element_type=F32)

    def consume(j, slot, masked):
        rows = _tile_rows(j, blk)
        for g, st in enumerate(heads):
            s = s_scr[slot, g]
            _attn_step(_causal(s) if masked else s, v_ref[0, rows, g * LANE:(g + 1) * LANE], *st)

    _pipelined_key_tiles(i, scores, consume)
    for g, (_, l_scr, acc_scr) in enumerate(heads):
        o_ref[0, :, g * LANE:(g + 1) * LANE] = _attn_out(l_scr, acc_scr, o_ref.dtype)


def _mla_attention(q, kv, r64, *, blk, hg=2 * _HEADS_PER_STEP):
    b, s, _ = q.shape
    h = MLA_HEADS
    return pl.pallas_call(
        functools.partial(_mla_body, blk=blk, hg=hg),
        grid=(b, h // hg, s // blk),
        in_specs=[pl.BlockSpec((1, blk, hg * 2 * LANE), lambda b_, h_, i: (b_, i, h_)),
                  pl.BlockSpec((1, s, hg * LANE), lambda b_, h_, i: (b_, 0, h_)),
                  pl.BlockSpec((1, s, LANE), lambda b_, h_, i: (b_, 0, _R64_KPE)),
                  pl.BlockSpec((1, s, hg * LANE), lambda b_, h_, i: (b_, 0, h // hg + h_))],
        out_specs=pl.BlockSpec((1, blk, hg * LANE), lambda b_, h_, i: (b_, i, h_)),
        out_shape=jax.ShapeDtypeStruct((b, s, h * MLA_V), _CD),
        scratch_shapes=_attn_scratch(blk, hg),
        compiler_params=_params("parallel", "parallel", "arbitrary"),
        name="mla_attention",
    )(q, kv, r64, kv)


def _order_key(x):
    bits = lax.bitcast_convert_type(x, jnp.int32)
    return bits ^ ((bits >> 31) & _INT_MAX)


def _key_score(key):
    return lax.bitcast_convert_type(key ^ ((key >> 31) & _INT_MAX), F32)


def _dsa_select(i, qi_ref, ka_ref, kb_ref, w_ref, sel_scr, jc_scr, *, blk, topk, seq):
    wt = w_ref[0].T
    qits = [_key_major(qi_ref[0, :, p * LANE:(p + 1) * LANE]) for p in range(IDX_HEADS // 2)]
    kloc = lax.broadcasted_iota(jnp.int32, (blk, blk), 0)
    qpos = i * blk + lax.broadcasted_iota(jnp.int32, (blk, blk), 1)

    def score_tile(c, carry):
        rows = _tile_rows(c, blk)
        ka, kb = ka_ref[0, rows, :], kb_ref[0, rows, :]
        acc = jnp.zeros((blk, blk), F32)
        for p in range(IDX_HEADS // 2):
            da = jnp.dot(ka, qits[p], preferred_element_type=F32)
            db = jnp.dot(kb, qits[p], preferred_element_type=F32)
            acc = acc + wt[2 * p:2 * p + 1, :] * jnp.maximum(da, 0.0) + wt[2 * p + 1:2 * p + 2, :] * jnp.maximum(db, 0.0)
        causal = c * blk + kloc <= qpos
        sel_scr[c] = jnp.where(causal, acc, -jnp.inf)
        lo8, hi8 = carry
        fold = lambda x: x.reshape(blk // SUBLANE, SUBLANE, blk)
        lo8 = jnp.minimum(lo8, jnp.min(fold(jnp.where(causal, acc, jnp.inf)), axis=0))
        hi8 = jnp.maximum(hi8, jnp.max(fold(jnp.where(causal, acc, -jnp.inf)), axis=0))
        return lo8, hi8

    lo8, hi8 = lax.fori_loop(0, i + 1, score_tile, (jnp.full((SUBLANE, blk), jnp.inf, F32),
                                                    jnp.full((SUBLANE, blk), -jnp.inf, F32)))
    s_min, s_max = lo8[0:1], hi8[0:1]
    for r in range(1, SUBLANE):
        s_min = jnp.minimum(s_min, lo8[r:r + 1])
        s_max = jnp.maximum(s_max, hi8[r:r + 1])

    n_causal = i * blk + lax.broadcasted_iota(jnp.int32, (1, blk), 1) + 1
    kk = jnp.minimum(topk, n_causal).astype(F32)

    def count(pred):
        def body(c, cnt):
            hit = jnp.where(pred(sel_scr[c], c * blk + kloc), 1.0, 0.0)
            return cnt + jnp.sum(hit.reshape(blk // SUBLANE, SUBLANE, blk), axis=0)
        cnt = lax.fori_loop(0, i + 1, body, jnp.zeros((SUBLANE, blk), F32))
        return jnp.sum(cnt, axis=0, keepdims=True)

    def unfinished(lo, hi, n_lo):
        return (n_lo > kk) & (hi > lo + 1)

    def probe(state):
        lo, hi, n_lo, n_hi, w_lo, w_hi, last, it, _ = state
        lo_f, hi_f = _key_score(lo), _key_score(hi)
        a, b = (n_lo - kk + 0.5) * w_lo, (kk - 0.5 - n_hi) * w_hi
        interp = _order_key(lo_f + (hi_f - lo_f) * (a / (a + b)))
        mid = (lo >> 1) + (hi >> 1) + (lo & hi & 1)
        cand = jnp.clip(jnp.where(it < _INTERP_PROBES, interp, mid), lo + 1, hi - 1)
        cand_f = _key_score(cand)
        n = count(lambda s, kpos: s >= cand_f)
        live = unfinished(lo, hi, n_lo)
        up = live & (n >= kk)
        down = live & (n < kk)
        lo, n_lo = jnp.where(up, cand, lo), jnp.where(up, n, n_lo)
        hi, n_hi = jnp.where(down, cand, hi), jnp.where(down, n, n_hi)
        w_hi = jnp.where(up, jnp.where(last > 0.5, 0.5 * w_hi, 1.0), jnp.where(down, 1.0, w_hi))
        w_lo = jnp.where(down, jnp.where(last < -0.5, 0.5 * w_lo, 1.0), jnp.where(up, 1.0, w_lo))
        last = jnp.where(up, 1.0, jnp.where(down, -1.0, last))
        more = jnp.max(jnp.where(unfinished(lo, hi, n_lo), 1.0, 0.0))
        return lo, hi, n_lo, n_hi, w_lo, w_hi, last, it + 1, more

    n_all = n_causal.astype(F32)
    one, zero = jnp.ones((1, blk), F32), jnp.zeros((1, blk), F32)
    margin = 0.125 * (jnp.abs(s_min) + jnp.abs(s_max))
    key_min, key_end = _order_key(s_min - margin) - 1, _order_key(s_max) + 1
    start = (key_min, key_end, n_all, zero, one, one, zero, jnp.int32(0),
             jnp.max(jnp.where(unfinished(key_min, key_end, n_all), 1.0, 0.0)))
    t_key, _, n_ge = lax.while_loop(lambda s: s[8] > 0.5, probe, start)[:3]
    t = _key_score(t_key)
    jc_scr[...] = jnp.full((1, blk), seq, jnp.int32)

    @pl.when(jnp.max(n_ge - kk) > 0.5)
    def _ties():
        need = kk - count(lambda s, kpos: s > t)
        nbits = (seq - 1).bit_length()

        def pos_step(b, jc):
            cand = jc + (jnp.int32(1) << (nbits - 1 - b))
            n_lt = count(lambda s, kpos: (s == t) & (kpos < cand))
            return jnp.where(n_lt < need, cand, jc)

        jc_scr[...] = lax.fori_loop(0, nbits, pos_step, jnp.zeros((1, blk), jnp.int32))

    jc = jc_scr[...]

    def to_bias(c, carry):
        s = sel_scr[c]
        chosen = (s > t) | ((s == t) & (c * blk + kloc <= jc))
        sel_scr[c] = jnp.where(chosen, 0.0, _NEG).astype(F32)
        return carry

    lax.fori_loop(0, i + 1, to_bias, 0)


def _dsa_body(qi_ref, ka_ref, kb_ref, w_ref, q_ref, k_ref, v_ref, o_ref, sel_scr, jc_scr, *scr,
              blk, hg, topk, seq):
    i = pl.program_id(1)

    @pl.when(pl.program_id(2) == 0)
    def _select():
        _dsa_select(i, qi_ref, ka_ref, kb_ref, w_ref, sel_scr, jc_scr, blk=blk, topk=topk, seq=seq)

    s_scr, heads = _head_state(scr, hg)
    for st in heads:
        _attn_init(*st)
    qts = [_key_major(q_ref[0, :, g * LANE:(g + 1) * LANE]) for g in range(hg)]

    def scores(j, slot):
        rows = _tile_rows(j, blk)
        bias = sel_scr[j]
        for g in range(hg):
            s_scr[slot, g] = bias + jnp.dot(k_ref[0, rows, g * LANE:(g + 1) * LANE], qts[g],
                                            preferred_element_type=F32)

    def consume(j, slot, masked):
        rows = _tile_rows(j, blk)
        for g, st in enumerate(heads):
            _attn_step(s_scr[slot, g], v_ref[0, rows, g * LANE:(g + 1) * LANE], *st)

    _pipelined_key_tiles(i, scores, consume)
    for g, (_, l_scr, acc_scr) in enumerate(heads):
        o_ref[0, :, g * LANE:(g + 1) * LANE] = _attn_out(l_scr, acc_scr, o_ref.dtype)


def _dsa_attention(r64, w_idx, qkv, *, blk, hg=_HEADS_PER_STEP):
    b, s, _ = qkv.shape
    h = DSA_HEADS
    topk = min(DSA_TOPK_MAX, s // 4)
    return pl.pallas_call(
        functools.partial(_dsa_body, blk=blk, hg=hg, topk=topk, seq=s),
        grid=(b, s // blk, h // hg),
        in_specs=[pl.BlockSpec((1, blk, 8 * LANE), lambda b_, i, h_: (b_, i, 0)),
                  pl.BlockSpec((1, s, LANE), lambda b_, i, h_: (b_, 0, _R64_KIA)),
                  pl.BlockSpec((1, s, LANE), lambda b_, i, h_: (b_, 0, _R64_KIB)),
                  pl.BlockSpec((1, blk, LANE), lambda b_, i, h_: (b_, i, 0)),
                  pl.BlockSpec((1, blk, hg * LANE), lambda b_, i, h_: (b_, i, h_)),
                  pl.BlockSpec((1, s, hg * LANE), lambda b_, i, h_: (b_, 0, h // hg + h_)),
                  pl.BlockSpec((1, s, hg * LANE), lambda b_, i, h_: (b_, 0, 2 * (h // hg) + h_))],
        out_specs=pl.BlockSpec((1, blk, hg * LANE), lambda b_, i, h_: (b_, i, h_)),
        out_shape=jax.ShapeDtypeStruct((b, s, h * HEAD_DIM), _CD),
        scratch_shapes=[pltpu.VMEM((s // blk, blk, blk), F32),
                        pltpu.VMEM((1, blk), jnp.int32)] + _attn_scratch(blk, hg),
        compiler_params=_params("parallel", "arbitrary", "arbitrary"),
        name="dsa_attention",
    )(r64, r64, r64, w_idx, qkv, qkv, qkv)


def _moba_body(q_ref, k_ref, v_ref, o_ref, kmean_scr, pick_scr, *scr, blk, hg, seq, nbp):
    i = pl.program_id(2)
    nper = blk // MOBA_BLOCK
    shift = MOBA_BLOCK.bit_length() - 1
    s_scr, heads = _head_state(scr, hg)

    @pl.when(i == 0)
    def _block_means():
        r = lax.broadcasted_iota(jnp.int32, (nbp, seq), 0)
        c = lax.broadcasted_iota(jnp.int32, (nbp, seq), 1)
        avg = jnp.where((c >> shift) == r, 1.0 / MOBA_BLOCK, 0.0).astype(k_ref.dtype)
        for g in range(hg):
            kmean_scr[g] = jnp.dot(avg, k_ref[0, :, g * LANE:(g + 1) * LANE],
                                   preferred_element_type=F32).astype(kmean_scr.dtype)

    qts = [_key_major(q_ref[0, :, g * LANE:(g + 1) * LANE]) for g in range(hg)]
    kblk = lax.broadcasted_iota(jnp.int32, (nbp, blk), 0)
    own = (i * blk + lax.broadcasted_iota(jnp.int32, (nbp, blk), 1)) >> shift
    kblk_f = kblk.astype(F32)
    for g, st in enumerate(heads):
        _attn_init(*st)
        gate = jnp.where(kblk < own, jnp.dot(kmean_scr[g], qts[g], preferred_element_type=F32), _NEG)
        pick = jnp.zeros((nbp, blk), F32)
        for _ in range(MOBA_TOPK):
            best = jnp.max(gate, axis=0, keepdims=True)
            hit = (gate == best) & (best > 0.5 * _NEG)
            first = jnp.min(jnp.where(hit, kblk_f, float(nbp)), axis=0, keepdims=True)
            new = kblk_f == first
            pick = jnp.where(new, 1.0, pick)
            gate = jnp.where(new, _NEG, gate)
        pick_scr[g] = jnp.where((pick > 0.5) | (kblk == own), 0.0, _NEG)

    def scores(j, slot):
        for g in range(hg):
            for u in range(nper):
                rows = pl.ds(pl.multiple_of(j * blk + u * MOBA_BLOCK, MOBA_BLOCK), MOBA_BLOCK)
                s = jnp.dot(k_ref[0, rows, g * LANE:(g + 1) * LANE], qts[g], preferred_element_type=F32)
                s_scr[slot, g, u * MOBA_BLOCK:(u + 1) * MOBA_BLOCK, :] = s + pick_scr[g, pl.ds(j * nper + u, 1), :]

    def consume(j, slot, masked):
        rows = _tile_rows(j, blk)
        for g, st in enumerate(heads):
            s = s_scr[slot, g]
            _attn_step(_causal(s) if masked else s, v_ref[0, rows, g * LANE:(g + 1) * LANE], *st)

    _pipelined_key_tiles(i, scores, consume)
    for g, (_, l_scr, acc_scr) in enumerate(heads):
        o_ref[0, :, g * LANE:(g + 1) * LANE] = _attn_out(l_scr, acc_scr, o_ref.dtype)


def _moba_attention(qk, v, *, blk, q_off, k_off, v_off, hg=_HEADS_PER_STEP):
    b, s, _ = qk.shape
    h = MOBA_HEADS
    assert s % MOBA_BLOCK == 0 and blk % MOBA_BLOCK == 0
    nbp = -(-(s // MOBA_BLOCK) // SUBLANE) * SUBLANE
    return pl.pallas_call(
        functools.partial(_moba_body, blk=blk, hg=hg, seq=s, nbp=nbp),
        grid=(b, h // hg, s // blk),
        in_specs=[pl.BlockSpec((1, blk, hg * LANE), lambda b_, h_, i: (b_, i, q_off // hg + h_)),
                  pl.BlockSpec((1, s, hg * LANE), lambda b_, h_, i: (b_, 0, k_off // hg + h_)),
                  pl.BlockSpec((1, s, hg * LANE), lambda b_, h_, i: (b_, 0, v_off // hg + h_))],
        out_specs=pl.BlockSpec((1, blk, hg * LANE), lambda b_, h_, i: (b_, i, h_)),
        out_shape=jax.ShapeDtypeStruct((b, s, h * HEAD_DIM), _CD),
        scratch_shapes=[pltpu.VMEM((hg, nbp, HEAD_DIM), _CD),
                        pltpu.VMEM((hg, nbp, blk), F32)] + _attn_scratch(blk, hg),
        compiler_params=_params("parallel", "parallel", "arbitrary"),
        name="moba_attention",
    )(qk, qk, v)


def _dilated_body(q_ref, k_ref, v_ref, bias_ref, o_ref, *scr, blk, hg, nrel):
    i = pl.program_id(2)
    s_scr, heads = _head_state(scr, hg)
    for st in heads:
        _attn_init(*st)
    qts = [_key_major(q_ref[0, :, g * LANE:(g + 1) * LANE]) for g in range(hg)]

    def scores(j, slot):
        rows = _tile_rows(j, blk)
        bias = bias_ref[i - j]
        for g in range(hg):
            s_scr[slot, g] = bias + jnp.dot(k_ref[0, rows, g * LANE:(g + 1) * LANE], qts[g],
                                            preferred_element_type=F32)

    def consume(j, slot, masked):
        rows = _tile_rows(j, blk)
        for g, st in enumerate(heads):
            _attn_step(s_scr[slot, g], v_ref[0, rows, g * LANE:(g + 1) * LANE], *st)

    _pipelined_key_tiles(i, scores, consume, first=jnp.maximum(i - (nrel - 1), 0))
    for g, (_, l_scr, acc_scr) in enumerate(heads):
        o_ref[0, :, g * LANE:(g + 1) * LANE] = _attn_out(l_scr, acc_scr, o_ref.dtype)


def _dilated_bias(blk):
    reach = max(w for w, _ in DIL_PATTERNS)
    nrel = -(-reach // blk) + 1
    rel = jnp.arange(nrel, dtype=jnp.int32)[:, None, None]
    k = jnp.arange(blk, dtype=jnp.int32)[None, :, None]
    q = jnp.arange(blk, dtype=jnp.int32)[None, None, :]
    d = rel * blk + q - k
    mult = jnp.zeros(d.shape, F32)
    for window, dil in DIL_PATTERNS:
        mult = mult + ((d >= 0) & (d <= (window // dil) * dil) & (d % dil == 0)).astype(F32)
    return jnp.where(mult > 0, jnp.log2(jnp.maximum(mult, 1.0)), _NEG), nrel


def _dilated_attention(qk, v, *, blk, q_off, k_off, v_off, hg=_HEADS_PER_STEP):
    b, s, _ = qk.shape
    h = DIL_HEADS
    bias, nrel = _dilated_bias(blk)
    return pl.pallas_call(
        functools.partial(_dilated_body, blk=blk, hg=hg, nrel=nrel),
        grid=(b, h // hg, s // blk),
        in_specs=[pl.BlockSpec((1, blk, hg * LANE), lambda b_, h_, i: (b_, i, q_off // hg + h_)),
                  pl.BlockSpec((1, s, hg * LANE), lambda b_, h_, i: (b_, 0, k_off // hg + h_)),
                  pl.BlockSpec((1, s, hg * LANE), lambda b_, h_, i: (b_, 0, v_off // hg + h_)),
                  pl.BlockSpec((nrel, blk, blk), lambda b_, h_, i: (0, 0, 0))],
        out_specs=pl.BlockSpec((1, blk, hg * LANE), lambda b_, h_, i: (b_, i, h_)),
        out_shape=jax.ShapeDtypeStruct((b, s, h * HEAD_DIM), _CD),
        scratch_shapes=_attn_scratch(blk, hg),
        compiler_params=_params("parallel", "parallel", "arbitrary"),
        name="dilated_attention",
    )(qk, qk, v, bias)


_E_CQ, _E_CKV, _E_KPE, _E_Q, _E_K, _E_V, _E_QI, _E_KI, _E_WI = [
    int(o) for o in np.cumsum([0, MLA_Q_RANK, MLA_KV_RANK, MLA_ROPE, DSA_HEADS * HEAD_DIM, DSA_HEADS * HEAD_DIM,
                               DSA_HEADS * HEAD_DIM, IDX_HEADS * IDX_DIM, IDX_DIM])]
_R64_KIA, _R64_KIB, _R64_KPE, _R64_SLABS = 8, 9, 10, 12


def _take_cols(w, idx):
    idx = np.asarray(idx)
    cols = jnp.take(w, jnp.asarray(np.maximum(idx, 0)), axis=1)
    return jnp.where(jnp.asarray(idx >= 0)[None, :], cols, 0.0).astype(_CD)


def _r64_columns():
    half = IDX_DIM // 2
    a = np.arange(half)
    z = -np.ones(half, np.int64)
    cols = []
    for p in range(IDX_HEADS // 2):
        ha, hb = _E_QI + 2 * p * IDX_DIM, _E_QI + (2 * p + 1) * IDX_DIM
        cols += [ha + a, hb + a, ha + half + a, hb + half + a]
    cols += [_E_KI + a, z, _E_KI + half + a, z]
    cols += [z, _E_KI + a, z, _E_KI + half + a]
    cols += [_E_KPE + a, z, _E_KPE + half + a, z]
    cols += [z, z, z, z]
    return np.concatenate(cols)


def _uq_columns():
    half = MLA_ROPE // 2
    a = np.arange(half)
    z = -np.ones(half, np.int64)
    cols = []
    for h in range(MLA_HEADS):
        o = h * (MLA_NOPE + MLA_ROPE)
        cols += [o + np.arange(MLA_NOPE), o + MLA_NOPE + a, z, o + MLA_NOPE + half + a, z]
    return np.concatenate(cols)


def _ukv_columns():
    per = MLA_NOPE + MLA_V
    kn = [h * per + np.arange(MLA_NOPE) for h in range(MLA_HEADS)]
    vv = [h * per + MLA_NOPE + np.arange(MLA_V) for h in range(MLA_HEADS)]
    return np.concatenate(kn + vv)


def _rope_tables(seq, dim, scales, with_identity=False):
    inv = ROPE_THETA ** (-jnp.arange(0, dim, 2, dtype=F32) / dim)
    ang = jnp.arange(seq, dtype=F32)[:, None] * inv[None, :]
    reps = (LANE // 2) // (dim // 2)
    cos = jnp.tile(jnp.cos(ang), (1, 2 * reps))
    sin = jnp.tile(jnp.sin(ang), (1, reps))
    sin = jnp.concatenate([-sin, sin], axis=1)
    sc = jnp.asarray(scales, F32)[:, None, None]
    cos, sin = cos[None] * sc, sin[None] * sc
    if with_identity:
        cos = jnp.concatenate([cos, jnp.ones((1, seq, LANE), F32)], axis=0)
        sin = jnp.concatenate([sin, jnp.zeros((1, seq, LANE), F32)], axis=0)
    return cos, sin


def _mlp_block(x2, g, w1, w2, *, tm):
    tn = _PROJ_COLS
    up = _norm_matmul(x2, g, _to_mxu_dtype(*w1), tm=tm, tn=2 * tn, out_dtype=_CD, epilogue=_ep_relu2, name="mlp_up")
    res = pl.BlockSpec((tm, tn), lambda i, j, k: (i, j))
    return _matmul(up, _to_mxu_dtype(*w2), tm=tm, tn=tn, tk=2 * tn, out_dtype=F32,
                   epilogue=_ep_residual, extra=(x2,), extra_specs=(res,), name="mlp_down")


def _out_proj_body(a_ref, b_ref, w_ref, r_ref, o_ref):
    half = a_ref.shape[1]
    y = jnp.dot(a_ref[...], w_ref[:half, :], preferred_element_type=F32)
    y = y + jnp.dot(b_ref[...], w_ref[half:, :], preferred_element_type=F32)
    o_ref[...] = y + r_ref[...]


def _out_proj(a, b, w_out, x2, *, tm, tn=_PROJ_COLS):
    t, half = a.shape
    d = x2.shape[1]
    return pl.pallas_call(
        _out_proj_body,
        grid=(t // tm, d // tn),
        in_specs=[pl.BlockSpec((tm, half), lambda i, j: (i, 0)),
                  pl.BlockSpec((tm, half), lambda i, j: (i, 0)),
                  pl.BlockSpec((2 * half, tn), lambda i, j: (0, j)),
                  pl.BlockSpec((tm, tn), lambda i, j: (i, j))],
        out_specs=pl.BlockSpec((tm, tn), lambda i, j: (i, j)),
        out_shape=jax.ShapeDtypeStruct((t, d), F32),
        compiler_params=_params("parallel", "parallel"),
        name="out_proj",
    )(a, b, _to_mxu_dtype(*w_out), x2)


def _even_mixer(x2, g_mix, w_in, g_q, g_kv, w_uq, w_ukv, w_out, *, batch, seq, blk):
    t, d = x2.shape
    tm = min(_PROJ_ROWS, seq)
    mla_scale = _LOG2E * (MLA_NOPE + MLA_ROPE) ** -0.5
    rope64 = _rope_tables(seq, IDX_DIM, (1.0, mla_scale))
    rope128 = _rope_tables(seq, HEAD_DIM, (_LOG2E * HEAD_DIM ** -0.5, 1.0), with_identity=True)
    nh = DSA_HEADS * HEAD_DIM

    qkv, hm = _rope_matmul(x2, w_in[:, _E_Q:_E_QI].astype(_CD), rope128, lambda j: j, seq=seq, tm=tm, tn=nh,
                           pattern=(True,) * DSA_HEADS, norm_gain=g_mix, keep_h=True, name="in_proj_qkv")
    r64 = _rope_matmul(hm, _take_cols(w_in, _r64_columns()), rope64, lambda j: 0, seq=seq, tm=tm,
                       tn=_R64_SLABS // 2 * LANE, pattern=(True,) * (_R64_SLABS // 2), name="in_proj_rope64")
    w_idx = _matmul(hm, _take_cols(w_in, np.concatenate([_E_WI + np.arange(IDX_HEADS),
                                                          -np.ones(LANE - IDX_HEADS, np.int64)])),
                    tm=tm, tn=LANE, tk=d, out_dtype=F32,
                    epilogue=_ep_scale(IDX_HEADS ** -0.5 * IDX_DIM ** -0.5), name="in_proj_widx")
    ranks = (MLA_Q_RANK, MLA_KV_RANK)
    lat = _matmul(hm, w_in[:, _E_CQ:_E_KPE].astype(_CD), tm=tm, tn=sum(ranks), tk=d, out_dtype=_CD,
                  epilogue=_ep_rmsnorm_groups(ranks), extra=(jnp.concatenate([g_q, g_kv]).reshape(1, -1),),
                  extra_specs=(pl.BlockSpec((1, sum(ranks)), lambda i, j, k: (0, 0)),), name="in_proj_latents")

    q_mla = _rope_matmul(lat, _take_cols(w_uq, _uq_columns()), rope64, lambda j: 1, seq=seq,
                         tm=tm, tn=_PROJ_COLS, pattern=(False, True) * (_PROJ_COLS // (2 * LANE)),
                         plain_scale=mla_scale, name="mla_q_up")
    kv_mla = _matmul(lat, _take_cols(w_ukv, _ukv_columns()), tm=tm, tn=_PROJ_COLS, tk=MLA_KV_RANK,
                     out_dtype=_CD, epilogue=_ep_scale(1.0), a_col=MLA_Q_RANK, name="mla_kv_up")

    sh = lambda z: z.reshape(batch, seq, z.shape[-1])
    a = _mla_attention(sh(q_mla), sh(kv_mla), sh(r64), blk=blk)
    bsa = _dsa_attention(sh(r64), sh(w_idx), sh(qkv), blk=blk)
    return _out_proj(a.reshape(t, -1), bsa.reshape(t, -1), w_out, x2, tm=tm)


def _odd_mixer(x2, g_mix, w_in, w_out, *, batch, seq, blk):
    t, d = x2.shape
    tm = min(_PROJ_ROWS, seq)
    nh = MOBA_HEADS * HEAD_DIM
    rope128 = _rope_tables(seq, HEAD_DIM, (_LOG2E * HEAD_DIM ** -0.5, 1.0), with_identity=True)
    qkv = _rope_matmul(x2, _to_mxu_dtype(*w_in), rope128, lambda j: j % 3, seq=seq, tm=tm, tn=nh,
                       pattern=(True,) * MOBA_HEADS, norm_gain=g_mix, name="in_proj_odd")
    qkv = qkv.reshape(batch, seq, qkv.shape[-1])
    hs = MOBA_HEADS
    c = _moba_attention(qkv, qkv, blk=blk, q_off=0, k_off=hs, v_off=2 * hs)
    dl = _dilated_attention(qkv, qkv, blk=blk, q_off=3 * hs, k_off=4 * hs, v_off=5 * hs)
    return _out_proj(c.reshape(t, -1), dl.reshape(t, -1), w_out, x2, tm=tm)


def kernel(x, ln_mix, ln_mlp, ln_final, e_w_in, e_g_q, e_g_kv, e_w_uq, e_w_ukv, e_w_out,
           o_w_in, o_w_out, mlp_w1, mlp_w2):
    batch, seq, d = x.shape
    blk = min(_ATTN_TILE, seq)
    assert seq % blk == 0 and blk % MOBA_BLOCK == 0
    x2 = x.reshape(batch * seq, d)
    depth = ln_mix.shape[0]
    for layer in range(depth):
        j = layer // 2
        if layer % 2 == 0:
            x2 = _even_mixer(x2, ln_mix[layer], e_w_in[j], e_g_q[j], e_g_kv[j], e_w_uq[j], e_w_ukv[j],
                             (e_w_out, j), batch=batch, seq=seq, blk=blk)
        else:
            x2 = _odd_mixer(x2, ln_mix[layer], (o_w_in, j), (o_w_out, j), batch=batch, seq=seq, blk=blk)
        x2 = _mlp_block(x2, ln_mlp[layer], (mlp_w1, layer), (mlp_w2, layer), tm=min(_PROJ_ROWS, batch * seq))
    return _rmsnorm(x2, ln_final, x.dtype).reshape(batch, seq, d)
```

```python
import functools

import numpy as np
import jax
import jax.numpy as jnp
from jax import lax
from jax.experimental import pallas as pl
from jax.experimental.pallas import tpu as pltpu

HEAD_DIM = 128
ROPE_THETA = 10000.0
NORM_EPS = 1e-6
MLA_HEADS, MLA_Q_RANK, MLA_KV_RANK, MLA_NOPE, MLA_ROPE, MLA_V = 8, 512, 256, 128, 64, 128
DSA_HEADS, IDX_HEADS, IDX_DIM, DSA_TOPK_MAX = 8, 16, 64, 256
MOBA_HEADS, MOBA_BLOCK, MOBA_TOPK = 8, 256, 3
DIL_HEADS = 8
DIL_PATTERNS = ((128, 1), (512, 4), (2048, 16))

LANE = 128
SUBLANE = 8
VMEM_LIMIT_BYTES = 56 * 2**20

F32 = jnp.float32
_CD = jnp.bfloat16
_NEG = -1e30
_LOG2E = 1.4426950408889634
_INT_MAX = 2**31 - 1
_ATTN_TILE = 512
_PROJ_ROWS = 2 * _ATTN_TILE
_PROJ_COLS = 1024
_HEADS_PER_STEP = 4
_DSA_HEADS_PER_STEP = 2
_INTERP_PROBES = 40


def _params(*sem):
    return pltpu.CompilerParams(dimension_semantics=sem, vmem_limit_bytes=VMEM_LIMIT_BYTES)


def _rmsnorm_body(x_ref, g_ref, o_ref):
    x = x_ref[...].astype(F32)
    y = x * lax.rsqrt(jnp.mean(x * x, axis=-1, keepdims=True) + NORM_EPS)
    o_ref[...] = (y * g_ref[...]).astype(o_ref.dtype)


def _rmsnorm(x, g, out_dtype, tm=_ATTN_TILE):
    m, d = x.shape
    return pl.pallas_call(
        _rmsnorm_body,
        grid=(m // tm,),
        in_specs=[pl.BlockSpec((tm, d), lambda i: (i, 0)), pl.BlockSpec((1, d), lambda i: (0, 0))],
        out_specs=pl.BlockSpec((tm, d), lambda i: (i, 0)),
        out_shape=jax.ShapeDtypeStruct((m, d), out_dtype),
        compiler_params=_params("parallel"),
        name="rmsnorm",
    )(x, g.reshape(1, d).astype(F32))


def _cast_body(x_ref, o_ref):
    o_ref[...] = x_ref[...].astype(o_ref.dtype)


def _to_mxu_dtype(w, layer=None, tm=256):
    m, n = w.shape[-2:]
    if layer is None:
        spec = pl.BlockSpec((tm, n), lambda i: (i, 0))
    else:
        spec = pl.BlockSpec((None, tm, n), lambda i: (layer, i, 0))
    return pl.pallas_call(
        _cast_body,
        grid=(m // tm,),
        in_specs=[spec],
        out_specs=pl.BlockSpec((tm, n), lambda i: (i, 0)),
        out_shape=jax.ShapeDtypeStruct((m, n), _CD),
        compiler_params=_params("parallel"),
        name="weight_cast",
    )(w)


def _rope_slab(y, cos, sin):
    return y * cos + pltpu.roll(y, LANE // 2, 1) * sin


def _matmul_body(*refs, n_extra, epilogue, out_dtype):
    a_ref, w_ref = refs[0], refs[1]
    extra = refs[2:2 + n_extra]
    o_ref, acc_ref = refs[2 + n_extra], refs[3 + n_extra]
    k = pl.program_id(2)

    @pl.when(k == 0)
    def _init():
        acc_ref[...] = jnp.zeros_like(acc_ref)

    acc_ref[...] += jnp.dot(a_ref[...], w_ref[...], preferred_element_type=F32)

    @pl.when(k == pl.num_programs(2) - 1)
    def _finish():
        o_ref[...] = epilogue(acc_ref[...], *extra).astype(out_dtype)


def _matmul(a, w, *, tm, tn, tk, out_dtype, epilogue, extra=(), extra_specs=(), a_col=0, name):
    m = a.shape[0]
    kd, n = w.shape
    assert m % tm == 0 and n % tn == 0 and kd % tk == 0 and a_col % tk == 0, (a.shape, w.shape, tm, tn, tk)
    body = functools.partial(_matmul_body, n_extra=len(extra), epilogue=epilogue, out_dtype=out_dtype)
    k0 = a_col // tk
    return pl.pallas_call(
        body,
        grid=(m // tm, n // tn, kd // tk),
        in_specs=[pl.BlockSpec((tm, tk), lambda i, j, k: (i, k0 + k)),
                  pl.BlockSpec((tk, tn), lambda i, j, k: (k, j))] + list(extra_specs),
        out_specs=pl.BlockSpec((tm, tn), lambda i, j, k: (i, j)),
        out_shape=jax.ShapeDtypeStruct((m, n), out_dtype),
        scratch_shapes=[pltpu.VMEM((tm, tn), F32)],
        compiler_params=_params("parallel", "parallel", "arbitrary"),
        name=name,
    )(a, w, *extra)


def _norm_matmul_body(*refs, n_extra, epilogue, out_dtype):
    x_ref, g_ref, w_ref = refs[:3]
    extra = refs[3:3 + n_extra]
    o_ref = refs[3 + n_extra]
    h_ref = refs[4 + n_extra]

    @pl.when(pl.program_id(1) == 0)
    def _norm():
        x = x_ref[...]
        y = x * lax.rsqrt(jnp.mean(x * x, axis=-1, keepdims=True) + NORM_EPS)
        h_ref[...] = (y * g_ref[...]).astype(h_ref.dtype)

    y = jnp.dot(h_ref[...], w_ref[...], preferred_element_type=F32)
    o_ref[...] = epilogue(y, *extra).astype(out_dtype)


def _norm_matmul(x, g, w, *, tm, tn, out_dtype, epilogue, extra=(), extra_specs=(), keep_h=False, name):
    m, d = x.shape
    n = w.shape[1]
    assert m % tm == 0 and n % tn == 0, (x.shape, w.shape, tm, tn)
    body = functools.partial(_norm_matmul_body, n_extra=len(extra), epilogue=epilogue, out_dtype=out_dtype)
    y_spec, y_shape = pl.BlockSpec((tm, tn), lambda i, j: (i, j)), jax.ShapeDtypeStruct((m, n), out_dtype)
    h_spec, h_shape = pl.BlockSpec((tm, d), lambda i, j: (i, 0)), jax.ShapeDtypeStruct((m, d), _CD)
    return pl.pallas_call(
        body,
        grid=(m // tm, n // tn),
        in_specs=[pl.BlockSpec((tm, d), lambda i, j: (i, 0)),
                  pl.BlockSpec((1, d), lambda i, j: (0, 0)),
                  pl.BlockSpec((d, tn), lambda i, j: (0, j))] + list(extra_specs),
        out_specs=(y_spec, h_spec) if keep_h else y_spec,
        out_shape=(y_shape, h_shape) if keep_h else y_shape,
        scratch_shapes=[] if keep_h else [pltpu.VMEM((tm, d), _CD)],
        compiler_params=_params("parallel", "arbitrary"),
        name=name,
    )(x, g.reshape(1, d).astype(F32), w, *extra)


def _ep_scale(scale):
    def ep(y):
        return y if scale == 1.0 else y * scale
    return ep


def _ep_relu2(y):
    return jnp.square(jnp.maximum(y, 0.0))


def _ep_residual(y, r_ref):
    return y + r_ref[...]


def _ep_rmsnorm_groups(widths):
    def ep(y, g_ref):
        out, lo = [], 0
        for w in widths:
            seg = y[:, lo:lo + w]
            out.append(seg * lax.rsqrt(jnp.mean(seg * seg, axis=-1, keepdims=True) + NORM_EPS) * g_ref[:, lo:lo + w])
            lo += w
        return jnp.concatenate(out, axis=1)
    return ep


def _ep_rope(pattern, plain_scale):
    def ep(y, cos_ref, sin_ref):
        cos, sin = cos_ref[0], sin_ref[0]
        out = []
        for c, rot in enumerate(pattern):
            slab = y[:, c * LANE:(c + 1) * LANE]
            out.append(_rope_slab(slab, cos, sin) if rot else slab * plain_scale)
        return jnp.concatenate(out, axis=1)
    return ep


def _rope_matmul(a, w, tabs, tab_of_tile, *, seq, tm, tn, pattern, plain_scale=1.0, norm_gain=None,
                 keep_h=False, name):
    nblk = seq // tm
    spec = pl.BlockSpec((1, tm, LANE), lambda i, j, *_: (tab_of_tile(j), i % nblk, 0))
    common = dict(tm=tm, tn=tn, out_dtype=_CD, epilogue=_ep_rope(pattern, plain_scale), extra=tabs,
                  extra_specs=(spec, spec), name=name)
    if norm_gain is None:
        return _matmul(a, w, tk=w.shape[0], **common)
    return _norm_matmul(a, norm_gain, w, keep_h=keep_h, **common)


def _attn_init(m_scr, l_scr, acc_scr):
    m_scr[...] = jnp.full(m_scr.shape, _NEG, F32)
    l_scr[...] = jnp.zeros(l_scr.shape, F32)
    acc_scr[...] = jnp.zeros(acc_scr.shape, F32)


def _tile_rows(j, blk):
    return pl.ds(pl.multiple_of(j * blk, blk), blk)


def _dot_tn(a, b):
    return lax.dot_general(a, b, (((0,), (0,)), ((), ())), preferred_element_type=F32)


def _key_major(q):
    return q.astype(F32).T.astype(q.dtype)


def _attn_scratch(blk, hg):
    per_head = [pltpu.VMEM((1, blk), F32), pltpu.VMEM((1, blk), F32), pltpu.VMEM((HEAD_DIM, blk), F32)]
    return [pltpu.VMEM((2, hg, blk, blk), F32)] + per_head * hg


def _head_state(scr, hg):
    return scr[0], [scr[1 + 3 * g:4 + 3 * g] for g in range(hg)]


def _attn_step(st, v, m_scr, l_scr, acc_scr):
    m_prev = m_scr[...]
    m_new = jnp.maximum(m_prev, jnp.max(st, axis=0, keepdims=True))
    alpha = jnp.exp2(m_prev - m_new)
    p = jnp.exp2(st - m_new)
    l_scr[...] = alpha * l_scr[...] + jnp.sum(p, axis=0, keepdims=True)
    acc_scr[...] = alpha * acc_scr[...] + _dot_tn(v, p.astype(v.dtype))
    m_scr[...] = m_new


def _attn_out(l_scr, acc_scr, dtype):
    return (acc_scr[...] / l_scr[...]).T.astype(dtype)


def _pipelined_key_tiles(last, scores, consume, first=0):
    n_past = last - first

    def pair(t, carry):
        j = first + 2 * t
        scores(j + 1, 1)
        consume(j, 0, False)
        scores(j + 2, 0)
        consume(j + 1, 1, False)
        return carry

    scores(first, 0)
    lax.fori_loop(0, n_past // 2, pair, 0)

    @pl.when(n_past % 2 == 1)
    def _odd():
        scores(last, 1)
        consume(last - 1, 0, False)
        consume(last, 1, True)

    @pl.when(n_past % 2 == 0)
    def _even():
        consume(last, 0, True)


def _causal(s):
    kpos = lax.broadcasted_iota(jnp.int32, s.shape, 0)
    qpos = lax.broadcasted_iota(jnp.int32, s.shape, 1)
    return jnp.where(kpos <= qpos, s, _NEG)


def _mla_body(q_ref, kn_ref, kp_ref, v_ref, o_ref, *scr, blk, hg):
    i = pl.program_id(2)
    s_scr, heads = _head_state(scr, hg)
    for st in heads:
        _attn_init(*st)
    qts = [_key_major(q_ref[0, :, g * 2 * LANE:(g + 1) * 2 * LANE]) for g in range(hg)]

    def scores(j, slot):
        rows = _tile_rows(j, blk)
        kp = kp_ref[0, rows, :]
        for g in range(hg):
            k = jnp.concatenate([kn_ref[0, rows, g * LANE:(g + 1) * LANE], kp], axis=1)
            s_scr[slot, g] = jnp.dot(k, qts[g], preferred_element_type=F32)

    def consume(j, slot, masked):
        rows = _tile_rows(j, blk)
        for g, st in enumerate(heads):
            s = s_scr[slot, g]
            _attn_step(_causal(s) if masked else s, v_ref[0, rows, g * LANE:(g + 1) * LANE], *st)

    _pipelined_key_tiles(i, scores, consume)
    for g, (_, l_scr, acc_scr) in enumerate(heads):
        o_ref[0, :, g * LANE:(g + 1) * LANE] = _attn_out(l_scr, acc_scr, o_ref.dtype)


def _mla_attention(q, kv, r64, *, blk, hg=_HEADS_PER_STEP):
    b, s, _ = q.shape
    h = MLA_HEADS
    return pl.pallas_call(
        functools.partial(_mla_body, blk=blk, hg=hg),
        grid=(b, h // hg, s // blk),
        in_specs=[pl.BlockSpec((1, blk, hg * 2 * LANE), lambda b_, h_, i: (b_, i, h_)),
                  pl.BlockSpec((1, s, hg * LANE), lambda b_, h_, i: (b_, 0, h_)),
                  pl.BlockSpec((1, s, LANE), lambda b_, h_, i: (b_, 0, _R64_KPE)),
                  pl.BlockSpec((1, s, hg * LANE), lambda b_, h_, i: (b_, 0, h // hg + h_))],
        out_specs=pl.BlockSpec((1, blk, hg * LANE), lambda b_, h_, i: (b_, i, h_)),
        out_shape=jax.ShapeDtypeStruct((b, s, h * MLA_V), _CD),
        scratch_shapes=_attn_scratch(blk, hg),
        compiler_params=_params("parallel", "parallel", "arbitrary"),
        name="mla_attention",
    )(q, kv, r64, kv)


def _order_key(x):
    bits = lax.bitcast_convert_type(x, jnp.int32)
    return bits ^ ((bits >> 31) & _INT_MAX)


def _key_score(key):
    return lax.bitcast_convert_type(key ^ ((key >> 31) & _INT_MAX), F32)


def _dsa_select(i, qi_ref, ka_ref, kb_ref, w_ref, sel_scr, jc_scr, *, blk, topk, seq):
    wt = w_ref[0].T
    qits = [_key_major(qi_ref[0, :, p * LANE:(p + 1) * LANE]) for p in range(IDX_HEADS // 2)]
    kloc = lax.broadcasted_iota(jnp.int32, (blk, blk), 0)
    qpos = i * blk + lax.broadcasted_iota(jnp.int32, (blk, blk), 1)

    def score_tile(c, carry):
        rows = _tile_rows(c, blk)
        ka, kb = ka_ref[0, rows, :], kb_ref[0, rows, :]
        acc = jnp.zeros((blk, blk), F32)
        for p in range(IDX_HEADS // 2):
            da = jnp.dot(ka, qits[p], preferred_element_type=F32)
            db = jnp.dot(kb, qits[p], preferred_element_type=F32)
            acc = acc + wt[2 * p:2 * p + 1, :] * jnp.maximum(da, 0.0) + wt[2 * p + 1:2 * p + 2, :] * jnp.maximum(db, 0.0)
        causal = c * blk + kloc <= qpos
        sel_scr[c] = jnp.where(causal, acc, -jnp.inf)
        lo8, hi8 = carry
        fold = lambda x: x.reshape(blk // SUBLANE, SUBLANE, blk)
        lo8 = jnp.minimum(lo8, jnp.min(fold(jnp.where(causal, acc, jnp.inf)), axis=0))
        hi8 = jnp.maximum(hi8, jnp.max(fold(jnp.where(causal, acc, -jnp.inf)), axis=0))
        return lo8, hi8

    lo8, hi8 = lax.fori_loop(0, i + 1, score_tile, (jnp.full((SUBLANE, blk), jnp.inf, F32),
                                                    jnp.full((SUBLANE, blk), -jnp.inf, F32)))
    s_min, s_max = lo8[0:1], hi8[0:1]
    for r in range(1, SUBLANE):
        s_min = jnp.minimum(s_min, lo8[r:r + 1])
        s_max = jnp.maximum(s_max, hi8[r:r + 1])

    n_causal = i * blk + lax.broadcasted_iota(jnp.int32, (1, blk), 1) + 1
    kk = jnp.minimum(topk, n_causal).astype(F32)

    def count(pred):
        def body(c, cnt):
            hit = jnp.where(pred(sel_scr[c], c * blk + kloc), 1.0, 0.0)
            return cnt + jnp.sum(hit.reshape(blk // SUBLANE, SUBLANE, blk), axis=0)
        cnt = lax.fori_loop(0, i + 1, body, jnp.zeros((SUBLANE, blk), F32))
        return jnp.sum(cnt, axis=0, keepdims=True)

    def unfinished(lo, hi, n_lo):
        return (n_lo > kk) & (hi > lo + 1)

    def probe(state):
        lo, hi, n_lo, n_hi, w_lo, w_hi, last, it, _ = state
        lo_f, hi_f = _key_score(lo), _key_score(hi)
        a, b = (n_lo - kk + 0.5) * w_lo, (kk - 0.5 - n_hi) * w_hi
        interp = _order_key(lo_f + (hi_f - lo_f) * (a / (a + b)))
        mid = (lo >> 1) + (hi >> 1) + (lo & hi & 1)
        cand = jnp.clip(jnp.where(it < _INTERP_PROBES, interp, mid), lo + 1, hi - 1)
        cand_f = _key_score(cand)
        n = count(lambda s, kpos: s >= cand_f)
        live = unfinished(lo, hi, n_lo)
        up = live & (n >= kk)
        down = live & (n < kk)
        lo, n_lo = jnp.where(up, cand, lo), jnp.where(up, n, n_lo)
        hi, n_hi = jnp.where(down, cand, hi), jnp.where(down, n, n_hi)
        w_hi = jnp.where(up, jnp.where(last > 0.5, 0.5 * w_hi, 1.0), jnp.where(down, 1.0, w_hi))
        w_lo = jnp.where(down, jnp.where(last < -0.5, 0.5 * w_lo, 1.0), jnp.where(up, 1.0, w_lo))
        last = jnp.where(up, 1.0, jnp.where(down, -1.0, last))
        more = jnp.max(jnp.where(unfinished(lo, hi, n_lo), 1.0, 0.0))
        return lo, hi, n_lo, n_hi, w_lo, w_hi, last, it + 1, more

    n_all = n_causal.astype(F32)
    one, zero = jnp.ones((1, blk), F32), jnp.zeros((1, blk), F32)
    margin = 0.125 * (jnp.abs(s_min) + jnp.abs(s_max))
    key_min, key_end = _order_key(s_min - margin) - 1, _order_key(s_max) + 1
    start = (key_min, key_end, n_all, zero, one, one, zero, jnp.int32(0),
             jnp.max(jnp.where(unfinished(key_min, key_end, n_all), 1.0, 0.0)))
    t_key, _, n_ge = lax.while_loop(lambda s: s[8] > 0.5, probe, start)[:3]
    t = _key_score(t_key)
    jc_scr[...] = jnp.full((1, blk), seq, jnp.int32)

    @pl.when(jnp.max(n_ge - kk) > 0.5)
    def _ties():
        need = kk - count(lambda s, kpos: s > t)
        nbits = (seq - 1).bit_length()

        def pos_step(b, jc):
            cand = jc + (jnp.int32(1) << (nbits - 1 - b))
            n_lt = count(lambda s, kpos: (s == t) & (kpos < cand))
            return jnp.where(n_lt < need, cand, jc)

        jc_scr[...] = lax.fori_loop(0, nbits, pos_step, jnp.zeros((1, blk), jnp.int32))

    jc = jc_scr[...]

    def to_bias(c, carry):
        s = sel_scr[c]
        chosen = (s > t) | ((s == t) & (c * blk + kloc <= jc))
        sel_scr[c] = jnp.where(chosen, 0.0, _NEG).astype(F32)
        return carry

    lax.fori_loop(0, i + 1, to_bias, 0)


def _dsa_body(qi_ref, ka_ref, kb_ref, w_ref, q_ref, k_ref, v_ref, o_ref, sel_scr, jc_scr, *scr,
              blk, hg, topk, seq):
    i = pl.program_id(1)

    @pl.when(pl.program_id(2) == 0)
    def _select():
        _dsa_select(i, qi_ref, ka_ref, kb_ref, w_ref, sel_scr, jc_scr, blk=blk, topk=topk, seq=seq)

    s_scr, heads = _head_state(scr, hg)
    for st in heads:
        _attn_init(*st)
    qts = [_key_major(q_ref[0, :, g * LANE:(g + 1) * LANE]) for g in range(hg)]

    def scores(j, slot):
        rows = _tile_rows(j, blk)
        bias = sel_scr[j]
        for g in range(hg):
            s_scr[slot, g] = bias + jnp.dot(k_ref[0, rows, g * LANE:(g + 1) * LANE], qts[g],
                                            preferred_element_type=F32)

    def consume(j, slot, masked):
        rows = _tile_rows(j, blk)
        for g, st in enumerate(heads):
            _attn_step(s_scr[slot, g], v_ref[0, rows, g * LANE:(g + 1) * LANE], *st)

    _pipelined_key_tiles(i, scores, consume)
    for g, (_, l_scr, acc_scr) in enumerate(heads):
        o_ref[0, :, g * LANE:(g + 1) * LANE] = _attn_out(l_scr, acc_scr, o_ref.dtype)


def _dsa_attention(r64, w_idx, qkv, *, blk, hg=_DSA_HEADS_PER_STEP):
    b, s, _ = qkv.shape
    h = DSA_HEADS
    topk = min(DSA_TOPK_MAX, s // 4)
    return pl.pallas_call(
        functools.partial(_dsa_body, blk=blk, hg=hg, topk=topk, seq=s),
        grid=(b, s // blk, h // hg),
        in_specs=[pl.BlockSpec((1, blk, 8 * LANE), lambda b_, i, h_: (b_, i, 0)),
                  pl.BlockSpec((1, s, LANE), lambda b_, i, h_: (b_, 0, _R64_KIA)),
                  pl.BlockSpec((1, s, LANE), lambda b_, i, h_: (b_, 0, _R64_KIB)),
                  pl.BlockSpec((1, blk, LANE), lambda b_, i, h_: (b_, i, 0)),
                  pl.BlockSpec((1, blk, hg * LANE), lambda b_, i, h_: (b_, i, h_)),
                  pl.BlockSpec((1, s, hg * LANE), lambda b_, i, h_: (b_, 0, h // hg + h_)),
                  pl.BlockSpec((1, s, hg * LANE), lambda b_, i, h_: (b_, 0, 2 * (h // hg) + h_))],
        out_specs=pl.BlockSpec((1, blk, hg * LANE), lambda b_, i, h_: (b_, i, h_)),
        out_shape=jax.ShapeDtypeStruct((b, s, h * HEAD_DIM), _CD),
        scratch_shapes=[pltpu.VMEM((s // blk, blk, blk), F32),
                        pltpu.VMEM((1, blk), jnp.int32)] + _attn_scratch(blk, hg),
        compiler_params=_params("parallel", "arbitrary", "arbitrary"),
        name="dsa_attention",
    )(r64, r64, r64, w_idx, qkv, qkv, qkv)


def _moba_body(q_ref, k_ref, v_ref, o_ref, kmean_scr, pick_scr, *scr, blk, hg, seq, nbp):
    i = pl.program_id(2)
    nper = blk // MOBA_BLOCK
    shift = MOBA_BLOCK.bit_length() - 1
    s_scr, heads = _head_state(scr, hg)

    @pl.when(i == 0)
    def _block_means():
        r = lax.broadcasted_iota(jnp.int32, (nbp, seq), 0)
        c = lax.broadcasted_iota(jnp.int32, (nbp, seq), 1)
        avg = jnp.where((c >> shift) == r, 1.0 / MOBA_BLOCK, 0.0).astype(k_ref.dtype)
        for g in range(hg):
            kmean_scr[g] = jnp.dot(avg, k_ref[0, :, g * LANE:(g + 1) * LANE],
                                   preferred_element_type=F32).astype(kmean_scr.dtype)

    qts = [_key_major(q_ref[0, :, g * LANE:(g + 1) * LANE]) for g in range(hg)]
    kblk = lax.broadcasted_iota(jnp.int32, (nbp, blk), 0)
    own = (i * blk + lax.broadcasted_iota(jnp.int32, (nbp, blk), 1)) >> shift
    kblk_f = kblk.astype(F32)
    for g, st in enumerate(heads):
        _attn_init(*st)
        gate = jnp.where(kblk < own, jnp.dot(kmean_scr[g], qts[g], preferred_element_type=F32), _NEG)
        pick = jnp.zeros((nbp, blk), F32)
        for _ in range(MOBA_TOPK):
            best = jnp.max(gate, axis=0, keepdims=True)
            hit = (gate == best) & (best > 0.5 * _NEG)
            first = jnp.min(jnp.where(hit, kblk_f, float(nbp)), axis=0, keepdims=True)
            new = kblk_f == first
            pick = jnp.where(new, 1.0, pick)
            gate = jnp.where(new, _NEG, gate)
        pick_scr[g] = jnp.where((pick > 0.5) | (kblk == own), 0.0, _NEG)

    def scores(j, slot):
        for g in range(hg):
            for u in range(nper):
                rows = pl.ds(pl.multiple_of(j * blk + u * MOBA_BLOCK, MOBA_BLOCK), MOBA_BLOCK)
                s = jnp.dot(k_ref[0, rows, g * LANE:(g + 1) * LANE], qts[g], preferred_element_type=F32)
                s_scr[slot, g, u * MOBA_BLOCK:(u + 1) * MOBA_BLOCK, :] = s + pick_scr[g, pl.ds(j * nper + u, 1), :]

    def consume(j, slot, masked):
        rows = _tile_rows(j, blk)
        for g, st in enumerate(heads):
            s = s_scr[slot, g]
            _attn_step(_causal(s) if masked else s, v_ref[0, rows, g * LANE:(g + 1) * LANE], *st)

    _pipelined_key_tiles(i, scores, consume)
    for g, (_, l_scr, acc_scr) in enumerate(heads):
        o_ref[0, :, g * LANE:(g + 1) * LANE] = _attn_out(l_scr, acc_scr, o_ref.dtype)


def _moba_attention(qk, v, *, blk, q_off, k_off, v_off, hg=_HEADS_PER_STEP):
    b, s, _ = qk.shape
    h = MOBA_HEADS
    assert s % MOBA_BLOCK == 0 and blk % MOBA_BLOCK == 0
    nbp = -(-(s // MOBA_BLOCK) // SUBLANE) * SUBLANE
    return pl.pallas_call(
        functools.partial(_moba_body, blk=blk, hg=hg, seq=s, nbp=nbp),
        grid=(b, h // hg, s // blk),
        in_specs=[pl.BlockSpec((1, blk, hg * LANE), lambda b_, h_, i: (b_, i, q_off // hg + h_)),
                  pl.BlockSpec((1, s, hg * LANE), lambda b_, h_, i: (b_, 0, k_off // hg + h_)),
                  pl.BlockSpec((1, s, hg * LANE), lambda b_, h_, i: (b_, 0, v_off // hg + h_))],
        out_specs=pl.BlockSpec((1, blk, hg * LANE), lambda b_, h_, i: (b_, i, h_)),
        out_shape=jax.ShapeDtypeStruct((b, s, h * HEAD_DIM), _CD),
        scratch_shapes=[pltpu.VMEM((hg, nbp, HEAD_DIM), _CD),
                        pltpu.VMEM((hg, nbp, blk), F32)] + _attn_scratch(blk, hg),
        compiler_params=_params("parallel", "parallel", "arbitrary"),
        name="moba_attention",
    )(qk, qk, v)


def _dilated_body(q_ref, k_ref, v_ref, bias_ref, o_ref, *scr, blk, hg, nrel):
    i = pl.program_id(2)
    s_scr, heads = _head_state(scr, hg)
    for st in heads:
        _attn_init(*st)
    qts = [_key_major(q_ref[0, :, g * LANE:(g + 1) * LANE]) for g in range(hg)]

    def scores(j, slot):
        rows = _tile_rows(j, blk)
        bias = bias_ref[i - j]
        for g in range(hg):
            s_scr[slot, g] = bias + jnp.dot(k_ref[0, rows, g * LANE:(g + 1) * LANE], qts[g],
                                            preferred_element_type=F32)

    def consume(j, slot, masked):
        rows = _tile_rows(j, blk)
        for g, st in enumerate(heads):
            _attn_step(s_scr[slot, g], v_ref[0, rows, g * LANE:(g + 1) * LANE], *st)

    _pipelined_key_tiles(i, scores, consume, first=jnp.maximum(i - (nrel - 1), 0))
    for g, (_, l_scr, acc_scr) in enumerate(heads):
        o_ref[0, :, g * LANE:(g + 1) * LANE] = _attn_out(l_scr, acc_scr, o_ref.dtype)


def _dilated_bias(blk):
    reach = max(w for w, _ in DIL_PATTERNS)
    nrel = -(-reach // blk) + 1
    rel = jnp.arange(nrel, dtype=jnp.int32)[:, None, None]
    k = jnp.arange(blk, dtype=jnp.int32)[None, :, None]
    q = jnp.arange(blk, dtype=jnp.int32)[None, None, :]
    d = rel * blk + q - k
    mult = jnp.zeros(d.shape, F32)
    for window, dil in DIL_PATTERNS:
        mult = mult + ((d >= 0) & (d <= (window // dil) * dil) & (d % dil == 0)).astype(F32)
    return jnp.where(mult > 0, jnp.log2(jnp.maximum(mult, 1.0)), _NEG), nrel


def _dilated_attention(qk, v, *, blk, q_off, k_off, v_off, hg=_HEADS_PER_STEP):
    b, s, _ = qk.shape
    h = DIL_HEADS
    bias, nrel = _dilated_bias(blk)
    return pl.pallas_call(
        functools.partial(_dilated_body, blk=blk, hg=hg, nrel=nrel),
        grid=(b, h // hg, s // blk),
        in_specs=[pl.BlockSpec((1, blk, hg * LANE), lambda b_, h_, i: (b_, i, q_off // hg + h_)),
                  pl.BlockSpec((1, s, hg * LANE), lambda b_, h_, i: (b_, 0, k_off // hg + h_)),
                  pl.BlockSpec((1, s, hg * LANE), lambda b_, h_, i: (b_, 0, v_off // hg + h_)),
                  pl.BlockSpec((nrel, blk, blk), lambda b_, h_, i: (0, 0, 0))],
        out_specs=pl.BlockSpec((1, blk, hg * LANE), lambda b_, h_, i: (b_, i, h_)),
        out_shape=jax.ShapeDtypeStruct((b, s, h * HEAD_DIM), _CD),
        scratch_shapes=_attn_scratch(blk, hg),
        compiler_params=_params("parallel", "parallel", "arbitrary"),
        name="dilated_attention",
    )(qk, qk, v, bias)


_E_CQ, _E_CKV, _E_KPE, _E_Q, _E_K, _E_V, _E_QI, _E_KI, _E_WI = [
    int(o) for o in np.cumsum([0, MLA_Q_RANK, MLA_KV_RANK, MLA_ROPE, DSA_HEADS * HEAD_DIM, DSA_HEADS * HEAD_DIM,
                               DSA_HEADS * HEAD_DIM, IDX_HEADS * IDX_DIM, IDX_DIM])]
_R64_KIA, _R64_KIB, _R64_KPE, _R64_SLABS = 8, 9, 10, 12


def _take_cols(w, idx):
    idx = np.asarray(idx)
    cols = jnp.take(w, jnp.asarray(np.maximum(idx, 0)), axis=1)
    return jnp.where(jnp.asarray(idx >= 0)[None, :], cols, 0.0).astype(_CD)


def _r64_columns():
    half = IDX_DIM // 2
    a = np.arange(half)
    z = -np.ones(half, np.int64)
    cols = []
    for p in range(IDX_HEADS // 2):
        ha, hb = _E_QI + 2 * p * IDX_DIM, _E_QI + (2 * p + 1) * IDX_DIM
        cols += [ha + a, hb + a, ha + half + a, hb + half + a]
    cols += [_E_KI + a, z, _E_KI + half + a, z]
    cols += [z, _E_KI + a, z, _E_KI + half + a]
    cols += [_E_KPE + a, z, _E_KPE + half + a, z]
    cols += [z, z, z, z]
    return np.concatenate(cols)


def _uq_columns():
    half = MLA_ROPE // 2
    a = np.arange(half)
    z = -np.ones(half, np.int64)
    cols = []
    for h in range(MLA_HEADS):
        o = h * (MLA_NOPE + MLA_ROPE)
        cols += [o + np.arange(MLA_NOPE), o + MLA_NOPE + a, z, o + MLA_NOPE + half + a, z]
    return np.concatenate(cols)


def _ukv_columns():
    per = MLA_NOPE + MLA_V
    kn = [h * per + np.arange(MLA_NOPE) for h in range(MLA_HEADS)]
    vv = [h * per + MLA_NOPE + np.arange(MLA_V) for h in range(MLA_HEADS)]
    return np.concatenate(kn + vv)


def _rope_tables(seq, dim, scales, with_identity=False):
    inv = ROPE_THETA ** (-jnp.arange(0, dim, 2, dtype=F32) / dim)
    ang = jnp.arange(seq, dtype=F32)[:, None] * inv[None, :]
    reps = (LANE // 2) // (dim // 2)
    cos = jnp.tile(jnp.cos(ang), (1, 2 * reps))
    sin = jnp.tile(jnp.sin(ang), (1, reps))
    sin = jnp.concatenate([-sin, sin], axis=1)
    sc = jnp.asarray(scales, F32)[:, None, None]
    cos, sin = cos[None] * sc, sin[None] * sc
    if with_identity:
        cos = jnp.concatenate([cos, jnp.ones((1, seq, LANE), F32)], axis=0)
        sin = jnp.concatenate([sin, jnp.zeros((1, seq, LANE), F32)], axis=0)
    return cos, sin


def _mlp_block(x2, g, w1, w2, *, tm):
    tn = _PROJ_COLS
    up = _norm_matmul(x2, g, _to_mxu_dtype(*w1), tm=tm, tn=2 * tn, out_dtype=_CD, epilogue=_ep_relu2, name="mlp_up")
    res = pl.BlockSpec((tm, tn), lambda i, j, k: (i, j))
    return _matmul(up, _to_mxu_dtype(*w2), tm=tm, tn=tn, tk=2 * tn, out_dtype=F32,
                   epilogue=_ep_residual, extra=(x2,), extra_specs=(res,), name="mlp_down")


def _out_proj_body(a_ref, b_ref, w_ref, r_ref, o_ref):
    half = a_ref.shape[1]
    y = jnp.dot(a_ref[...], w_ref[:half, :], preferred_element_type=F32)
    y = y + jnp.dot(b_ref[...], w_ref[half:, :], preferred_element_type=F32)
    o_ref[...] = y + r_ref[...]


def _out_proj(a, b, w_out, x2, *, tm, tn=_PROJ_COLS):
    t, half = a.shape
    d = x2.shape[1]
    return pl.pallas_call(
        _out_proj_body,
        grid=(t // tm, d // tn),
        in_specs=[pl.BlockSpec((tm, half), lambda i, j: (i, 0)),
                  pl.BlockSpec((tm, half), lambda i, j: (i, 0)),
                  pl.BlockSpec((2 * half, tn), lambda i, j: (0, j)),
                  pl.BlockSpec((tm, tn), lambda i, j: (i, j))],
        out_specs=pl.BlockSpec((tm, tn), lambda i, j: (i, j)),
        out_shape=jax.ShapeDtypeStruct((t, d), F32),
        compiler_params=_params("parallel", "parallel"),
        name="out_proj",
    )(a, b, _to_mxu_dtype(*w_out), x2)


def _even_mixer(x2, g_mix, w_in, g_q, g_kv, w_uq, w_ukv, w_out, *, batch, seq, blk):
    t, d = x2.shape
    tm = min(_PROJ_ROWS, seq)
    mla_scale = _LOG2E * (MLA_NOPE + MLA_ROPE) ** -0.5
    rope64 = _rope_tables(seq, IDX_DIM, (1.0, mla_scale))
    rope128 = _rope_tables(seq, HEAD_DIM, (_LOG2E * HEAD_DIM ** -0.5, 1.0), with_identity=True)
    nh = DSA_HEADS * HEAD_DIM

    qkv, hm = _rope_matmul(x2, w_in[:, _E_Q:_E_QI].astype(_CD), rope128, lambda j: j, seq=seq, tm=tm, tn=nh,
                           pattern=(True,) * DSA_HEADS, norm_gain=g_mix, keep_h=True, name="in_proj_qkv")
    r64 = _rope_matmul(hm, _take_cols(w_in, _r64_columns()), rope64, lambda j: 0, seq=seq, tm=tm,
                       tn=_R64_SLABS // 2 * LANE, pattern=(True,) * (_R64_SLABS // 2), name="in_proj_rope64")
    w_idx = _matmul(hm, _take_cols(w_in, np.concatenate([_E_WI + np.arange(IDX_HEADS),
                                                          -np.ones(LANE - IDX_HEADS, np.int64)])),
                    tm=tm, tn=LANE, tk=d, out_dtype=F32,
                    epilogue=_ep_scale(IDX_HEADS ** -0.5 * IDX_DIM ** -0.5), name="in_proj_widx")
    ranks = (MLA_Q_RANK, MLA_KV_RANK)
    lat = _matmul(hm, w_in[:, _E_CQ:_E_KPE].astype(_CD), tm=tm, tn=sum(ranks), tk=d, out_dtype=_CD,
                  epilogue=_ep_rmsnorm_groups(ranks), extra=(jnp.concatenate([g_q, g_kv]).reshape(1, -1),),
                  extra_specs=(pl.BlockSpec((1, sum(ranks)), lambda i, j, k: (0, 0)),), name="in_proj_latents")

    q_mla = _rope_matmul(lat, _take_cols(w_uq, _uq_columns()), rope64, lambda j: 1, seq=seq,
                         tm=tm, tn=_PROJ_COLS, pattern=(False, True) * (_PROJ_COLS // (2 * LANE)),
                         plain_scale=mla_scale, name="mla_q_up")
    kv_mla = _matmul(lat, _take_cols(w_ukv, _ukv_columns()), tm=tm, tn=_PROJ_COLS, tk=MLA_KV_RANK,
                     out_dtype=_CD, epilogue=_ep_scale(1.0), a_col=MLA_Q_RANK, name="mla_kv_up")

    sh = lambda z: z.reshape(batch, seq, z.shape[-1])
    a = _mla_attention(sh(q_mla), sh(kv_mla), sh(r64), blk=blk)
    bsa = _dsa_attention(sh(r64), sh(w_idx), sh(qkv), blk=blk)
    return _out_proj(a.reshape(t, -1), bsa.reshape(t, -1), w_out, x2, tm=tm)


def _odd_mixer(x2, g_mix, w_in, w_out, *, batch, seq, blk):
    t, d = x2.shape
    tm = min(_PROJ_ROWS, seq)
    nh = MOBA_HEADS * HEAD_DIM
    rope128 = _rope_tables(seq, HEAD_DIM, (_LOG2E * HEAD_DIM ** -0.5, 1.0), with_identity=True)
    qkv = _rope_matmul(x2, _to_mxu_dtype(*w_in), rope128, lambda j: j % 3, seq=seq, tm=tm, tn=nh,
                       pattern=(True,) * MOBA_HEADS, norm_gain=g_mix, name="in_proj_odd")
    qkv = qkv.reshape(batch, seq, qkv.shape[-1])
    hs = MOBA_HEADS
    c = _moba_attention(qkv, qkv, blk=blk, q_off=0, k_off=hs, v_off=2 * hs)
    dl = _dilated_attention(qkv, qkv, blk=blk, q_off=3 * hs, k_off=4 * hs, v_off=5 * hs)
    return _out_proj(c.reshape(t, -1), dl.reshape(t, -1), w_out, x2, tm=tm)


def kernel(x, ln_mix, ln_mlp, ln_final, e_w_in, e_g_q, e_g_kv, e_w_uq, e_w_ukv, e_w_out,
           o_w_in, o_w_out, mlp_w1, mlp_w2):
    batch, seq, d = x.shape
    blk = min(_ATTN_TILE, seq)
    assert seq % blk == 0 and blk % MOBA_BLOCK == 0
    x2 = x.reshape(batch * seq, d)
    depth = ln_mix.shape[0]
    for layer in range(depth):
        j = layer // 2
        if layer % 2 == 0:
            x2 = _even_mixer(x2, ln_mix[layer], e_w_in[j], e_g_q[j], e_g_kv[j], e_w_uq[j], e_w_ukv[j],
                             (e_w_out, j), batch=batch, seq=seq, blk=blk)
        else:
            x2 = _odd_mixer(x2, ln_mix[layer], (o_w_in, j), (o_w_out, j), batch=batch, seq=seq, blk=blk)
        x2 = _mlp_block(x2, ln_mlp[layer], (mlp_w1, layer), (mlp_w2, layer), tm=min(_PROJ_ROWS, batch * seq))
    return _rmsnorm(x2, ln_final, x.dtype).reshape(batch, seq, d)
```
